```python
import jax, jax.numpy as jnp
from jax import lax
import numpy as np

D_MODEL = 1024
BATCH = 2
SEQ = 8192
DEPTH = 2

D_MIX = D_MODEL
DN_HEADS = 4
DN_HEAD_K = 128
DN_HEAD_V = 128
DN_QK = DN_HEADS * DN_HEAD_K
DN_V = DN_HEADS * DN_HEAD_V
DN_CONV_DIM = 2 * DN_QK + DN_V
CONV_K = 5
DN_CHUNK = 64
N_DIR = 2
ATT_HEADS = 8
ATT_KV_HEADS = 2
ATT_HEAD_DIM = 64
ATT_Q = ATT_HEADS * ATT_HEAD_DIM
ATT_KV = ATT_KV_HEADS * ATT_HEAD_DIM
Q_BLOCK = 128
ROPE_THETA = 10000.0
GRID_W = 64
EPS = 1e-6
IN_SIZES = (DN_CONV_DIM, DN_V, N_DIR * DN_HEADS, N_DIR * DN_HEADS, ATT_Q, ATT_KV, ATT_KV, ATT_Q)
IN_COLS = sum(IN_SIZES)

kernel_name = "hybrid_deltanet_gqa_axial_encoder"


def _rmsnorm(x, w):
    x32 = x.astype(jnp.float32)
    y = x32 * lax.rsqrt(jnp.mean(x32 * x32, axis=-1, keepdims=True) + EPS)
    return (y * w.astype(jnp.float32)).astype(x.dtype)


def _l2norm(x):
    return x * lax.rsqrt(jnp.sum(x * x, axis=-1, keepdims=True) + EPS)


def _gated_delta_rule(q, k, v, g, beta):
    B, T, H, Dk = q.shape
    Dv = v.shape[-1]
    C = DN_CHUNK
    N = T // C

    def chunk(t):
        return jnp.moveaxis(t.reshape(B, N, C, H, t.shape[-1]), 3, 2)

    qc, kc, vc = chunk(q), chunk(k), chunk(v)
    gc = jnp.cumsum(jnp.moveaxis(g.reshape(B, N, C, H), 3, 2), axis=-1)
    bc = jnp.moveaxis(beta.reshape(B, N, C, H), 3, 2)[..., None]
    incl = jnp.tril(jnp.ones((C, C), dtype=bool))
    strict = jnp.tril(jnp.ones((C, C), dtype=bool), -1)
    decay = jnp.exp(jnp.where(incl, gc[..., :, None] - gc[..., None, :], -jnp.inf))
    k_beta = kc * bc
    lower = jnp.where(strict, jnp.einsum('bnhid,bnhjd->bnhij', k_beta, kc) * decay, 0.0)
    a_mat = lower + jnp.eye(C, dtype=lower.dtype)
    rhs = jnp.concatenate([vc * bc, k_beta * jnp.exp(gc)[..., None]], axis=-1)
    sol = lax.linalg.triangular_solve(a_mat, rhs, left_side=True, lower=True, unit_diagonal=True)
    u, w = sol[..., :Dv], sol[..., Dv:]
    attn = jnp.einsum('bnhid,bnhjd->bnhij', qc, kc) * decay

    def step(state, inp):
        q_i, k_i, u_i, w_i, g_i, a_i = inp
        v_new = u_i - jnp.einsum('bhcd,bhde->bhce', w_i, state)
        o_i = (jnp.einsum('bhcd,bhde->bhce', q_i * jnp.exp(g_i)[..., None], state)
               + jnp.einsum('bhij,bhje->bhie', a_i, v_new))
        g_last = g_i[..., -1:]
        k_dec = k_i * jnp.exp(g_last - g_i)[..., None]
        state = state * jnp.exp(g_last)[..., None] + jnp.einsum('bhcd,bhce->bhde', k_dec, v_new)
        return state, o_i

    xs = tuple(jnp.moveaxis(t, 1, 0) for t in (qc, kc, u, w, gc, attn))
    s0 = jnp.zeros((B, H, Dk, Dv), dtype=q.dtype)
    _, o = lax.scan(step, s0, xs)
    return o.transpose(1, 0, 3, 2, 4).reshape(B, T, H, Dv)


def _bidir_delta(q, k, v, g, beta):
    def flip(t):
        return jnp.flip(t, axis=1)
    fwd = _gated_delta_rule(q, k, v, g[:, :, 0], beta[:, :, 0])
    bwd = flip(_gated_delta_rule(flip(q), flip(k), flip(v), flip(g[:, :, 1]), flip(beta[:, :, 1])))
    return fwd + bwd


def _deltanet_branch(qkv, z, b, a, conv_w, a_log, dt_bias, norm_w):
    Bsz, T, _ = qkv.shape
    pad = CONV_K // 2
    qkv = jax.nn.silu(lax.conv_general_dilated(
        qkv, conv_w[:, None, :], window_strides=(1,), padding=[(pad, pad)],
        dimension_numbers=('NWC', 'WIO', 'NWC'), feature_group_count=DN_CONV_DIM))
    qkv32 = qkv.astype(jnp.float32)
    q = qkv32[..., :DN_QK].reshape(Bsz, T, DN_HEADS, DN_HEAD_K)
    k = qkv32[..., DN_QK:2 * DN_QK].reshape(Bsz, T, DN_HEADS, DN_HEAD_K)
    v = qkv32[..., 2 * DN_QK:].reshape(Bsz, T, DN_HEADS, DN_HEAD_V)
    q = _l2norm(q) * (DN_HEAD_K ** -0.5)
    k = _l2norm(k)
    beta = jax.nn.sigmoid(b.astype(jnp.float32)).reshape(Bsz, T, N_DIR, DN_HEADS)
    g = -jnp.exp(a_log.astype(jnp.float32)) * jax.nn.softplus(
        a.astype(jnp.float32).reshape(Bsz, T, N_DIR, DN_HEADS) + dt_bias.astype(jnp.float32))
    o = _bidir_delta(q, k, v, g, beta)
    o = _rmsnorm(o, norm_w).reshape(Bsz, T, DN_V)
    return (o * jax.nn.silu(z.astype(jnp.float32))).astype(z.dtype)


def _rope(x, pos):
    d = x.shape[-1]
    inv = ROPE_THETA ** (-jnp.arange(0, d, 2, dtype=jnp.float32) / d)
    ang = pos[:, None] * inv[None, :]
    cos = jnp.concatenate([jnp.cos(ang), jnp.cos(ang)], axis=-1)[:, None, :]
    sin = jnp.concatenate([jnp.sin(ang), jnp.sin(ang)], axis=-1)[:, None, :]
    x32 = x.astype(jnp.float32)
    rot = jnp.concatenate([-x32[..., d // 2:], x32[..., :d // 2]], axis=-1)
    return (x32 * cos + rot * sin).astype(x.dtype)


def _axial_rope(x, row, col):
    h = x.shape[-1] // 2
    return jnp.concatenate([_rope(x[..., :h], row), _rope(x[..., h:], col)], axis=-1)


def _block_attention(q, k, v):
    B, T, Hq, Dh = q.shape
    Hkv = k.shape[2]
    G = Hq // Hkv
    NB = T // Q_BLOCK
    qb = jnp.moveaxis(q.reshape(B, NB, Q_BLOCK, Hkv, G, Dh), 1, 0)
    scale = Dh ** -0.5

    def one_block(q_blk):
        s = jnp.einsum('bqkgd,bskd->bkgqs', q_blk, k).astype(jnp.float32) * scale
        p = jax.nn.softmax(s, axis=-1).astype(v.dtype)
        return jnp.einsum('bkgqs,bskd->bqkgd', p, v)

    o = lax.map(one_block, qb)
    return jnp.moveaxis(o, 0, 1).reshape(B, T, Hq * Dh)


def _attention_branch(q, k, v, z, q_norm_w, k_norm_w, row, col):
    Bsz, T, _ = q.shape
    q = q.reshape(Bsz, T, ATT_HEADS, ATT_HEAD_DIM)
    k = k.reshape(Bsz, T, ATT_KV_HEADS, ATT_HEAD_DIM)
    v = v.reshape(Bsz, T, ATT_KV_HEADS, ATT_HEAD_DIM)
    q = _axial_rope(_rmsnorm(q, q_norm_w), row, col)
    k = _axial_rope(_rmsnorm(k, k_norm_w), row, col)
    o = _block_attention(q, k, v)
    return (o.astype(jnp.float32) * jax.nn.silu(z.astype(jnp.float32))).astype(z.dtype)


def setup_inputs(seed: int = 0) -> dict:
    key = jax.random.key(seed)
    ks = jax.random.split(key, 12)
    f32 = jnp.float32
    x = jax.random.normal(ks[0], (BATCH, SEQ, D_MODEL), f32)
    norm_w = 1.0 + 0.02 * jax.random.normal(ks[1], (DEPTH, D_MODEL), f32)
    w_in = jax.random.normal(ks[2], (DEPTH, D_MODEL, IN_COLS), f32) * D_MODEL ** -0.5
    conv_w = jax.random.normal(ks[3], (DEPTH, CONV_K, DN_CONV_DIM), f32) * CONV_K ** -0.5
    a_log = jnp.log(jax.random.uniform(ks[4], (DEPTH, N_DIR, DN_HEADS), f32, 1.0, 16.0))
    dt = jnp.exp(jax.random.uniform(ks[5], (DEPTH, N_DIR, DN_HEADS), f32, np.log(1e-3), np.log(1e-1)))
    dt_bias = dt + jnp.log(-jnp.expm1(-dt))
    dn_norm_w = 1.0 + 0.02 * jax.random.normal(ks[6], (DEPTH, DN_HEAD_V), f32)
    q_norm_w = 1.0 + 0.02 * jax.random.normal(ks[7], (DEPTH, ATT_HEAD_DIM), f32)
    k_norm_w = 1.0 + 0.02 * jax.random.normal(ks[8], (DEPTH, ATT_HEAD_DIM), f32)
    w_out = jax.random.normal(ks[9], (DEPTH, D_MIX, D_MODEL), f32) * D_MIX ** -0.5
    final_norm_w = 1.0 + 0.02 * jax.random.normal(ks[10], (D_MODEL,), f32)
    return {"x": x, "norm_w": norm_w, "w_in": w_in, "conv_w": conv_w, "a_log": a_log,
            "dt_bias": dt_bias, "dn_norm_w": dn_norm_w, "q_norm_w": q_norm_w, "k_norm_w": k_norm_w,
            "w_out": w_out, "final_norm_w": final_norm_w}


def reference(x, norm_w, w_in, conv_w, a_log, dt_bias, dn_norm_w, q_norm_w, k_norm_w, w_out, final_norm_w):
    T = x.shape[1]
    rows = T // GRID_W
    row = jnp.repeat(jnp.arange(rows, dtype=jnp.float32), GRID_W)
    col = jnp.tile(jnp.arange(GRID_W, dtype=jnp.float32), rows)
    splits = np.cumsum(IN_SIZES)[:-1].tolist()
    for l in range(DEPTH):
        h = _rmsnorm(x, norm_w[l])
        proj = jnp.einsum('btd,dc->btc', h, w_in[l])
        dn_qkv, dn_z, dn_b, dn_a, at_q, at_k, at_v, at_z = jnp.split(proj, splits, axis=-1)
        y_dn = _deltanet_branch(dn_qkv, dn_z, dn_b, dn_a, conv_w[l], a_log[l], dt_bias[l], dn_norm_w[l])
        y_at = _attention_branch(at_q, at_k, at_v, at_z, q_norm_w[l], k_norm_w[l], row, col)
        y = jnp.concatenate([y_dn, y_at], axis=-1)
        x = x + jnp.einsum('btc,cd->btd', y, w_out[l])
    return _rmsnorm(x, final_norm_w)
```

```python
import functools

import jax
import jax.numpy as jnp
import numpy as np
from jax import lax
from jax.experimental import pallas as pl
from jax.experimental.pallas import tpu as pltpu

D_MODEL = 1024
DN_HEADS = 4
DN_HEAD = 128
DN_QK = DN_HEADS * DN_HEAD
DN_CONV_DIM = 3 * DN_QK
CONV_K = 5
CHUNK = 64
N_DIR = 2
ATT_HEADS = 8
ATT_KV_HEADS = 2
ATT_GROUP = ATT_HEADS // ATT_KV_HEADS
ATT_HEAD_DIM = 64
ATT_Q = ATT_HEADS * ATT_HEAD_DIM
ATT_KV = ATT_KV_HEADS * ATT_HEAD_DIM
ROPE_THETA = 10000.0
GRID_W = 64
EPS = 1e-6
IN_SIZES = (DN_CONV_DIM, DN_QK, N_DIR * DN_HEADS, N_DIR * DN_HEADS, ATT_Q, ATT_KV, ATT_KV, ATT_Q)
LANES = 128
SUBLANES = 8
VMEM_LIMIT = 48 * 1024 * 1024
PACK_COLS = DN_CONV_DIM + DN_QK + ATT_Q + ATT_KV + ATT_KV + ATT_Q + LANES
BETA_LANE = 0
DECAY_LANE = N_DIR * DN_HEADS

F32 = jnp.float32
BF16 = jnp.bfloat16
HI = lax.Precision.HIGHEST


def _params(sem):
    return pltpu.CompilerParams(dimension_semantics=sem, vmem_limit_bytes=VMEM_LIMIT)


def _silu(x):
    return x / (1.0 + jnp.exp(-x))


def _in_proj_kernel(x_ref, nw_ref, w_ref, qkv_ref, dnz_ref, atq_ref, atk_ref, atv_ref, atz_ref, ba_ref):
    x = x_ref[...]
    h = x * lax.rsqrt(jnp.mean(x * x, axis=-1, keepdims=True) + EPS) * nw_ref[...]
    p = jnp.dot(h.astype(BF16), w_ref[...], preferred_element_type=F32)
    o = 0
    for ref in (qkv_ref, dnz_ref, atq_ref, atk_ref, atv_ref, atz_ref, ba_ref):
        n = ref.shape[-1]
        ref[...] = p[:, o:o + n]
        o += n


def _in_proj(x2, nw, w_packed, tm):
    bt = x2.shape[0]
    widths = (DN_CONV_DIM, DN_QK, ATT_Q, ATT_KV, ATT_KV, ATT_Q, LANES)
    return pl.pallas_call(
        _in_proj_kernel,
        grid=(bt // tm,),
        in_specs=[pl.BlockSpec((tm, D_MODEL), lambda i: (i, 0)),
                  pl.BlockSpec((1, D_MODEL), lambda i: (0, 0)),
                  pl.BlockSpec((D_MODEL, PACK_COLS), lambda i: (0, 0))],
        out_specs=[pl.BlockSpec((tm, n), lambda i: (i, 0)) for n in widths],
        out_shape=[jax.ShapeDtypeStruct((bt, n), F32) for n in widths],
        compiler_params=_params(("parallel",)),
        name="in_proj",
    )(x2, nw, w_packed)


def _dn_prep_kernel(x_ref, prev_ref, next_ref, ba_ref, cw_ref, alog_ref, dtb_ref,
                    q_ref, k_ref, v_ref, gb_ref, xe_ref, *, tm):
    i = pl.program_id(1)
    n = pl.num_programs(1)
    halo = SUBLANES
    xe_ref[0:halo, :] = jnp.where(i > 0, prev_ref[...], 0.0)
    xe_ref[halo:halo + tm, :] = x_ref[...]
    xe_ref[halo + tm:2 * halo + tm, :] = jnp.where(i < n - 1, next_ref[...], 0.0)
    pad = CONV_K // 2
    y = jnp.zeros((tm, DN_CONV_DIM), F32)
    for j in range(CONV_K):
        y = y + xe_ref[halo + j - pad:halo + j - pad + tm, :] * cw_ref[j:j + 1, :]
    y = _silu(y)
    for hd in range(DN_HEADS):
        lo = hd * DN_HEAD
        q = y[:, lo:lo + DN_HEAD]
        k = y[:, DN_QK + lo:DN_QK + lo + DN_HEAD]
        q_ref[:, lo:lo + DN_HEAD] = q * lax.rsqrt(jnp.sum(q * q, axis=-1, keepdims=True) + EPS) * (DN_HEAD ** -0.5)
        k_ref[:, lo:lo + DN_HEAD] = k * lax.rsqrt(jnp.sum(k * k, axis=-1, keepdims=True) + EPS)
    v_ref[...] = y[:, 2 * DN_QK:]

    ba = ba_ref[...]
    beta = 1.0 / (1.0 + jnp.exp(-ba))
    z = ba + dtb_ref[...]
    softplus = jnp.maximum(z, 0.0) + jnp.log(1.0 + jnp.exp(-jnp.abs(z)))
    g = -jnp.exp(alog_ref[...]) * softplus
    r = lax.broadcasted_iota(jnp.int32, (tm, tm), 0)
    c = lax.broadcasted_iota(jnp.int32, (tm, tm), 1)
    same = (r // CHUNK) == (c // CHUNK)
    tri_f = jnp.where(same & (c <= r), 1.0, 0.0).astype(F32)
    tri_b = jnp.where(same & (c >= r), 1.0, 0.0).astype(F32)
    gc_f = jnp.dot(tri_f, g, precision=HI, preferred_element_type=F32)
    gc_b = jnp.dot(tri_b, g, precision=HI, preferred_element_type=F32)
    lane = lax.broadcasted_iota(jnp.int32, (tm, LANES), 1)
    gc = jnp.where(lane < DECAY_LANE + DN_HEADS, gc_f, gc_b)
    gb_ref[...] = jnp.where(lane < DECAY_LANE, beta, gc)


def _dn_prep(qkv, ba, cw, alog_row, dtb_row, tm):
    b, t, _ = qkv.shape
    nb = tm // SUBLANES
    last = t // SUBLANES - 1
    kern = functools.partial(_dn_prep_kernel, tm=tm)
    return pl.pallas_call(
        kern,
        grid=(b, t // tm),
        in_specs=[pl.BlockSpec((None, tm, DN_CONV_DIM), lambda bi, i: (bi, i, 0)),
                  pl.BlockSpec((None, SUBLANES, DN_CONV_DIM), lambda bi, i: (bi, jnp.maximum(i * nb - 1, 0), 0)),
                  pl.BlockSpec((None, SUBLANES, DN_CONV_DIM), lambda bi, i: (bi, jnp.minimum((i + 1) * nb, last), 0)),
                  pl.BlockSpec((None, tm, LANES), lambda bi, i: (bi, i, 0)),
                  pl.BlockSpec((SUBLANES, DN_CONV_DIM), lambda bi, i: (0, 0)),
                  pl.BlockSpec((1, LANES), lambda bi, i: (0, 0)),
                  pl.BlockSpec((1, LANES), lambda bi, i: (0, 0))],
        out_specs=[pl.BlockSpec((None, tm, DN_QK), lambda bi, i: (bi, i, 0))] * 3
        + [pl.BlockSpec((None, tm, LANES), lambda bi, i: (bi, i, 0))],
        out_shape=[jax.ShapeDtypeStruct((b, t, DN_QK), F32)] * 3 + [jax.ShapeDtypeStruct((b, t, LANES), F32)],
        scratch_shapes=[pltpu.VMEM((tm + 2 * SUBLANES, DN_CONV_DIM), F32)],
        compiler_params=_params(("parallel", "parallel")),
        name="dn_prep",
    )(qkv, qkv, qkv, ba, cw, alog_row, dtb_row)


def _bmm(a, b):
    return jnp.einsum('cij,cjk->cik', a, b, precision=HI, preferred_element_type=F32)


def _dn_direction(d, q_ref, k_ref, v_ref, gb_ref, gr_ref, o_ref, s_ref, cb):
    ii = lax.broadcasted_iota(jnp.int32, (CHUNK, CHUNK), 0)
    jj = lax.broadcasted_iota(jnp.int32, (CHUNK, CHUNK), 1)
    if d == 1:
        ii, jj = jj, ii
    incl = ii >= jj
    strict = ii > jj
    eye = lax.broadcasted_iota(jnp.int32, (CHUNK, CHUNK), 0) == lax.broadcasted_iota(jnp.int32, (CHUNK, CHUNK), 1)
    gb = gb_ref[...]
    gr = gr_ref[...]
    per_head = []
    for hd in range(DN_HEADS):
        lo = hd * DN_HEAD
        q = q_ref[:, lo:lo + DN_HEAD].reshape(cb, CHUNK, DN_HEAD)
        k = k_ref[:, lo:lo + DN_HEAD].reshape(cb, CHUNK, DN_HEAD)
        v = v_ref[:, lo:lo + DN_HEAD].reshape(cb, CHUNK, DN_HEAD)
        bl = BETA_LANE + d * DN_HEADS + hd
        gl = DECAY_LANE + d * DN_HEADS + hd
        beta = gb[:, bl:bl + 1].reshape(cb, CHUNK, 1)
        gcol = gb[:, gl:gl + 1].reshape(cb, CHUNK, 1)
        grow = gr[:, gl:gl + 1, :]
        glast = grow[:, :, CHUNK - 1:CHUNK] if d == 0 else grow[:, :, 0:1]
        decay = jnp.where(incl, jnp.exp(jnp.minimum(gcol - grow, 0.0)), 0.0)
        kb = k * beta
        kk = jnp.einsum('cid,cjd->cij', kb, k, precision=HI, preferred_element_type=F32)
        lower = jnp.where(strict, kk * decay, 0.0)
        tinv = jnp.where(eye, 1.0, 0.0).astype(F32) - jnp.where((ii // 2 == jj // 2) & (ii % 2 == 1) & (jj % 2 == 0), lower, 0.0)
        s = 2
        while s < CHUNK:
            m = (ii // (2 * s) == jj // (2 * s)) & ((ii // s) % 2 == 1) & ((jj // s) % 2 == 0)
            e = jnp.where(m, lower, 0.0)
            tinv = tinv - _bmm(_bmm(tinv, e), tinv)
            s *= 2
        rhs = jnp.concatenate([v * beta, kb * jnp.exp(gcol)], axis=-1)
        sol = _bmm(tinv, rhs)
        u, w = sol[..., :DN_HEAD], sol[..., DN_HEAD:]
        attn = jnp.einsum('cid,cjd->cij', q, k, precision=HI, preferred_element_type=F32) * decay
        qg = q * jnp.exp(gcol)
        kdec = k * jnp.exp(glast - gcol)
        per_head.append((u, w, attn, qg, kdec, jnp.exp(glast)))
    order = range(cb) if d == 0 else range(cb - 1, -1, -1)
    for c in order:
        for hd in range(DN_HEADS):
            u, w, attn, qg, kdec, eg = per_head[hd]
            idx = d * DN_HEADS + hd
            st = s_ref[idx]
            v_new = u[c] - jnp.dot(w[c], st, precision=HI, preferred_element_type=F32)
            o = (jnp.dot(qg[c], st, precision=HI, preferred_element_type=F32)
                 + jnp.dot(attn[c], v_new, precision=HI, preferred_element_type=F32))
            s_ref[idx] = st * eg[c] + lax.dot_general(kdec[c], v_new, (((0,), (0,)), ((), ())),
                                                      precision=HI, preferred_element_type=F32)
            o_ref[c * CHUNK:(c + 1) * CHUNK, hd * DN_HEAD:(hd + 1) * DN_HEAD] = o


def _dn_main_kernel(qf, kf, vf, gbf, grf, qb, kb, vb, gbb, grb, of_ref, ob_ref, s_ref, *, cb):
    @pl.when(pl.program_id(1) == 0)
    def _():
        s_ref[...] = jnp.zeros_like(s_ref)

    _dn_direction(0, qf, kf, vf, gbf, grf, of_ref, s_ref, cb)
    _dn_direction(1, qb, kb, vb, gbb, grb, ob_ref, s_ref, cb)


def _dn_main(qn, kn, vv, gb, gr, cb):
    b, t, _ = qn.shape
    cbt = cb * CHUNK
    ng = t // cbt
    fwd = lambda bi, j: (bi, j, 0)
    bwd = lambda bi, j: (bi, ng - 1 - j, 0)
    fwd4 = lambda bi, j: (bi, j, 0, 0)
    bwd4 = lambda bi, j: (bi, ng - 1 - j, 0, 0)

    def specs(im3, im4):
        return ([pl.BlockSpec((None, cbt, DN_QK), im3)] * 3
                + [pl.BlockSpec((None, cbt, LANES), im3), pl.BlockSpec((None, cb, 2 * SUBLANES, CHUNK), im4)])

    kern = functools.partial(_dn_main_kernel, cb=cb)
    return pl.pallas_call(
        kern,
        grid=(b, ng),
        in_specs=specs(fwd, fwd4) + specs(bwd, bwd4),
        out_specs=[pl.BlockSpec((None, cbt, DN_QK), fwd), pl.BlockSpec((None, cbt, DN_QK), bwd)],
        out_shape=[jax.ShapeDtypeStruct((b, t, DN_QK), F32)] * 2,
        scratch_shapes=[pltpu.VMEM((N_DIR * DN_HEADS, DN_HEAD, DN_HEAD), F32)],
        compiler_params=_params(("parallel", "arbitrary")),
        name="dn_main",
    )(qn, kn, vv, gb, gr, qn, kn, vv, gb, gr)


def _rope_norm(x, w, seg, cos, sin):
    ss = jnp.dot(x * x, seg, precision=HI, preferred_element_type=F32)
    xn = x * lax.rsqrt(ss * (1.0 / ATT_HEAD_DIM) + EPS) * w
    quarter = ATT_HEAD_DIM // 4
    outs = []
    for t in range(x.shape[-1] // LANES):
        xt = xn[:, t * LANES:(t + 1) * LANES]
        lane = lax.broadcasted_iota(jnp.int32, xt.shape, 1)
        rot = jnp.where(lane % (2 * quarter) < quarter,
                        pltpu.roll(xt, LANES - quarter, axis=1), pltpu.roll(xt, quarter, axis=1))
        outs.append(xt * cos + rot * sin)
    return outs[0] if len(outs) == 1 else jnp.concatenate(outs, axis=-1)


def _attn_prep_kernel(q_ref, k_ref, v_ref, qw_ref, kw_ref, seg_ref, cos_ref, sin_ref, qo_ref, ko_ref, vo_ref):
    cos = cos_ref[...]
    sin = sin_ref[...]
    q = _rope_norm(q_ref[...], qw_ref[...], seg_ref[...], cos, sin)
    qo_ref[...] = (q * (ATT_HEAD_DIM ** -0.5)).astype(BF16)
    k = _rope_norm(k_ref[...], kw_ref[...], seg_ref[0:LANES, 0:LANES], cos, sin)
    ko_ref[...] = k.astype(BF16)
    vo_ref[...] = v_ref[...].astype(BF16)


def _attn_prep(atq, atk, atv, qw_row, kw_row, seg, cos_t, sin_t, tm):
    b, t, _ = atq.shape
    row = lambda bi, i: (bi, i, 0)
    const = lambda bi, i: (0, 0)
    return pl.pallas_call(
        _attn_prep_kernel,
        grid=(b, t // tm),
        in_specs=[pl.BlockSpec((None, tm, ATT_Q), row), pl.BlockSpec((None, tm, ATT_KV), row),
                  pl.BlockSpec((None, tm, ATT_KV), row),
                  pl.BlockSpec((1, ATT_Q), const), pl.BlockSpec((1, ATT_KV), const),
                  pl.BlockSpec((ATT_Q, ATT_Q), const),
                  pl.BlockSpec((tm, LANES), lambda bi, i: (i, 0)), pl.BlockSpec((tm, LANES), lambda bi, i: (i, 0))],
        out_specs=[pl.BlockSpec((None, tm, ATT_Q), row), pl.BlockSpec((None, tm, ATT_KV), row),
                   pl.BlockSpec((None, tm, ATT_KV), row)],
        out_shape=[jax.ShapeDtypeStruct((b, t, ATT_Q), BF16), jax.ShapeDtypeStruct((b, t, ATT_KV), BF16),
                   jax.ShapeDtypeStruct((b, t, ATT_KV), BF16)],
        compiler_params=_params(("parallel", "parallel")),
        name="attn_prep",
    )(atq, atk, atv, qw_row, kw_row, seg, cos_t, sin_t)


def _attn_kernel(q_ref, k_ref, vt_ref, o_ref, *, tq, tk):
    m_cols = ATT_GROUP * tq
    q = q_ref[...].reshape(m_cols, ATT_HEAD_DIM)
    nk = k_ref.shape[0] // tk

    def body(j, carry):
        m, l, acc = carry
        start = pl.multiple_of(j * tk, tk)
        k = k_ref[pl.ds(start, tk), :]
        vt = vt_ref[:, pl.ds(start, tk)]
        s = lax.dot_general(k, q, (((1,), (1,)), ((), ())), preferred_element_type=F32)
        m_new = jnp.maximum(m, jnp.max(s, axis=0, keepdims=True))
        alpha = jnp.exp(m - m_new)
        p = jnp.exp(s - m_new)
        l = alpha * l + jnp.sum(p, axis=0, keepdims=True)
        acc = alpha * acc + jnp.dot(vt, p.astype(BF16), preferred_element_type=F32)
        return m_new, l, acc

    init = (jnp.full((1, m_cols), -1e30, F32), jnp.zeros((1, m_cols), F32), jnp.zeros((ATT_HEAD_DIM, m_cols), F32))
    m, l, acc = lax.fori_loop(0, nk, body, init)
    o_ref[...] = acc / l


def _attention(q5, k4, vt4, tq, tk):
    b, _, _, t, _ = q5.shape
    nq = t // tq
    m_cols = ATT_GROUP * tq
    kern = functools.partial(_attn_kernel, tq=tq, tk=tk)
    return pl.pallas_call(
        kern,
        grid=(b, ATT_KV_HEADS, nq),
        in_specs=[pl.BlockSpec((None, None, ATT_GROUP, tq, ATT_HEAD_DIM), lambda bi, h, i: (bi, h, 0, i, 0)),
                  pl.BlockSpec((None, None, t, ATT_HEAD_DIM), lambda bi, h, i: (bi, h, 0, 0)),
                  pl.BlockSpec((None, None, ATT_HEAD_DIM, t), lambda bi, h, i: (bi, h, 0, 0))],
        out_specs=pl.BlockSpec((None, None, None, ATT_HEAD_DIM, m_cols), lambda bi, h, i: (bi, h, i, 0, 0)),
        out_shape=jax.ShapeDtypeStruct((b, ATT_KV_HEADS, nq, ATT_HEAD_DIM, m_cols), F32),
        compiler_params=_params(("parallel", "parallel", "parallel")),
        name="attention",
    )(q5, k4, vt4)


def _out_proj_kernel(of_ref, ob_ref, dnz_ref, dnw_ref, oat_ref, atz_ref, x_ref, w_ref, fw_ref, y_ref, *, final):
    o = of_ref[...] + ob_ref[...]
    parts = []
    for hd in range(DN_HEADS):
        oh = o[:, hd * DN_HEAD:(hd + 1) * DN_HEAD]
        parts.append(oh * lax.rsqrt(jnp.mean(oh * oh, axis=-1, keepdims=True) + EPS) * dnw_ref[...])
    y_dn = jnp.concatenate(parts, axis=-1) * _silu(dnz_ref[...])
    y_at = oat_ref[...] * _silu(atz_ref[...])
    y = jnp.concatenate([y_dn, y_at], axis=-1).astype(BF16)
    out = x_ref[...] + jnp.dot(y, w_ref[...], preferred_element_type=F32)
    if final:
        out = out * lax.rsqrt(jnp.mean(out * out, axis=-1, keepdims=True) + EPS) * fw_ref[...]
    y_ref[...] = out


def _out_proj(o_f, o_b, dnz, dnw_row, o_at, atz, x2, w_out, fw_row, tm, final):
    bt = x2.shape[0]
    row = lambda i: (i, 0)
    const = lambda i: (0, 0)
    kern = functools.partial(_out_proj_kernel, final=final)
    return pl.pallas_call(
        kern,
        grid=(bt // tm,),
        in_specs=[pl.BlockSpec((tm, DN_QK), row), pl.BlockSpec((tm, DN_QK), row), pl.BlockSpec((tm, DN_QK), row),
                  pl.BlockSpec((1, DN_HEAD), const),
                  pl.BlockSpec((tm, ATT_Q), row), pl.BlockSpec((tm, ATT_Q), row),
                  pl.BlockSpec((tm, D_MODEL), row),
                  pl.BlockSpec((D_MODEL, D_MODEL), const), pl.BlockSpec((1, D_MODEL), const)],
        out_specs=pl.BlockSpec((tm, D_MODEL), row),
        out_shape=jax.ShapeDtypeStruct((bt, D_MODEL), F32),
        compiler_params=_params(("parallel",)),
        name="out_proj_final" if final else "out_proj",
    )(o_f, o_b, dnz, dnw_row, o_at, atz, x2, w_out, fw_row)


def _pack_w_in(w):
    splits = np.cumsum(IN_SIZES)[:-1].tolist()
    qkv, dnz, bb, aa, atq, atk, atv, atz = jnp.split(w, splits, axis=-1)
    pad = jnp.zeros((w.shape[0], LANES - 2 * N_DIR * DN_HEADS), w.dtype)
    return jnp.concatenate([qkv, dnz, atq, atk, atv, atz, bb, aa, pad], axis=-1).astype(BF16)


def _rope_tables(t):
    half = ATT_HEAD_DIM // 2
    pos = np.arange(t)
    row = (pos // GRID_W).astype(np.float32)
    col = (pos % GRID_W).astype(np.float32)
    inv = jnp.asarray(ROPE_THETA, F32) ** (-jnp.arange(0, half, 2, dtype=F32) / half)
    ang_r = jnp.asarray(row)[:, None] * inv[None, :]
    ang_c = jnp.asarray(col)[:, None] * inv[None, :]
    cos = jnp.concatenate([jnp.cos(ang_r)] * 2 + [jnp.cos(ang_c)] * 2, axis=-1)
    sin = jnp.concatenate([-jnp.sin(ang_r), jnp.sin(ang_r), -jnp.sin(ang_c), jnp.sin(ang_c)], axis=-1)
    reps = LANES // ATT_HEAD_DIM
    return jnp.tile(cos, (1, reps)), jnp.tile(sin, (1, reps))


def _gate_row(p):
    return jnp.zeros((1, LANES), F32).at[0, DECAY_LANE:DECAY_LANE + N_DIR * DN_HEADS].set(p.reshape(-1))


def kernel(x, norm_w, w_in, conv_w, a_log, dt_bias, dn_norm_w, q_norm_w, k_norm_w, w_out, final_norm_w):
    b, t, d = x.shape
    depth = w_in.shape[0]
    tm = 256
    cb = 4
    tq = 128
    tk = 512
    cos_t, sin_t = _rope_tables(t)
    lane = np.arange(ATT_Q)
    seg = jnp.asarray((lane[:, None] // ATT_HEAD_DIM) == (lane[None, :] // ATT_HEAD_DIM), F32)
    x2 = x.reshape(b * t, d)
    for l in range(depth):
        qkv, dnz, atq, atk, atv, atz, ba = _in_proj(x2, norm_w[l][None, :], _pack_w_in(w_in[l]), tm)
        cw = jnp.zeros((SUBLANES, DN_CONV_DIM), F32).at[:CONV_K].set(conv_w[l])
        qn, kn, vv, gb = _dn_prep(qkv.reshape(b, t, -1), ba.reshape(b, t, -1), cw,
                                  _gate_row(a_log[l]), _gate_row(dt_bias[l]), tm)
        gr = gb[..., :2 * SUBLANES].reshape(b, t // CHUNK, CHUNK, 2 * SUBLANES).transpose(0, 1, 3, 2)
        o_f, o_b = _dn_main(qn, kn, vv, gb, gr, cb)
        qr, kr, vb = _attn_prep(atq.reshape(b, t, -1), atk.reshape(b, t, -1), atv.reshape(b, t, -1),
                                jnp.tile(q_norm_w[l], ATT_HEADS)[None, :], jnp.tile(k_norm_w[l], ATT_KV_HEADS)[None, :],
                                seg, cos_t, sin_t, tm)
        q5 = qr.reshape(b, t, ATT_KV_HEADS, ATT_GROUP, ATT_HEAD_DIM).transpose(0, 2, 3, 1, 4)
        k4 = kr.reshape(b, t, ATT_KV_HEADS, ATT_HEAD_DIM).transpose(0, 2, 1, 3)
        vt4 = vb.reshape(b, t, ATT_KV_HEADS, ATT_HEAD_DIM).transpose(0, 2, 3, 1)
        ot = _attention(q5, k4, vt4, tq, tk)
        o_at = ot.reshape(b, ATT_KV_HEADS, t // tq, ATT_HEAD_DIM, ATT_GROUP, tq)
        o_at = o_at.transpose(0, 2, 5, 1, 4, 3).reshape(b * t, ATT_Q)
        x2 = _out_proj(o_f.reshape(b * t, -1), o_b.reshape(b * t, -1), dnz, dn_norm_w[l][None, :], o_at, atz,
                       x2, w_out[l].astype(BF16), final_norm_w[None, :], tm, final=(l == depth - 1))
    return x2.reshape(b, t, d)
```

```python
import functools

import jax
import jax.numpy as jnp
import numpy as np
from jax import lax
from jax.experimental import pallas as pl
from jax.experimental.pallas import tpu as pltpu

D_MODEL = 1024
DN_HEADS = 4
DN_HEAD = 128
DN_QK = DN_HEADS * DN_HEAD
DN_CONV_DIM = 3 * DN_QK
CONV_K = 5
CHUNK = 64
N_DIR = 2
ATT_HEADS = 8
ATT_KV_HEADS = 2
ATT_GROUP = ATT_HEADS // ATT_KV_HEADS
ATT_HEAD_DIM = 64
ATT_Q = ATT_HEADS * ATT_HEAD_DIM
ATT_KV = ATT_KV_HEADS * ATT_HEAD_DIM
ROPE_THETA = 10000.0
GRID_W = 64
EPS = 1e-6
IN_SIZES = (DN_CONV_DIM, DN_QK, N_DIR * DN_HEADS, N_DIR * DN_HEADS, ATT_Q, ATT_KV, ATT_KV, ATT_Q)
LANES = 128
SUBLANES = 8
VMEM_LIMIT = 48 * 1024 * 1024
PACK_COLS = DN_CONV_DIM + DN_QK + ATT_Q + ATT_KV + ATT_KV + ATT_Q + LANES
BETA_LANE = 0
DECAY_LANE = N_DIR * DN_HEADS

F32 = jnp.float32
BF16 = jnp.bfloat16
HI = lax.Precision.HIGHEST
LOG2E = 1.4426950408889634


def _params(sem):
    return pltpu.CompilerParams(dimension_semantics=sem, vmem_limit_bytes=VMEM_LIMIT)


def _silu(x):
    return x / (1.0 + jnp.exp(-x))


def _in_proj_kernel(x_ref, nw_ref, w_ref, qkv_ref, dnz_ref, atq_ref, atk_ref, atv_ref, atz_ref, ba_ref):
    x = x_ref[...]
    h = x * lax.rsqrt(jnp.mean(x * x, axis=-1, keepdims=True) + EPS) * nw_ref[...]
    p = jnp.dot(h.astype(BF16), w_ref[...], preferred_element_type=F32)
    o = 0
    for ref in (qkv_ref, dnz_ref, atq_ref, atk_ref, atv_ref, atz_ref, ba_ref):
        n = ref.shape[-1]
        ref[...] = p[:, o:o + n]
        o += n


def _in_proj(x2, nw, w_packed, tm):
    bt = x2.shape[0]
    widths = (DN_CONV_DIM, DN_QK, ATT_Q, ATT_KV, ATT_KV, ATT_Q, LANES)
    return pl.pallas_call(
        _in_proj_kernel,
        grid=(bt // tm,),
        in_specs=[pl.BlockSpec((tm, D_MODEL), lambda i: (i, 0)),
                  pl.BlockSpec((1, D_MODEL), lambda i: (0, 0)),
                  pl.BlockSpec((D_MODEL, PACK_COLS), lambda i: (0, 0))],
        out_specs=[pl.BlockSpec((tm, n), lambda i: (i, 0)) for n in widths],
        out_shape=[jax.ShapeDtypeStruct((bt, n), F32) for n in widths],
        compiler_params=_params(("parallel",)),
        name="in_proj",
    )(x2, nw, w_packed)


def _dn_prep_kernel(x_ref, prev_ref, next_ref, ba_ref, cw_ref, alog_ref, dtb_ref,
                    q_ref, k_ref, v_ref, gb_ref, xe_ref, *, tm):
    i = pl.program_id(1)
    n = pl.num_programs(1)
    halo = SUBLANES
    xe_ref[0:halo, :] = jnp.where(i > 0, prev_ref[...], 0.0)
    xe_ref[halo:halo + tm, :] = x_ref[...]
    xe_ref[halo + tm:2 * halo + tm, :] = jnp.where(i < n - 1, next_ref[...], 0.0)
    pad = CONV_K // 2
    y = jnp.zeros((tm, DN_CONV_DIM), F32)
    for j in range(CONV_K):
        y = y + xe_ref[halo + j - pad:halo + j - pad + tm, :] * cw_ref[j:j + 1, :]
    y = _silu(y)
    for hd in range(DN_HEADS):
        lo = hd * DN_HEAD
        q = y[:, lo:lo + DN_HEAD]
        k = y[:, DN_QK + lo:DN_QK + lo + DN_HEAD]
        q_ref[:, lo:lo + DN_HEAD] = q * lax.rsqrt(jnp.sum(q * q, axis=-1, keepdims=True) + EPS) * (DN_HEAD ** -0.5)
        k_ref[:, lo:lo + DN_HEAD] = k * lax.rsqrt(jnp.sum(k * k, axis=-1, keepdims=True) + EPS)
    v_ref[...] = y[:, 2 * DN_QK:]

    ba = ba_ref[...]
    beta = 1.0 / (1.0 + jnp.exp(-ba))
    z = ba + dtb_ref[...]
    softplus = jnp.maximum(z, 0.0) + jnp.log(1.0 + jnp.exp(-jnp.abs(z)))
    g = -jnp.exp(alog_ref[...]) * softplus
    r = lax.broadcasted_iota(jnp.int32, (tm, tm), 0)
    c = lax.broadcasted_iota(jnp.int32, (tm, tm), 1)
    same = (r // CHUNK) == (c // CHUNK)
    tri_f = jnp.where(same & (c <= r), 1.0, 0.0).astype(F32)
    tri_b = jnp.where(same & (c >= r), 1.0, 0.0).astype(F32)
    gc_f = jnp.dot(tri_f, g, precision=HI, preferred_element_type=F32)
    gc_b = jnp.dot(tri_b, g, precision=HI, preferred_element_type=F32)
    lane = lax.broadcasted_iota(jnp.int32, (tm, LANES), 1)
    gc = jnp.where(lane < DECAY_LANE + DN_HEADS, gc_f, gc_b)
    gb_ref[...] = jnp.where(lane < DECAY_LANE, beta, gc)


def _dn_prep(qkv, ba, cw, alog_row, dtb_row, tm):
    b, t, _ = qkv.shape
    nb = tm // SUBLANES
    last = t // SUBLANES - 1
    kern = functools.partial(_dn_prep_kernel, tm=tm)
    return pl.pallas_call(
        kern,
        grid=(b, t // tm),
        in_specs=[pl.BlockSpec((None, tm, DN_CONV_DIM), lambda bi, i: (bi, i, 0)),
                  pl.BlockSpec((None, SUBLANES, DN_CONV_DIM), lambda bi, i: (bi, jnp.maximum(i * nb - 1, 0), 0)),
                  pl.BlockSpec((None, SUBLANES, DN_CONV_DIM), lambda bi, i: (bi, jnp.minimum((i + 1) * nb, last), 0)),
                  pl.BlockSpec((None, tm, LANES), lambda bi, i: (bi, i, 0)),
                  pl.BlockSpec((SUBLANES, DN_CONV_DIM), lambda bi, i: (0, 0)),
                  pl.BlockSpec((1, LANES), lambda bi, i: (0, 0)),
                  pl.BlockSpec((1, LANES), lambda bi, i: (0, 0))],
        out_specs=[pl.BlockSpec((None, tm, DN_QK), lambda bi, i: (bi, i, 0))] * 3
        + [pl.BlockSpec((None, tm, LANES), lambda bi, i: (bi, i, 0))],
        out_shape=[jax.ShapeDtypeStruct((b, t, DN_QK), F32)] * 3 + [jax.ShapeDtypeStruct((b, t, LANES), F32)],
        scratch_shapes=[pltpu.VMEM((tm + 2 * SUBLANES, DN_CONV_DIM), F32)],
        compiler_params=_params(("parallel", "parallel")),
        name="dn_prep",
    )(qkv, qkv, qkv, ba, cw, alog_row, dtb_row)


def _bmm(a, b):
    return jnp.einsum('cij,cjk->cik', a, b, precision=HI, preferred_element_type=F32)


def _dn_direction(d, q_ref, k_ref, v_ref, gb_ref, gr_ref, o_ref, s_ref, cb):
    ii = lax.broadcasted_iota(jnp.int32, (CHUNK, CHUNK), 0)
    jj = lax.broadcasted_iota(jnp.int32, (CHUNK, CHUNK), 1)
    if d == 1:
        ii, jj = jj, ii
    incl = ii >= jj
    strict = ii > jj
    eye = lax.broadcasted_iota(jnp.int32, (CHUNK, CHUNK), 0) == lax.broadcasted_iota(jnp.int32, (CHUNK, CHUNK), 1)
    gb = gb_ref[...]
    gr = gr_ref[...]
    per_head = []
    for hd in range(DN_HEADS):
        lo = hd * DN_HEAD
        q = q_ref[:, lo:lo + DN_HEAD].reshape(cb, CHUNK, DN_HEAD)
        k = k_ref[:, lo:lo + DN_HEAD].reshape(cb, CHUNK, DN_HEAD)
        v = v_ref[:, lo:lo + DN_HEAD].reshape(cb, CHUNK, DN_HEAD)
        bl = BETA_LANE + d * DN_HEADS + hd
        gl = DECAY_LANE + d * DN_HEADS + hd
        beta = gb[:, bl:bl + 1].reshape(cb, CHUNK, 1)
        gcol = gb[:, gl:gl + 1].reshape(cb, CHUNK, 1)
        grow = gr[:, gl:gl + 1, :]
        glast = grow[:, :, CHUNK - 1:CHUNK] if d == 0 else grow[:, :, 0:1]
        decay = jnp.where(incl, jnp.exp(jnp.minimum(gcol - grow, 0.0)), 0.0)
        kb = k * beta
        kk = jnp.einsum('cid,cjd->cij', kb, k, precision=HI, preferred_element_type=F32)
        lower = jnp.where(strict, kk * decay, 0.0)
        tinv = jnp.where(eye, 1.0, 0.0).astype(F32) - jnp.where((ii // 2 == jj // 2) & (ii % 2 == 1) & (jj % 2 == 0), lower, 0.0)
        s = 2
        while s < CHUNK:
            m = (ii // (2 * s) == jj // (2 * s)) & ((ii // s) % 2 == 1) & ((jj // s) % 2 == 0)
            e = jnp.where(m, lower, 0.0)
            tinv = tinv - _bmm(_bmm(tinv, e), tinv)
            s *= 2
        rhs = jnp.concatenate([v * beta, kb * jnp.exp(gcol)], axis=-1)
        sol = _bmm(tinv, rhs)
        u, w = sol[..., :DN_HEAD], sol[..., DN_HEAD:]
        attn = jnp.einsum('cid,cjd->cij', q, k, precision=HI, preferred_element_type=F32) * decay
        qg = q * jnp.exp(gcol)
        kdec = k * jnp.exp(glast - gcol)
        per_head.append((u, w, attn, qg, kdec, jnp.exp(glast)))
    order = range(cb) if d == 0 else range(cb - 1, -1, -1)
    for c in order:
        for hd in range(DN_HEADS):
            u, w, attn, qg, kdec, eg = per_head[hd]
            idx = d * DN_HEADS + hd
            st = s_ref[idx]
            v_new = u[c] - jnp.dot(w[c], st, precision=HI, preferred_element_type=F32)
            o = (jnp.dot(qg[c], st, precision=HI, preferred_element_type=F32)
                 + jnp.dot(attn[c], v_new, precision=HI, preferred_element_type=F32))
            s_ref[idx] = st * eg[c] + lax.dot_general(kdec[c], v_new, (((0,), (0,)), ((), ())),
                                                      precision=HI, preferred_element_type=F32)
            o_ref[c * CHUNK:(c + 1) * CHUNK, hd * DN_HEAD:(hd + 1) * DN_HEAD] = o


def _dn_main_kernel(qf, kf, vf, gbf, grf, qb, kb, vb, gbb, grb, of_ref, ob_ref, s_ref, *, cb):
    @pl.when(pl.program_id(1) == 0)
    def _():
        s_ref[...] = jnp.zeros_like(s_ref)

    _dn_direction(0, qf, kf, vf, gbf, grf, of_ref, s_ref, cb)
    _dn_direction(1, qb, kb, vb, gbb, grb, ob_ref, s_ref, cb)


def _dn_main(qn, kn, vv, gb, gr, cb):
    b, t, _ = qn.shape
    cbt = cb * CHUNK
    ng = t // cbt
    fwd = lambda bi, j: (bi, j, 0)
    bwd = lambda bi, j: (bi, ng - 1 - j, 0)
    fwd4 = lambda bi, j: (bi, j, 0, 0)
    bwd4 = lambda bi, j: (bi, ng - 1 - j, 0, 0)

    def specs(im3, im4):
        return ([pl.BlockSpec((None, cbt, DN_QK), im3)] * 3
                + [pl.BlockSpec((None, cbt, LANES), im3), pl.BlockSpec((None, cb, 2 * SUBLANES, CHUNK), im4)])

    kern = functools.partial(_dn_main_kernel, cb=cb)
    return pl.pallas_call(
        kern,
        grid=(b, ng),
        in_specs=specs(fwd, fwd4) + specs(bwd, bwd4),
        out_specs=[pl.BlockSpec((None, cbt, DN_QK), fwd), pl.BlockSpec((None, cbt, DN_QK), bwd)],
        out_shape=[jax.ShapeDtypeStruct((b, t, DN_QK), F32)] * 2,
        scratch_shapes=[pltpu.VMEM((N_DIR * DN_HEADS, DN_HEAD, DN_HEAD), F32)],
        compiler_params=_params(("parallel", "arbitrary")),
        name="dn_main",
    )(qn, kn, vv, gb, gr, qn, kn, vv, gb, gr)


def _rope_norm(x, w, seg, cos, sin):
    ss = jnp.dot(x * x, seg, precision=HI, preferred_element_type=F32)
    xn = x * lax.rsqrt(ss * (1.0 / ATT_HEAD_DIM) + EPS) * w
    quarter = ATT_HEAD_DIM // 4
    outs = []
    for t in range(x.shape[-1] // LANES):
        xt = xn[:, t * LANES:(t + 1) * LANES]
        lane = lax.broadcasted_iota(jnp.int32, xt.shape, 1)
        rot = jnp.where(lane % (2 * quarter) < quarter,
                        pltpu.roll(xt, LANES - quarter, axis=1), pltpu.roll(xt, quarter, axis=1))
        outs.append(xt * cos + rot * sin)
    return outs[0] if len(outs) == 1 else jnp.concatenate(outs, axis=-1)


def _attn_prep_kernel(q_ref, k_ref, v_ref, qw_ref, kw_ref, seg_ref, cos_ref, sin_ref, qo_ref, ko_ref, vo_ref):
    cos = cos_ref[...]
    sin = sin_ref[...]
    q = _rope_norm(q_ref[...], qw_ref[...], seg_ref[...], cos, sin)
    qo_ref[...] = (q * (LOG2E * ATT_HEAD_DIM ** -0.5)).astype(BF16)
    k = _rope_norm(k_ref[...], kw_ref[...], seg_ref[0:LANES, 0:LANES], cos, sin)
    ko_ref[...] = k.astype(BF16)
    vo_ref[...] = v_ref[...].astype(BF16)


def _attn_prep(atq, atk, atv, qw_row, kw_row, seg, cos_t, sin_t, tm):
    b, t, _ = atq.shape
    row = lambda bi, i: (bi, i, 0)
    const = lambda bi, i: (0, 0)
    return pl.pallas_call(
        _attn_prep_kernel,
        grid=(b, t // tm),
        in_specs=[pl.BlockSpec((None, tm, ATT_Q), row), pl.BlockSpec((None, tm, ATT_KV), row),
                  pl.BlockSpec((None, tm, ATT_KV), row),
                  pl.BlockSpec((1, ATT_Q), const), pl.BlockSpec((1, ATT_KV), const),
                  pl.BlockSpec((ATT_Q, ATT_Q), const),
                  pl.BlockSpec((tm, LANES), lambda bi, i: (i, 0)), pl.BlockSpec((tm, LANES), lambda bi, i: (i, 0))],
        out_specs=[pl.BlockSpec((None, tm, ATT_Q), row), pl.BlockSpec((None, tm, ATT_KV), row),
                   pl.BlockSpec((None, tm, ATT_KV), row)],
        out_shape=[jax.ShapeDtypeStruct((b, t, ATT_Q), BF16), jax.ShapeDtypeStruct((b, t, ATT_KV), BF16),
                   jax.ShapeDtypeStruct((b, t, ATT_KV), BF16)],
        compiler_params=_params(("parallel", "parallel")),
        name="attn_prep",
    )(atq, atk, atv, qw_row, kw_row, seg, cos_t, sin_t)


def _attn_kernel(q_ref, kt_ref, v_ref, o_ref, *, tq, tk):
    m_rows = ATT_GROUP * tq
    q = q_ref[...].reshape(m_rows, ATT_HEAD_DIM)
    nk = kt_ref.shape[1] // tk

    def body(j, carry):
        m, acc = carry
        start = pl.multiple_of(j * tk, tk)
        s = jnp.dot(q, kt_ref[:, pl.ds(start, tk)], preferred_element_type=F32)
        m_new = jnp.maximum(m, jnp.max(s, axis=-1, keepdims=True))
        p = jnp.exp2(s - m_new).astype(BF16)
        acc = jnp.exp2(m - m_new) * acc + jnp.dot(p, v_ref[pl.ds(start, tk), :], preferred_element_type=F32)
        return m_new, acc

    init = (jnp.full((m_rows, 1), jnp.finfo(F32).min, F32), jnp.zeros((m_rows, 2 * ATT_HEAD_DIM), F32))
    _, acc = lax.fori_loop(0, nk, body, init)
    o = acc[:, :ATT_HEAD_DIM] / acc[:, ATT_HEAD_DIM:ATT_HEAD_DIM + 1]
    o_ref[...] = o.reshape(ATT_GROUP, tq, ATT_HEAD_DIM)


def _attention(q5, kt4, va4, tq, tk):
    b, _, _, t, _ = q5.shape
    kern = functools.partial(_attn_kernel, tq=tq, tk=tk)
    return pl.pallas_call(
        kern,
        grid=(b, ATT_KV_HEADS, t // tq),
        in_specs=[pl.BlockSpec((None, None, ATT_GROUP, tq, ATT_HEAD_DIM), lambda bi, h, i: (bi, h, 0, i, 0)),
                  pl.BlockSpec((None, None, ATT_HEAD_DIM, t), lambda bi, h, i: (bi, h, 0, 0)),
                  pl.BlockSpec((None, None, t, 2 * ATT_HEAD_DIM), lambda bi, h, i: (bi, h, 0, 0))],
        out_specs=pl.BlockSpec((None, None, ATT_GROUP, tq, ATT_HEAD_DIM), lambda bi, h, i: (bi, h, 0, i, 0)),
        out_shape=jax.ShapeDtypeStruct((b, ATT_KV_HEADS, ATT_GROUP, t, ATT_HEAD_DIM), F32),
        compiler_params=_params(("parallel", "parallel", "parallel")),
        name="attention",
    )(q5, kt4, va4)


def _out_proj_kernel(of_ref, ob_ref, dnz_ref, dnw_ref, oat_ref, atz_ref, x_ref, w_ref, fw_ref, y_ref, *, final):
    o = of_ref[...] + ob_ref[...]
    parts = []
    for hd in range(DN_HEADS):
        oh = o[:, hd * DN_HEAD:(hd + 1) * DN_HEAD]
        parts.append(oh * lax.rsqrt(jnp.mean(oh * oh, axis=-1, keepdims=True) + EPS) * dnw_ref[...])
    y_dn = jnp.concatenate(parts, axis=-1) * _silu(dnz_ref[...])
    y_at = oat_ref[...] * _silu(atz_ref[...])
    y = jnp.concatenate([y_dn, y_at], axis=-1).astype(BF16)
    out = x_ref[...] + jnp.dot(y, w_ref[...], preferred_element_type=F32)
    if final:
        out = out * lax.rsqrt(jnp.mean(out * out, axis=-1, keepdims=True) + EPS) * fw_ref[...]
    y_ref[...] = out


def _out_proj(o_f, o_b, dnz, dnw_row, o_at, atz, x2, w_out, fw_row, tm, final):
    bt = x2.shape[0]
    row = lambda i: (i, 0)
    const = lambda i: (0, 0)
    kern = functools.partial(_out_proj_kernel, final=final)
    return pl.pallas_call(
        kern,
        grid=(bt // tm,),
        in_specs=[pl.BlockSpec((tm, DN_QK), row), pl.BlockSpec((tm, DN_QK), row), pl.BlockSpec((tm, DN_QK), row),
                  pl.BlockSpec((1, DN_HEAD), const),
                  pl.BlockSpec((tm, ATT_Q), row), pl.BlockSpec((tm, ATT_Q), row),
                  pl.BlockSpec((tm, D_MODEL), row),
                  pl.BlockSpec((D_MODEL, D_MODEL), const), pl.BlockSpec((1, D_MODEL), const)],
        out_specs=pl.BlockSpec((tm, D_MODEL), row),
        out_shape=jax.ShapeDtypeStruct((bt, D_MODEL), F32),
        compiler_params=_params(("parallel",)),
        name="out_proj_final" if final else "out_proj",
    )(o_f, o_b, dnz, dnw_row, o_at, atz, x2, w_out, fw_row)


def _pack_w_in(w):
    splits = np.cumsum(IN_SIZES)[:-1].tolist()
    qkv, dnz, bb, aa, atq, atk, atv, atz = jnp.split(w, splits, axis=-1)
    pad = jnp.zeros((w.shape[0], LANES - 2 * N_DIR * DN_HEADS), w.dtype)
    return jnp.concatenate([qkv, dnz, atq, atk, atv, atz, bb, aa, pad], axis=-1).astype(BF16)


def _rope_tables(t):
    half = ATT_HEAD_DIM // 2
    pos = np.arange(t)
    row = (pos // GRID_W).astype(np.float32)
    col = (pos % GRID_W).astype(np.float32)
    inv = jnp.asarray(ROPE_THETA, F32) ** (-jnp.arange(0, half, 2, dtype=F32) / half)
    ang_r = jnp.asarray(row)[:, None] * inv[None, :]
    ang_c = jnp.asarray(col)[:, None] * inv[None, :]
    cos = jnp.concatenate([jnp.cos(ang_r)] * 2 + [jnp.cos(ang_c)] * 2, axis=-1)
    sin = jnp.concatenate([-jnp.sin(ang_r), jnp.sin(ang_r), -jnp.sin(ang_c), jnp.sin(ang_c)], axis=-1)
    reps = LANES // ATT_HEAD_DIM
    return jnp.tile(cos, (1, reps)), jnp.tile(sin, (1, reps))


def _gate_row(p):
    return jnp.zeros((1, LANES), F32).at[0, DECAY_LANE:DECAY_LANE + N_DIR * DN_HEADS].set(p.reshape(-1))


def kernel(x, norm_w, w_in, conv_w, a_log, dt_bias, dn_norm_w, q_norm_w, k_norm_w, w_out, final_norm_w):
    b, t, d = x.shape
    depth = w_in.shape[0]
    tm = 256
    cb = 4
    tq = 256
    tk = min(2048, t)
    cos_t, sin_t = _rope_tables(t)
    lane = np.arange(ATT_Q)
    seg = jnp.asarray((lane[:, None] // ATT_HEAD_DIM) == (lane[None, :] // ATT_HEAD_DIM), F32)
    x2 = x.reshape(b * t, d)
    for l in range(depth):
        qkv, dnz, atq, atk, atv, atz, ba = _in_proj(x2, norm_w[l][None, :], _pack_w_in(w_in[l]), tm)
        cw = jnp.zeros((SUBLANES, DN_CONV_DIM), F32).at[:CONV_K].set(conv_w[l])
        qn, kn, vv, gb = _dn_prep(qkv.reshape(b, t, -1), ba.reshape(b, t, -1), cw,
                                  _gate_row(a_log[l]), _gate_row(dt_bias[l]), tm)
        gr = gb[..., :2 * SUBLANES].reshape(b, t // CHUNK, CHUNK, 2 * SUBLANES).transpose(0, 1, 3, 2)
        o_f, o_b = _dn_main(qn, kn, vv, gb, gr, cb)
        qr, kr, vb = _attn_prep(atq.reshape(b, t, -1), atk.reshape(b, t, -1), atv.reshape(b, t, -1),
                                jnp.tile(q_norm_w[l], ATT_HEADS)[None, :], jnp.tile(k_norm_w[l], ATT_KV_HEADS)[None, :],
                                seg, cos_t, sin_t, tm)
        q5 = qr.reshape(b, t, ATT_KV_HEADS, ATT_GROUP, ATT_HEAD_DIM).transpose(0, 2, 3, 1, 4)
        kt4 = kr.reshape(b, t, ATT_KV_HEADS, ATT_HEAD_DIM).transpose(0, 2, 3, 1)
        v4 = vb.reshape(b, t, ATT_KV_HEADS, ATT_HEAD_DIM).transpose(0, 2, 1, 3)
        va4 = jnp.concatenate([v4, jnp.ones_like(v4[..., :1]), jnp.zeros_like(v4[..., 1:])], axis=-1)
        o5 = _attention(q5, kt4, va4, tq, tk)
        o_at = o5.transpose(0, 3, 1, 2, 4).reshape(b * t, ATT_Q)
        x2 = _out_proj(o_f.reshape(b * t, -1), o_b.reshape(b * t, -1), dnz, dn_norm_w[l][None, :], o_at, atz,
                       x2, w_out[l].astype(BF16), final_norm_w[None, :], tm, final=(l == depth - 1))
    return x2.reshape(b, t, d)
```

```python
import functools

import jax
import jax.numpy as jnp
import numpy as np
from jax import lax
from jax.experimental import pallas as pl
from jax.experimental.pallas import tpu as pltpu

D_MODEL = 1024
DN_HEADS = 4
DN_HEAD = 128
DN_QK = DN_HEADS * DN_HEAD
DN_CONV_DIM = 3 * DN_QK
CONV_K = 5
CHUNK = 64
N_DIR = 2
ATT_HEADS = 8
ATT_KV_HEADS = 2
ATT_GROUP = ATT_HEADS // ATT_KV_HEADS
ATT_HEAD_DIM = 64
ATT_Q = ATT_HEADS * ATT_HEAD_DIM
ATT_KV = ATT_KV_HEADS * ATT_HEAD_DIM
ROPE_THETA = 10000.0
GRID_W = 64
EPS = 1e-6
IN_SIZES = (DN_CONV_DIM, DN_QK, N_DIR * DN_HEADS, N_DIR * DN_HEADS, ATT_Q, ATT_KV, ATT_KV, ATT_Q)
LANES = 128
SUBLANES = 8
VMEM_LIMIT = 48 * 1024 * 1024
PACK_COLS = DN_CONV_DIM + DN_QK + ATT_Q + ATT_KV + ATT_KV + ATT_Q + LANES
BETA_LANE = 0
DECAY_LANE = N_DIR * DN_HEADS

F32 = jnp.float32
BF16 = jnp.bfloat16
HI = lax.Precision.HIGHEST
LOG2E = 1.4426950408889634


def _params(sem):
    return pltpu.CompilerParams(dimension_semantics=sem, vmem_limit_bytes=VMEM_LIMIT)


def _silu(x):
    return x / (1.0 + jnp.exp(-x))


def _in_proj_kernel(x_ref, nw_ref, w_ref, qkv_ref, dnz_ref, atq_ref, atk_ref, atv_ref, atz_ref, ba_ref):
    x = x_ref[...]
    h = x * lax.rsqrt(jnp.mean(x * x, axis=-1, keepdims=True) + EPS) * nw_ref[...]
    p = jnp.dot(h.astype(BF16), w_ref[...], preferred_element_type=F32)
    o = 0
    for ref in (qkv_ref, dnz_ref, atq_ref, atk_ref, atv_ref, atz_ref, ba_ref):
        n = ref.shape[-1]
        ref[...] = p[:, o:o + n]
        o += n


def _in_proj(x2, nw, w_packed, tm):
    bt = x2.shape[0]
    widths = (DN_CONV_DIM, DN_QK, ATT_Q, ATT_KV, ATT_KV, ATT_Q, LANES)
    return pl.pallas_call(
        _in_proj_kernel,
        grid=(bt // tm,),
        in_specs=[pl.BlockSpec((tm, D_MODEL), lambda i: (i, 0)),
                  pl.BlockSpec((1, D_MODEL), lambda i: (0, 0)),
                  pl.BlockSpec((D_MODEL, PACK_COLS), lambda i: (0, 0))],
        out_specs=[pl.BlockSpec((tm, n), lambda i: (i, 0)) for n in widths],
        out_shape=[jax.ShapeDtypeStruct((bt, n), F32) for n in widths],
        compiler_params=_params(("parallel",)),
        name="in_proj",
    )(x2, nw, w_packed)


def _dn_prep_kernel(x_ref, prev_ref, next_ref, ba_ref, cw_ref, alog_ref, dtb_ref,
                    q_ref, k_ref, v_ref, gb_ref, xe_ref, *, tm):
    i = pl.program_id(1)
    n = pl.num_programs(1)
    halo = SUBLANES
    xe_ref[0:halo, :] = jnp.where(i > 0, prev_ref[...], 0.0)
    xe_ref[halo:halo + tm, :] = x_ref[...]
    xe_ref[halo + tm:2 * halo + tm, :] = jnp.where(i < n - 1, next_ref[...], 0.0)
    pad = CONV_K // 2
    y = jnp.zeros((tm, DN_CONV_DIM), F32)
    for j in range(CONV_K):
        y = y + xe_ref[halo + j - pad:halo + j - pad + tm, :] * cw_ref[j:j + 1, :]
    y = _silu(y)
    for hd in range(DN_HEADS):
        lo = hd * DN_HEAD
        q = y[:, lo:lo + DN_HEAD]
        k = y[:, DN_QK + lo:DN_QK + lo + DN_HEAD]
        q_ref[:, lo:lo + DN_HEAD] = q * lax.rsqrt(jnp.sum(q * q, axis=-1, keepdims=True) + EPS) * (DN_HEAD ** -0.5)
        k_ref[:, lo:lo + DN_HEAD] = k * lax.rsqrt(jnp.sum(k * k, axis=-1, keepdims=True) + EPS)
    v_ref[...] = y[:, 2 * DN_QK:]

    ba = ba_ref[...]
    beta = 1.0 / (1.0 + jnp.exp(-ba))
    z = ba + dtb_ref[...]
    softplus = jnp.maximum(z, 0.0) + jnp.log(1.0 + jnp.exp(-jnp.abs(z)))
    g = -jnp.exp(alog_ref[...]) * softplus
    r = lax.broadcasted_iota(jnp.int32, (tm, tm), 0)
    c = lax.broadcasted_iota(jnp.int32, (tm, tm), 1)
    same = (r // CHUNK) == (c // CHUNK)
    tri_f = jnp.where(same & (c <= r), 1.0, 0.0).astype(F32)
    tri_b = jnp.where(same & (c >= r), 1.0, 0.0).astype(F32)
    gc_f = jnp.dot(tri_f, g, precision=HI, preferred_element_type=F32)
    gc_b = jnp.dot(tri_b, g, precision=HI, preferred_element_type=F32)
    lane = lax.broadcasted_iota(jnp.int32, (tm, LANES), 1)
    gc = jnp.where(lane < DECAY_LANE + DN_HEADS, gc_f, gc_b)
    gb_ref[...] = jnp.where(lane < DECAY_LANE, beta, gc)


def _dn_prep(qkv, ba, cw, alog_row, dtb_row, tm):
    b, t, _ = qkv.shape
    nb = tm // SUBLANES
    last = t // SUBLANES - 1
    kern = functools.partial(_dn_prep_kernel, tm=tm)
    return pl.pallas_call(
        kern,
        grid=(b, t // tm),
        in_specs=[pl.BlockSpec((None, tm, DN_CONV_DIM), lambda bi, i: (bi, i, 0)),
                  pl.BlockSpec((None, SUBLANES, DN_CONV_DIM), lambda bi, i: (bi, jnp.maximum(i * nb - 1, 0), 0)),
                  pl.BlockSpec((None, SUBLANES, DN_CONV_DIM), lambda bi, i: (bi, jnp.minimum((i + 1) * nb, last), 0)),
                  pl.BlockSpec((None, tm, LANES), lambda bi, i: (bi, i, 0)),
                  pl.BlockSpec((SUBLANES, DN_CONV_DIM), lambda bi, i: (0, 0)),
                  pl.BlockSpec((1, LANES), lambda bi, i: (0, 0)),
                  pl.BlockSpec((1, LANES), lambda bi, i: (0, 0))],
        out_specs=[pl.BlockSpec((None, tm, DN_QK), lambda bi, i: (bi, i, 0))] * 3
        + [pl.BlockSpec((None, tm, LANES), lambda bi, i: (bi, i, 0))],
        out_shape=[jax.ShapeDtypeStruct((b, t, DN_QK), F32)] * 3 + [jax.ShapeDtypeStruct((b, t, LANES), F32)],
        scratch_shapes=[pltpu.VMEM((tm + 2 * SUBLANES, DN_CONV_DIM), F32)],
        compiler_params=_params(("parallel", "parallel")),
        name="dn_prep",
    )(qkv, qkv, qkv, ba, cw, alog_row, dtb_row)


def _mm(a, b):
    return jnp.dot(a.astype(BF16), b.astype(BF16), preferred_element_type=F32)


def _bmm(a, b):
    return jnp.einsum('cij,cjk->cik', a.astype(BF16), b.astype(BF16), preferred_element_type=F32)


def _bmm_nt(a, b):
    return jnp.einsum('cid,cjd->cij', a.astype(BF16), b.astype(BF16), preferred_element_type=F32)


def _dn_chunk_terms(d, q_ref, k_ref, v_ref, gb_ref, gr_ref, cb):
    ii = lax.broadcasted_iota(jnp.int32, (CHUNK, CHUNK), 0)
    jj = lax.broadcasted_iota(jnp.int32, (CHUNK, CHUNK), 1)
    if d == 1:
        ii, jj = jj, ii
    incl = ii >= jj
    strict = ii > jj
    eye = jnp.where(ii == jj, 1.0, 0.0).astype(F32)
    merge = []
    s = 1
    while s < CHUNK:
        merge.append((ii // (2 * s) == jj // (2 * s)) & ((ii // s) % 2 == 1) & ((jj // s) % 2 == 0))
        s *= 2
    gb = gb_ref[...]
    gr = gr_ref[...]

    def heads(ref):
        return jnp.concatenate([ref[:, hd * DN_HEAD:(hd + 1) * DN_HEAD].reshape(cb, CHUNK, DN_HEAD)
                                for hd in range(DN_HEADS)], axis=0)

    def gate_cols(lane0):
        return jnp.concatenate([gb[:, lane0 + hd:lane0 + hd + 1].reshape(cb, CHUNK, 1) for hd in range(DN_HEADS)], axis=0)

    q, k, v = heads(q_ref), heads(k_ref), heads(v_ref)
    beta = gate_cols(BETA_LANE + d * DN_HEADS)
    gcol = gate_cols(DECAY_LANE + d * DN_HEADS)
    gl = DECAY_LANE + d * DN_HEADS
    grow = jnp.concatenate([gr[:, gl + hd:gl + hd + 1, :] for hd in range(DN_HEADS)], axis=0)
    glast = grow[:, :, CHUNK - 1:CHUNK] if d == 0 else grow[:, :, 0:1]
    decay = jnp.where(incl, jnp.exp(jnp.minimum(gcol - grow, 0.0)), 0.0)
    kb = k * beta
    lower = jnp.where(strict, _bmm_nt(kb, k) * decay, 0.0)
    tinv = eye - jnp.where(merge[0], lower, 0.0)
    for m in merge[1:]:
        tinv = tinv - _bmm(_bmm(tinv, jnp.where(m, lower, 0.0)), tinv)
    rhs = jnp.concatenate([v * beta, kb * jnp.exp(gcol)], axis=-1)
    sol = _bmm(tinv, rhs)
    u, w = sol[..., :DN_HEAD], sol[..., DN_HEAD:]
    attn = (_bmm_nt(q, k) * decay).astype(BF16)
    wq = jnp.concatenate([w, q * jnp.exp(gcol)], axis=1).astype(BF16)
    kdec = (k * jnp.exp(glast - gcol)).astype(BF16)
    return u, wq, attn, kdec, jnp.exp(glast)


def _dn_step(d, c, pre, o_ref, s_ref, cb):
    u, wq, attn, kdec, eg = pre
    for hd in range(DN_HEADS):
        n = hd * cb + c
        idx = d * DN_HEADS + hd
        st = s_ref[idx]
        ws = _mm(wq[n], st)
        v_new = (u[n] - ws[:CHUNK]).astype(BF16)
        o = ws[CHUNK:] + _mm(attn[n], v_new)
        s_ref[idx] = st * eg[n] + lax.dot_general(kdec[n], v_new, (((0,), (0,)), ((), ())),
                                                  preferred_element_type=F32)
        o_ref[c * CHUNK:(c + 1) * CHUNK, hd * DN_HEAD:(hd + 1) * DN_HEAD] = o


def _dn_main_kernel(qf, kf, vf, gbf, grf, qb, kb, vb, gbb, grb, of_ref, ob_ref, s_ref, *, cb):
    @pl.when(pl.program_id(1) == 0)
    def _():
        s_ref[...] = jnp.zeros_like(s_ref)

    pre_f = _dn_chunk_terms(0, qf, kf, vf, gbf, grf, cb)
    pre_b = _dn_chunk_terms(1, qb, kb, vb, gbb, grb, cb)
    for i in range(cb):
        _dn_step(0, i, pre_f, of_ref, s_ref, cb)
        _dn_step(1, cb - 1 - i, pre_b, ob_ref, s_ref, cb)


def _dn_main(qn, kn, vv, gb, gr, cb):
    b, t, _ = qn.shape
    cbt = cb * CHUNK
    ng = t // cbt
    fwd = lambda bi, j: (bi, j, 0)
    bwd = lambda bi, j: (bi, ng - 1 - j, 0)
    fwd4 = lambda bi, j: (bi, j, 0, 0)
    bwd4 = lambda bi, j: (bi, ng - 1 - j, 0, 0)

    def specs(im3, im4):
        return ([pl.BlockSpec((None, cbt, DN_QK), im3)] * 3
                + [pl.BlockSpec((None, cbt, LANES), im3), pl.BlockSpec((None, cb, 2 * SUBLANES, CHUNK), im4)])

    kern = functools.partial(_dn_main_kernel, cb=cb)
    return pl.pallas_call(
        kern,
        grid=(b, ng),
        in_specs=specs(fwd, fwd4) + specs(bwd, bwd4),
        out_specs=[pl.BlockSpec((None, cbt, DN_QK), fwd), pl.BlockSpec((None, cbt, DN_QK), bwd)],
        out_shape=[jax.ShapeDtypeStruct((b, t, DN_QK), F32)] * 2,
        scratch_shapes=[pltpu.VMEM((N_DIR * DN_HEADS, DN_HEAD, DN_HEAD), F32)],
        compiler_params=_params(("parallel", "arbitrary")),
        name="dn_main",
    )(qn, kn, vv, gb, gr, qn, kn, vv, gb, gr)


def _rope_norm(x, w, seg, cos, sin):
    ss = jnp.dot(x * x, seg, precision=HI, preferred_element_type=F32)
    xn = x * lax.rsqrt(ss * (1.0 / ATT_HEAD_DIM) + EPS) * w
    quarter = ATT_HEAD_DIM // 4
    outs = []
    for t in range(x.shape[-1] // LANES):
        xt = xn[:, t * LANES:(t + 1) * LANES]
        lane = lax.broadcasted_iota(jnp.int32, xt.shape, 1)
        rot = jnp.where(lane % (2 * quarter) < quarter,
                        pltpu.roll(xt, LANES - quarter, axis=1), pltpu.roll(xt, quarter, axis=1))
        outs.append(xt * cos + rot * sin)
    return outs[0] if len(outs) == 1 else jnp.concatenate(outs, axis=-1)


def _attn_prep_kernel(q_ref, k_ref, v_ref, qw_ref, kw_ref, seg_ref, cos_ref, sin_ref, qo_ref, ko_ref, vo_ref):
    cos = cos_ref[...]
    sin = sin_ref[...]
    q = _rope_norm(q_ref[...], qw_ref[...], seg_ref[...], cos, sin)
    qo_ref[...] = (q * (LOG2E * ATT_HEAD_DIM ** -0.5)).astype(BF16)
    k = _rope_norm(k_ref[...], kw_ref[...], seg_ref[0:LANES, 0:LANES], cos, sin)
    ko_ref[...] = k.astype(BF16)
    vo_ref[...] = v_ref[...].astype(BF16)


def _attn_prep(atq, atk, atv, qw_row, kw_row, seg, cos_t, sin_t, tm):
    b, t, _ = atq.shape
    row = lambda bi, i: (bi, i, 0)
    const = lambda bi, i: (0, 0)
    return pl.pallas_call(
        _attn_prep_kernel,
        grid=(b, t // tm),
        in_specs=[pl.BlockSpec((None, tm, ATT_Q), row), pl.BlockSpec((None, tm, ATT_KV), row),
                  pl.BlockSpec((None, tm, ATT_KV), row),
                  pl.BlockSpec((1, ATT_Q), const), pl.BlockSpec((1, ATT_KV), const),
                  pl.BlockSpec((ATT_Q, ATT_Q), const),
                  pl.BlockSpec((tm, LANES), lambda bi, i: (i, 0)), pl.BlockSpec((tm, LANES), lambda bi, i: (i, 0))],
        out_specs=[pl.BlockSpec((None, tm, ATT_Q), row), pl.BlockSpec((None, tm, ATT_KV), row),
                   pl.BlockSpec((None, tm, ATT_KV), row)],
        out_shape=[jax.ShapeDtypeStruct((b, t, ATT_Q), BF16), jax.ShapeDtypeStruct((b, t, ATT_KV), BF16),
                   jax.ShapeDtypeStruct((b, t, ATT_KV), BF16)],
        compiler_params=_params(("parallel", "parallel")),
        name="attn_prep",
    )(atq, atk, atv, qw_row, kw_row, seg, cos_t, sin_t)


def _attn_kernel(q_ref, kt_ref, v_ref, o_ref, *, tq, tk):
    m_rows = ATT_GROUP * tq
    q = q_ref[...].reshape(m_rows, ATT_HEAD_DIM)
    nk = kt_ref.shape[1] // tk

    def body(j, carry):
        m, acc = carry
        start = pl.multiple_of(j * tk, tk)
        s = jnp.dot(q, kt_ref[:, pl.ds(start, tk)], preferred_element_type=F32)
        m_new = jnp.maximum(m, jnp.max(s, axis=-1, keepdims=True))
        p = jnp.exp2(s - m_new).astype(BF16)
        acc = jnp.exp2(m - m_new) * acc + jnp.dot(p, v_ref[pl.ds(start, tk), :], preferred_element_type=F32)
        return m_new, acc

    init = (jnp.full((m_rows, 1), jnp.finfo(F32).min, F32), jnp.zeros((m_rows, 2 * ATT_HEAD_DIM), F32))
    _, acc = lax.fori_loop(0, nk, body, init)
    o = acc[:, :ATT_HEAD_DIM] / acc[:, ATT_HEAD_DIM:ATT_HEAD_DIM + 1]
    o_ref[...] = o.reshape(ATT_GROUP, tq, ATT_HEAD_DIM)


def _attention(q5, kt4, va4, tq, tk):
    b, _, _, t, _ = q5.shape
    kern = functools.partial(_attn_kernel, tq=tq, tk=tk)
    return pl.pallas_call(
        kern,
        grid=(b, ATT_KV_HEADS, t // tq),
        in_specs=[pl.BlockSpec((None, None, ATT_GROUP, tq, ATT_HEAD_DIM), lambda bi, h, i: (bi, h, 0, i, 0)),
                  pl.BlockSpec((None, None, ATT_HEAD_DIM, t), lambda bi, h, i: (bi, h, 0, 0)),
                  pl.BlockSpec((None, None, t, 2 * ATT_HEAD_DIM), lambda bi, h, i: (bi, h, 0, 0))],
        out_specs=pl.BlockSpec((None, None, ATT_GROUP, tq, ATT_HEAD_DIM), lambda bi, h, i: (bi, h, 0, i, 0)),
        out_shape=jax.ShapeDtypeStruct((b, ATT_KV_HEADS, ATT_GROUP, t, ATT_HEAD_DIM), F32),
        compiler_params=_params(("parallel", "parallel", "parallel")),
        name="attention",
    )(q5, kt4, va4)


def _out_proj_kernel(of_ref, ob_ref, dnz_ref, dnw_ref, oat_ref, atz_ref, x_ref, w_ref, fw_ref, y_ref, *, final):
    o = of_ref[...] + ob_ref[...]
    parts = []
    for hd in range(DN_HEADS):
        oh = o[:, hd * DN_HEAD:(hd + 1) * DN_HEAD]
        parts.append(oh * lax.rsqrt(jnp.mean(oh * oh, axis=-1, keepdims=True) + EPS) * dnw_ref[...])
    y_dn = jnp.concatenate(parts, axis=-1) * _silu(dnz_ref[...])
    y_at = oat_ref[...] * _silu(atz_ref[...])
    y = jnp.concatenate([y_dn, y_at], axis=-1).astype(BF16)
    out = x_ref[...] + jnp.dot(y, w_ref[...], preferred_element_type=F32)
    if final:
        out = out * lax.rsqrt(jnp.mean(out * out, axis=-1, keepdims=True) + EPS) * fw_ref[...]
    y_ref[...] = out


def _out_proj(o_f, o_b, dnz, dnw_row, o_at, atz, x2, w_out, fw_row, tm, final):
    bt = x2.shape[0]
    row = lambda i: (i, 0)
    const = lambda i: (0, 0)
    kern = functools.partial(_out_proj_kernel, final=final)
    return pl.pallas_call(
        kern,
        grid=(bt // tm,),
        in_specs=[pl.BlockSpec((tm, DN_QK), row), pl.BlockSpec((tm, DN_QK), row), pl.BlockSpec((tm, DN_QK), row),
                  pl.BlockSpec((1, DN_HEAD), const),
                  pl.BlockSpec((tm, ATT_Q), row), pl.BlockSpec((tm, ATT_Q), row),
                  pl.BlockSpec((tm, D_MODEL), row),
                  pl.BlockSpec((D_MODEL, D_MODEL), const), pl.BlockSpec((1, D_MODEL), const)],
        out_specs=pl.BlockSpec((tm, D_MODEL), row),
        out_shape=jax.ShapeDtypeStruct((bt, D_MODEL), F32),
        compiler_params=_params(("parallel",)),
        name="out_proj_final" if final else "out_proj",
    )(o_f, o_b, dnz, dnw_row, o_at, atz, x2, w_out, fw_row)


def _pack_w_in(w):
    splits = np.cumsum(IN_SIZES)[:-1].tolist()
    qkv, dnz, bb, aa, atq, atk, atv, atz = jnp.split(w, splits, axis=-1)
    pad = jnp.zeros((w.shape[0], LANES - 2 * N_DIR * DN_HEADS), w.dtype)
    return jnp.concatenate([qkv, dnz, atq, atk, atv, atz, bb, aa, pad], axis=-1).astype(BF16)


def _rope_tables(t):
    half = ATT_HEAD_DIM // 2
    pos = np.arange(t)
    row = (pos // GRID_W).astype(np.float32)
    col = (pos % GRID_W).astype(np.float32)
    inv = jnp.asarray(ROPE_THETA, F32) ** (-jnp.arange(0, half, 2, dtype=F32) / half)
    ang_r = jnp.asarray(row)[:, None] * inv[None, :]
    ang_c = jnp.asarray(col)[:, None] * inv[None, :]
    cos = jnp.concatenate([jnp.cos(ang_r)] * 2 + [jnp.cos(ang_c)] * 2, axis=-1)
    sin = jnp.concatenate([-jnp.sin(ang_r), jnp.sin(ang_r), -jnp.sin(ang_c), jnp.sin(ang_c)], axis=-1)
    reps = LANES // ATT_HEAD_DIM
    return jnp.tile(cos, (1, reps)), jnp.tile(sin, (1, reps))


def _gate_row(p):
    return jnp.zeros((1, LANES), F32).at[0, DECAY_LANE:DECAY_LANE + N_DIR * DN_HEADS].set(p.reshape(-1))


def kernel(x, norm_w, w_in, conv_w, a_log, dt_bias, dn_norm_w, q_norm_w, k_norm_w, w_out, final_norm_w):
    b, t, d = x.shape
    depth = w_in.shape[0]
    tm = 256
    cb = 4
    tq = 256
    tk = min(2048, t)
    cos_t, sin_t = _rope_tables(t)
    lane = np.arange(ATT_Q)
    seg = jnp.asarray((lane[:, None] // ATT_HEAD_DIM) == (lane[None, :] // ATT_HEAD_DIM), F32)
    x2 = x.reshape(b * t, d)
    for l in range(depth):
        qkv, dnz, atq, atk, atv, atz, ba = _in_proj(x2, norm_w[l][None, :], _pack_w_in(w_in[l]), tm)
        cw = jnp.zeros((SUBLANES, DN_CONV_DIM), F32).at[:CONV_K].set(conv_w[l])
        qn, kn, vv, gb = _dn_prep(qkv.reshape(b, t, -1), ba.reshape(b, t, -1), cw,
                                  _gate_row(a_log[l]), _gate_row(dt_bias[l]), tm)
        gr = gb[..., :2 * SUBLANES].reshape(b, t // CHUNK, CHUNK, 2 * SUBLANES).transpose(0, 1, 3, 2)
        o_f, o_b = _dn_main(qn, kn, vv, gb, gr, cb)
        qr, kr, vb = _attn_prep(atq.reshape(b, t, -1), atk.reshape(b, t, -1), atv.reshape(b, t, -1),
                                jnp.tile(q_norm_w[l], ATT_HEADS)[None, :], jnp.tile(k_norm_w[l], ATT_KV_HEADS)[None, :],
                                seg, cos_t, sin_t, tm)
        q5 = qr.reshape(b, t, ATT_KV_HEADS, ATT_GROUP, ATT_HEAD_DIM).transpose(0, 2, 3, 1, 4)
        kt4 = kr.reshape(b, t, ATT_KV_HEADS, ATT_HEAD_DIM).transpose(0, 2, 3, 1)
        v4 = vb.reshape(b, t, ATT_KV_HEADS, ATT_HEAD_DIM).transpose(0, 2, 1, 3)
        va4 = jnp.concatenate([v4, jnp.ones_like(v4[..., :1]), jnp.zeros_like(v4[..., 1:])], axis=-1)
        o5 = _attention(q5, kt4, va4, tq, tk)
        o_at = o5.transpose(0, 3, 1, 2, 4).reshape(b * t, ATT_Q)
        x2 = _out_proj(o_f.reshape(b * t, -1), o_b.reshape(b * t, -1), dnz, dn_norm_w[l][None, :], o_at, atz,
                       x2, w_out[l].astype(BF16), final_norm_w[None, :], tm, final=(l == depth - 1))
    return x2.reshape(b, t, d)
```

```python
import functools

import jax
import jax.numpy as jnp
import numpy as np
from jax import lax
from jax.experimental import pallas as pl
from jax.experimental.pallas import tpu as pltpu

D_MODEL = 1024
DN_HEADS = 4
DN_HEAD = 128
DN_QK = DN_HEADS * DN_HEAD
DN_CONV_DIM = 3 * DN_QK
CONV_K = 5
CHUNK = 64
N_DIR = 2
ATT_HEADS = 8
ATT_KV_HEADS = 2
ATT_GROUP = ATT_HEADS // ATT_KV_HEADS
ATT_HEAD_DIM = 64
ATT_Q = ATT_HEADS * ATT_HEAD_DIM
ATT_KV = ATT_KV_HEADS * ATT_HEAD_DIM
ROPE_THETA = 10000.0
GRID_W = 64
EPS = 1e-6
IN_SIZES = (DN_CONV_DIM, DN_QK, N_DIR * DN_HEADS, N_DIR * DN_HEADS, ATT_Q, ATT_KV, ATT_KV, ATT_Q)
LANES = 128
SUBLANES = 8
VMEM_LIMIT = 48 * 1024 * 1024
PACK_COLS = DN_CONV_DIM + DN_QK + ATT_Q + ATT_KV + ATT_KV + ATT_Q + LANES
BETA_LANE = 0
DECAY_LANE = N_DIR * DN_HEADS

F32 = jnp.float32
BF16 = jnp.bfloat16
HI = lax.Precision.HIGHEST
LOG2E = 1.4426950408889634


def _params(sem):
    return pltpu.CompilerParams(dimension_semantics=sem, vmem_limit_bytes=VMEM_LIMIT)


def _silu(x):
    return x / (1.0 + jnp.exp(-x))


def _in_proj_kernel(x_ref, nw_ref, w_ref, qkv_ref, dnz_ref, atq_ref, atk_ref, atv_ref, atz_ref, ba_ref):
    x = x_ref[...]
    h = x * lax.rsqrt(jnp.mean(x * x, axis=-1, keepdims=True) + EPS) * nw_ref[...]
    p = jnp.dot(h.astype(BF16), w_ref[...], preferred_element_type=F32)
    o = 0
    for ref in (qkv_ref, dnz_ref, atq_ref, atk_ref, atv_ref, atz_ref, ba_ref):
        n = ref.shape[-1]
        ref[...] = p[:, o:o + n]
        o += n


def _in_proj(x2, nw, w_packed, tm):
    bt = x2.shape[0]
    widths = (DN_CONV_DIM, DN_QK, ATT_Q, ATT_KV, ATT_KV, ATT_Q, LANES)
    return pl.pallas_call(
        _in_proj_kernel,
        grid=(bt // tm,),
        in_specs=[pl.BlockSpec((tm, D_MODEL), lambda i: (i, 0)),
                  pl.BlockSpec((1, D_MODEL), lambda i: (0, 0)),
                  pl.BlockSpec((D_MODEL, PACK_COLS), lambda i: (0, 0))],
        out_specs=[pl.BlockSpec((tm, n), lambda i: (i, 0)) for n in widths],
        out_shape=[jax.ShapeDtypeStruct((bt, n), F32) for n in widths],
        compiler_params=_params(("parallel",)),
        name="in_proj",
    )(x2, nw, w_packed)


def _dn_prep_kernel(x_ref, prev_ref, next_ref, ba_ref, cw_ref, alog_ref, dtb_ref,
                    q_ref, k_ref, v_ref, gb_ref, xe_ref, *, tm):
    i = pl.program_id(1)
    n = pl.num_programs(1)
    halo = SUBLANES
    xe_ref[0:halo, :] = jnp.where(i > 0, prev_ref[...], 0.0)
    xe_ref[halo:halo + tm, :] = x_ref[...]
    xe_ref[halo + tm:2 * halo + tm, :] = jnp.where(i < n - 1, next_ref[...], 0.0)
    pad = CONV_K // 2
    y = jnp.zeros((tm, DN_CONV_DIM), F32)
    for j in range(CONV_K):
        y = y + xe_ref[halo + j - pad:halo + j - pad + tm, :] * cw_ref[j:j + 1, :]
    y = _silu(y)
    for hd in range(DN_HEADS):
        lo = hd * DN_HEAD
        q = y[:, lo:lo + DN_HEAD]
        k = y[:, DN_QK + lo:DN_QK + lo + DN_HEAD]
        q_ref[:, lo:lo + DN_HEAD] = q * lax.rsqrt(jnp.sum(q * q, axis=-1, keepdims=True) + EPS) * (DN_HEAD ** -0.5)
        k_ref[:, lo:lo + DN_HEAD] = k * lax.rsqrt(jnp.sum(k * k, axis=-1, keepdims=True) + EPS)
    v_ref[...] = y[:, 2 * DN_QK:]

    ba = ba_ref[...]
    beta = 1.0 / (1.0 + jnp.exp(-ba))
    z = ba + dtb_ref[...]
    softplus = jnp.maximum(z, 0.0) + jnp.log(1.0 + jnp.exp(-jnp.abs(z)))
    g = -jnp.exp(alog_ref[...]) * softplus
    r = lax.broadcasted_iota(jnp.int32, (tm, tm), 0)
    c = lax.broadcasted_iota(jnp.int32, (tm, tm), 1)
    same = (r // CHUNK) == (c // CHUNK)
    tri_f = jnp.where(same & (c <= r), 1.0, 0.0).astype(F32)
    tri_b = jnp.where(same & (c >= r), 1.0, 0.0).astype(F32)
    gc_f = jnp.dot(tri_f, g, precision=HI, preferred_element_type=F32)
    gc_b = jnp.dot(tri_b, g, precision=HI, preferred_element_type=F32)
    lane = lax.broadcasted_iota(jnp.int32, (tm, LANES), 1)
    gc = jnp.where(lane < DECAY_LANE + DN_HEADS, gc_f, gc_b)
    gb_ref[...] = jnp.where(lane < DECAY_LANE, beta, gc)


def _dn_prep(qkv, ba, cw, alog_row, dtb_row, tm):
    b, t, _ = qkv.shape
    nb = tm // SUBLANES
    last = t // SUBLANES - 1
    kern = functools.partial(_dn_prep_kernel, tm=tm)
    return pl.pallas_call(
        kern,
        grid=(b, t // tm),
        in_specs=[pl.BlockSpec((None, tm, DN_CONV_DIM), lambda bi, i: (bi, i, 0)),
                  pl.BlockSpec((None, SUBLANES, DN_CONV_DIM), lambda bi, i: (bi, jnp.maximum(i * nb - 1, 0), 0)),
                  pl.BlockSpec((None, SUBLANES, DN_CONV_DIM), lambda bi, i: (bi, jnp.minimum((i + 1) * nb, last), 0)),
                  pl.BlockSpec((None, tm, LANES), lambda bi, i: (bi, i, 0)),
                  pl.BlockSpec((SUBLANES, DN_CONV_DIM), lambda bi, i: (0, 0)),
                  pl.BlockSpec((1, LANES), lambda bi, i: (0, 0)),
                  pl.BlockSpec((1, LANES), lambda bi, i: (0, 0))],
        out_specs=[pl.BlockSpec((None, tm, DN_QK), lambda bi, i: (bi, i, 0))] * 3
        + [pl.BlockSpec((None, tm, LANES), lambda bi, i: (bi, i, 0))],
        out_shape=[jax.ShapeDtypeStruct((b, t, DN_QK), F32)] * 3 + [jax.ShapeDtypeStruct((b, t, LANES), F32)],
        scratch_shapes=[pltpu.VMEM((tm + 2 * SUBLANES, DN_CONV_DIM), F32)],
        compiler_params=_params(("parallel", "parallel")),
        name="dn_prep",
    )(qkv, qkv, qkv, ba, cw, alog_row, dtb_row)


def _mm(a, b):
    return jnp.dot(a.astype(BF16), b.astype(BF16), preferred_element_type=F32)


def _bmm(a, b):
    return jnp.einsum('cij,cjk->cik', a.astype(BF16), b.astype(BF16), preferred_element_type=F32)


def _bmm_nt(a, b):
    return jnp.einsum('cid,cjd->cij', a.astype(BF16), b.astype(BF16), preferred_element_type=F32)


def _dn_chunk_terms(d, q_ref, k_ref, v_ref, gb_ref, gr_ref, cb):
    ii = lax.broadcasted_iota(jnp.int32, (CHUNK, CHUNK), 0)
    jj = lax.broadcasted_iota(jnp.int32, (CHUNK, CHUNK), 1)
    if d == 1:
        ii, jj = jj, ii
    incl = ii >= jj
    strict = ii > jj
    eye = jnp.where(ii == jj, 1.0, 0.0).astype(F32)
    merge = []
    s = 1
    while s < CHUNK:
        merge.append((ii // (2 * s) == jj // (2 * s)) & ((ii // s) % 2 == 1) & ((jj // s) % 2 == 0))
        s *= 2
    gb = gb_ref[...]
    gr = gr_ref[...]

    def heads(ref):
        return jnp.concatenate([ref[:, hd * DN_HEAD:(hd + 1) * DN_HEAD].reshape(cb, CHUNK, DN_HEAD)
                                for hd in range(DN_HEADS)], axis=0)

    def gate_cols(lane0):
        return jnp.concatenate([gb[:, lane0 + hd:lane0 + hd + 1].reshape(cb, CHUNK, 1) for hd in range(DN_HEADS)], axis=0)

    q, k, v = heads(q_ref), heads(k_ref), heads(v_ref)
    beta = gate_cols(BETA_LANE + d * DN_HEADS)
    gcol = gate_cols(DECAY_LANE + d * DN_HEADS)
    gl = DECAY_LANE + d * DN_HEADS
    grow = jnp.concatenate([gr[:, gl + hd:gl + hd + 1, :] for hd in range(DN_HEADS)], axis=0)
    glast = grow[:, :, CHUNK - 1:CHUNK] if d == 0 else grow[:, :, 0:1]
    decay = jnp.where(incl, jnp.exp(jnp.minimum(gcol - grow, 0.0)), 0.0)
    kb = k * beta
    lower = jnp.where(strict, _bmm_nt(kb, k) * decay, 0.0)
    tinv = eye - jnp.where(merge[0], lower, 0.0)
    for m in merge[1:]:
        tinv = tinv - _bmm(_bmm(tinv, jnp.where(m, lower, 0.0)), tinv)
    rhs = jnp.concatenate([v * beta, kb * jnp.exp(gcol)], axis=-1)
    sol = _bmm(tinv, rhs)
    u, w = sol[..., :DN_HEAD], sol[..., DN_HEAD:]
    attn = (_bmm_nt(q, k) * decay).astype(BF16)
    wq = jnp.concatenate([w, q * jnp.exp(gcol)], axis=1).astype(BF16)
    kdec = (k * jnp.exp(glast - gcol)).astype(BF16)
    return u, wq, attn, kdec, jnp.exp(glast)


def _dn_step(d, c, pre, o_ref, s_ref, cb):
    u, wq, attn, kdec, eg = pre
    for hd in range(DN_HEADS):
        n = hd * cb + c
        idx = d * DN_HEADS + hd
        st = s_ref[idx]
        ws = _mm(wq[n], st)
        v_new = (u[n] - ws[:CHUNK]).astype(BF16)
        o = ws[CHUNK:] + _mm(attn[n], v_new)
        s_ref[idx] = st * eg[n] + lax.dot_general(kdec[n], v_new, (((0,), (0,)), ((), ())),
                                                  preferred_element_type=F32)
        o_ref[c * CHUNK:(c + 1) * CHUNK, hd * DN_HEAD:(hd + 1) * DN_HEAD] = o


def _dn_main_kernel(qf, kf, vf, gbf, grf, qb, kb, vb, gbb, grb, of_ref, ob_ref, s_ref, *, cb):
    @pl.when(pl.program_id(1) == 0)
    def _():
        s_ref[...] = jnp.zeros_like(s_ref)

    pre_f = _dn_chunk_terms(0, qf, kf, vf, gbf, grf, cb)
    pre_b = _dn_chunk_terms(1, qb, kb, vb, gbb, grb, cb)
    for i in range(cb):
        _dn_step(0, i, pre_f, of_ref, s_ref, cb)
        _dn_step(1, cb - 1 - i, pre_b, ob_ref, s_ref, cb)


def _dn_main(qn, kn, vv, gb, gr, cb):
    b, t, _ = qn.shape
    cbt = cb * CHUNK
    ng = t // cbt
    fwd = lambda bi, j: (bi, j, 0)
    bwd = lambda bi, j: (bi, ng - 1 - j, 0)
    fwd4 = lambda bi, j: (bi, j, 0, 0)
    bwd4 = lambda bi, j: (bi, ng - 1 - j, 0, 0)

    def specs(im3, im4):
        return ([pl.BlockSpec((None, cbt, DN_QK), im3)] * 3
                + [pl.BlockSpec((None, cbt, LANES), im3), pl.BlockSpec((None, cb, 2 * SUBLANES, CHUNK), im4)])

    kern = functools.partial(_dn_main_kernel, cb=cb)
    return pl.pallas_call(
        kern,
        grid=(b, ng),
        in_specs=specs(fwd, fwd4) + specs(bwd, bwd4),
        out_specs=[pl.BlockSpec((None, cbt, DN_QK), fwd), pl.BlockSpec((None, cbt, DN_QK), bwd)],
        out_shape=[jax.ShapeDtypeStruct((b, t, DN_QK), F32)] * 2,
        scratch_shapes=[pltpu.VMEM((N_DIR * DN_HEADS, DN_HEAD, DN_HEAD), F32)],
        compiler_params=_params(("parallel", "arbitrary")),
        name="dn_main",
    )(qn, kn, vv, gb, gr, qn, kn, vv, gb, gr)


def _rope_norm(x, w, seg, cos, sin):
    ss = jnp.dot(x * x, seg, precision=HI, preferred_element_type=F32)
    xn = x * lax.rsqrt(ss * (1.0 / ATT_HEAD_DIM) + EPS) * w
    quarter = ATT_HEAD_DIM // 4
    outs = []
    for t in range(x.shape[-1] // LANES):
        xt = xn[:, t * LANES:(t + 1) * LANES]
        lane = lax.broadcasted_iota(jnp.int32, xt.shape, 1)
        rot = jnp.where(lane % (2 * quarter) < quarter,
                        pltpu.roll(xt, LANES - quarter, axis=1), pltpu.roll(xt, quarter, axis=1))
        outs.append(xt * cos + rot * sin)
    return outs[0] if len(outs) == 1 else jnp.concatenate(outs, axis=-1)


def _attn_prep_kernel(q_ref, k_ref, v_ref, qw_ref, kw_ref, seg_ref, cos_ref, sin_ref, qo_ref, ko_ref, vo_ref):
    cos = cos_ref[...]
    sin = sin_ref[...]
    q = _rope_norm(q_ref[...], qw_ref[...], seg_ref[...], cos, sin)
    qo_ref[...] = (q * (LOG2E * ATT_HEAD_DIM ** -0.5)).astype(BF16)
    k = _rope_norm(k_ref[...], kw_ref[...], seg_ref[0:LANES, 0:LANES], cos, sin)
    ko_ref[...] = k.astype(BF16)
    vo_ref[...] = v_ref[...].astype(BF16)


def _attn_prep(atq, atk, atv, qw_row, kw_row, seg, cos_t, sin_t, tm):
    b, t, _ = atq.shape
    row = lambda bi, i: (bi, i, 0)
    const = lambda bi, i: (0, 0)
    return pl.pallas_call(
        _attn_prep_kernel,
        grid=(b, t // tm),
        in_specs=[pl.BlockSpec((None, tm, ATT_Q), row), pl.BlockSpec((None, tm, ATT_KV), row),
                  pl.BlockSpec((None, tm, ATT_KV), row),
                  pl.BlockSpec((1, ATT_Q), const), pl.BlockSpec((1, ATT_KV), const),
                  pl.BlockSpec((ATT_Q, ATT_Q), const),
                  pl.BlockSpec((tm, LANES), lambda bi, i: (i, 0)), pl.BlockSpec((tm, LANES), lambda bi, i: (i, 0))],
        out_specs=[pl.BlockSpec((None, tm, ATT_Q), row), pl.BlockSpec((None, tm, ATT_KV), row),
                   pl.BlockSpec((None, tm, ATT_KV), row)],
        out_shape=[jax.ShapeDtypeStruct((b, t, ATT_Q), BF16), jax.ShapeDtypeStruct((b, t, ATT_KV), BF16),
                   jax.ShapeDtypeStruct((b, t, ATT_KV), BF16)],
        compiler_params=_params(("parallel", "parallel")),
        name="attn_prep",
    )(atq, atk, atv, qw_row, kw_row, seg, cos_t, sin_t)


def _attn_kernel(qt_ref, k_ref, vt_ref, o_ref, s_buf, p_buf, *, tk):
    m_cols = qt_ref.shape[1]
    qt = qt_ref[...]
    nk = k_ref.shape[0] // tk
    ones = jnp.ones((2 * SUBLANES, tk), BF16)

    def scores(j, slot):
        start = pl.multiple_of(j * tk, tk)
        s = jnp.dot(k_ref[pl.ds(start, tk), :], qt, preferred_element_type=F32)
        s_buf[slot] = s
        return jnp.max(jnp.max(s.reshape(tk // SUBLANES, SUBLANES, m_cols), axis=0), axis=0, keepdims=True)

    def softmax(slot, m, chunk_max):
        m_new = jnp.maximum(m, chunk_max)
        p_buf[slot] = jnp.exp2(s_buf[slot] - m_new).astype(BF16)
        return m_new, jnp.exp2(m - m_new)

    def values(j, slot, alpha, acc):
        start = pl.multiple_of(j * tk, tk)
        vt = jnp.concatenate([vt_ref[:, pl.ds(start, tk)], ones], axis=0)
        return alpha * acc + jnp.dot(vt, p_buf[slot], preferred_element_type=F32)

    cmax0 = scores(0, 0)
    p_buf[1] = jnp.zeros((tk, m_cols), BF16)

    def body(i, carry):
        m, cmax, alpha_prev, acc = carry
        j0 = 2 * i
        cmax1 = scores(j0 + 1, 1)
        m, alpha0 = softmax(0, m, cmax)
        acc = values(jnp.maximum(j0 - 1, 0), 1, alpha_prev, acc)
        cmax2 = scores(jnp.minimum(j0 + 2, nk - 1), 0)
        m, alpha1 = softmax(1, m, cmax1)
        acc = values(j0, 0, alpha0, acc)
        return m, cmax2, alpha1, acc

    init = (jnp.full((1, m_cols), jnp.finfo(F32).min, F32), cmax0, jnp.ones((1, m_cols), F32),
            jnp.zeros((ATT_HEAD_DIM + 2 * SUBLANES, m_cols), F32))
    _, _, alpha, acc = lax.fori_loop(0, nk // 2, body, init)
    acc = values(nk - 1, 1, alpha, acc)
    o_ref[...] = acc[:ATT_HEAD_DIM] / acc[ATT_HEAD_DIM:ATT_HEAD_DIM + 1]


def _attention(qt5, k4, vt4, tk):
    b, _, nq, _, m_cols = qt5.shape
    t = k4.shape[2]
    assert t % (2 * tk) == 0
    kern = functools.partial(_attn_kernel, tk=tk)
    return pl.pallas_call(
        kern,
        grid=(b, ATT_KV_HEADS, nq),
        in_specs=[pl.BlockSpec((None, None, None, ATT_HEAD_DIM, m_cols), lambda bi, h, i: (bi, h, i, 0, 0)),
                  pl.BlockSpec((None, None, t, ATT_HEAD_DIM), lambda bi, h, i: (bi, h, 0, 0)),
                  pl.BlockSpec((None, None, ATT_HEAD_DIM, t), lambda bi, h, i: (bi, h, 0, 0))],
        out_specs=pl.BlockSpec((None, None, None, ATT_HEAD_DIM, m_cols), lambda bi, h, i: (bi, h, i, 0, 0)),
        out_shape=jax.ShapeDtypeStruct((b, ATT_KV_HEADS, nq, ATT_HEAD_DIM, m_cols), F32),
        scratch_shapes=[pltpu.VMEM((2, tk, m_cols), F32), pltpu.VMEM((2, tk, m_cols), BF16)],
        compiler_params=_params(("parallel", "parallel", "parallel")),
        name="attention",
    )(qt5, k4, vt4)


def _out_proj_kernel(of_ref, ob_ref, dnz_ref, dnw_ref, oat_ref, atz_ref, x_ref, w_ref, fw_ref, y_ref, *, final):
    o = of_ref[...] + ob_ref[...]
    parts = []
    for hd in range(DN_HEADS):
        oh = o[:, hd * DN_HEAD:(hd + 1) * DN_HEAD]
        parts.append(oh * lax.rsqrt(jnp.mean(oh * oh, axis=-1, keepdims=True) + EPS) * dnw_ref[...])
    y_dn = jnp.concatenate(parts, axis=-1) * _silu(dnz_ref[...])
    y_at = oat_ref[...] * _silu(atz_ref[...])
    y = jnp.concatenate([y_dn, y_at], axis=-1).astype(BF16)
    out = x_ref[...] + jnp.dot(y, w_ref[...], preferred_element_type=F32)
    if final:
        out = out * lax.rsqrt(jnp.mean(out * out, axis=-1, keepdims=True) + EPS) * fw_ref[...]
    y_ref[...] = out


def _out_proj(o_f, o_b, dnz, dnw_row, o_at, atz, x2, w_out, fw_row, tm, final):
    bt = x2.shape[0]
    row = lambda i: (i, 0)
    const = lambda i: (0, 0)
    kern = functools.partial(_out_proj_kernel, final=final)
    return pl.pallas_call(
        kern,
        grid=(bt // tm,),
        in_specs=[pl.BlockSpec((tm, DN_QK), row), pl.BlockSpec((tm, DN_QK), row), pl.BlockSpec((tm, DN_QK), row),
                  pl.BlockSpec((1, DN_HEAD), const),
                  pl.BlockSpec((tm, ATT_Q), row), pl.BlockSpec((tm, ATT_Q), row),
                  pl.BlockSpec((tm, D_MODEL), row),
                  pl.BlockSpec((D_MODEL, D_MODEL), const), pl.BlockSpec((1, D_MODEL), const)],
        out_specs=pl.BlockSpec((tm, D_MODEL), row),
        out_shape=jax.ShapeDtypeStruct((bt, D_MODEL), F32),
        compiler_params=_params(("parallel",)),
        name="out_proj_final" if final else "out_proj",
    )(o_f, o_b, dnz, dnw_row, o_at, atz, x2, w_out, fw_row)


def _pack_w_in(w):
    splits = np.cumsum(IN_SIZES)[:-1].tolist()
    qkv, dnz, bb, aa, atq, atk, atv, atz = jnp.split(w, splits, axis=-1)
    pad = jnp.zeros((w.shape[0], LANES - 2 * N_DIR * DN_HEADS), w.dtype)
    return jnp.concatenate([qkv, dnz, atq, atk, atv, atz, bb, aa, pad], axis=-1).astype(BF16)


def _rope_tables(t):
    half = ATT_HEAD_DIM // 2
    pos = np.arange(t)
    row = (pos // GRID_W).astype(np.float32)
    col = (pos % GRID_W).astype(np.float32)
    inv = jnp.asarray(ROPE_THETA, F32) ** (-jnp.arange(0, half, 2, dtype=F32) / half)
    ang_r = jnp.asarray(row)[:, None] * inv[None, :]
    ang_c = jnp.asarray(col)[:, None] * inv[None, :]
    cos = jnp.concatenate([jnp.cos(ang_r)] * 2 + [jnp.cos(ang_c)] * 2, axis=-1)
    sin = jnp.concatenate([-jnp.sin(ang_r), jnp.sin(ang_r), -jnp.sin(ang_c), jnp.sin(ang_c)], axis=-1)
    reps = LANES // ATT_HEAD_DIM
    return jnp.tile(cos, (1, reps)), jnp.tile(sin, (1, reps))


def _gate_row(p):
    return jnp.zeros((1, LANES), F32).at[0, DECAY_LANE:DECAY_LANE + N_DIR * DN_HEADS].set(p.reshape(-1))


def kernel(x, norm_w, w_in, conv_w, a_log, dt_bias, dn_norm_w, q_norm_w, k_norm_w, w_out, final_norm_w):
    b, t, d = x.shape
    depth = w_in.shape[0]
    tm = 256
    cb = min(8, t // CHUNK)
    tq = min(512, t)
    tk = min(512, t // 2)
    cos_t, sin_t = _rope_tables(t)
    lane = np.arange(ATT_Q)
    seg = jnp.asarray((lane[:, None] // ATT_HEAD_DIM) == (lane[None, :] // ATT_HEAD_DIM), F32)
    x2 = x.reshape(b * t, d)
    for l in range(depth):
        qkv, dnz, atq, atk, atv, atz, ba = _in_proj(x2, norm_w[l][None, :], _pack_w_in(w_in[l]), tm)
        cw = jnp.zeros((SUBLANES, DN_CONV_DIM), F32).at[:CONV_K].set(conv_w[l])
        qn, kn, vv, gb = _dn_prep(qkv.reshape(b, t, -1), ba.reshape(b, t, -1), cw,
                                  _gate_row(a_log[l]), _gate_row(dt_bias[l]), tm)
        gr = gb[..., :2 * SUBLANES].reshape(b, t // CHUNK, CHUNK, 2 * SUBLANES).transpose(0, 1, 3, 2)
        o_f, o_b = _dn_main(qn, kn, vv, gb, gr, cb)
        qr, kr, vb = _attn_prep(atq.reshape(b, t, -1), atk.reshape(b, t, -1), atv.reshape(b, t, -1),
                                jnp.tile(q_norm_w[l], ATT_HEADS)[None, :], jnp.tile(k_norm_w[l], ATT_KV_HEADS)[None, :],
                                seg, cos_t, sin_t, tm)
        nq = t // tq
        qt5 = qr.reshape(b, nq, tq, ATT_KV_HEADS, ATT_GROUP, ATT_HEAD_DIM).transpose(0, 3, 1, 5, 4, 2)
        qt5 = qt5.reshape(b, ATT_KV_HEADS, nq, ATT_HEAD_DIM, ATT_GROUP * tq)
        k4 = kr.reshape(b, t, ATT_KV_HEADS, ATT_HEAD_DIM).transpose(0, 2, 1, 3)
        vt4 = vb.reshape(b, t, ATT_KV_HEADS, ATT_HEAD_DIM).transpose(0, 2, 3, 1)
        ot = _attention(qt5, k4, vt4, tk)
        o_at = ot.reshape(b, ATT_KV_HEADS, nq, ATT_HEAD_DIM, ATT_GROUP, tq)
        o_at = o_at.transpose(0, 2, 5, 1, 4, 3).reshape(b * t, ATT_Q)
        x2 = _out_proj(o_f.reshape(b * t, -1), o_b.reshape(b * t, -1), dnz, dn_norm_w[l][None, :], o_at, atz,
                       x2, w_out[l].astype(BF16), final_norm_w[None, :], tm, final=(l == depth - 1))
    return x2.reshape(b, t, d)
```

```python
import functools

import jax
import jax.numpy as jnp
import numpy as np
from jax import lax
from jax.experimental import pallas as pl
from jax.experimental.pallas import tpu as pltpu

D_MODEL = 1024
DN_HEADS = 4
DN_HEAD = 128
DN_QK = DN_HEADS * DN_HEAD
DN_CONV_DIM = 3 * DN_QK
CONV_K = 5
CHUNK = 64
N_DIR = 2
ATT_HEADS = 8
ATT_KV_HEADS = 2
ATT_GROUP = ATT_HEADS // ATT_KV_HEADS
ATT_HEAD_DIM = 64
ATT_Q = ATT_HEADS * ATT_HEAD_DIM
ATT_KV = ATT_KV_HEADS * ATT_HEAD_DIM
ROPE_THETA = 10000.0
GRID_W = 64
EPS = 1e-6
IN_SIZES = (DN_CONV_DIM, DN_QK, N_DIR * DN_HEADS, N_DIR * DN_HEADS, ATT_Q, ATT_KV, ATT_KV, ATT_Q)
LANES = 128
SUBLANES = 8
VMEM_LIMIT = 48 * 1024 * 1024
PACK_COLS = DN_CONV_DIM + DN_QK + ATT_Q + ATT_KV + ATT_KV + ATT_Q + LANES
BETA_LANE = 0
DECAY_LANE = N_DIR * DN_HEADS

F32 = jnp.float32
BF16 = jnp.bfloat16
HI = lax.Precision.HIGHEST
LOG2E = 1.4426950408889634


def _params(sem):
    return pltpu.CompilerParams(dimension_semantics=sem, vmem_limit_bytes=VMEM_LIMIT)


def _silu(x):
    return x / (1.0 + jnp.exp(-x))


def _in_proj_kernel(x_ref, nw_ref, w_ref, qkv_ref, dnz_ref, atq_ref, atk_ref, atv_ref, atz_ref, ba_ref):
    x = x_ref[...]
    h = x * lax.rsqrt(jnp.mean(x * x, axis=-1, keepdims=True) + EPS) * nw_ref[...]
    p = jnp.dot(h.astype(BF16), w_ref[...], preferred_element_type=F32)
    o = 0
    for ref in (qkv_ref, dnz_ref, atq_ref, atk_ref, atv_ref, atz_ref, ba_ref):
        n = ref.shape[-1]
        ref[...] = p[:, o:o + n]
        o += n


def _in_proj(x2, nw, w_packed, tm):
    bt = x2.shape[0]
    widths = (DN_CONV_DIM, DN_QK, ATT_Q, ATT_KV, ATT_KV, ATT_Q, LANES)
    return pl.pallas_call(
        _in_proj_kernel,
        grid=(bt // tm,),
        in_specs=[pl.BlockSpec((tm, D_MODEL), lambda i: (i, 0)),
                  pl.BlockSpec((1, D_MODEL), lambda i: (0, 0)),
                  pl.BlockSpec((D_MODEL, PACK_COLS), lambda i: (0, 0))],
        out_specs=[pl.BlockSpec((tm, n), lambda i: (i, 0)) for n in widths],
        out_shape=[jax.ShapeDtypeStruct((bt, n), F32) for n in widths],
        compiler_params=_params(("parallel",)),
        name="in_proj",
    )(x2, nw, w_packed)


def _dn_prep_kernel(x_ref, prev_ref, next_ref, ba_ref, cw_ref, alog_ref, dtb_ref,
                    q_ref, k_ref, v_ref, gb_ref, xe_ref, *, tm):
    i = pl.program_id(1)
    n = pl.num_programs(1)
    halo = SUBLANES
    xe_ref[0:halo, :] = jnp.where(i > 0, prev_ref[...], 0.0)
    xe_ref[halo:halo + tm, :] = x_ref[...]
    xe_ref[halo + tm:2 * halo + tm, :] = jnp.where(i < n - 1, next_ref[...], 0.0)
    pad = CONV_K // 2
    y = jnp.zeros((tm, DN_CONV_DIM), F32)
    for j in range(CONV_K):
        y = y + xe_ref[halo + j - pad:halo + j - pad + tm, :] * cw_ref[j:j + 1, :]
    y = _silu(y)
    for hd in range(DN_HEADS):
        lo = hd * DN_HEAD
        q = y[:, lo:lo + DN_HEAD]
        k = y[:, DN_QK + lo:DN_QK + lo + DN_HEAD]
        q_ref[:, lo:lo + DN_HEAD] = q * lax.rsqrt(jnp.sum(q * q, axis=-1, keepdims=True) + EPS) * (DN_HEAD ** -0.5)
        k_ref[:, lo:lo + DN_HEAD] = k * lax.rsqrt(jnp.sum(k * k, axis=-1, keepdims=True) + EPS)
    v_ref[...] = y[:, 2 * DN_QK:]

    ba = ba_ref[...]
    beta = 1.0 / (1.0 + jnp.exp(-ba))
    z = ba + dtb_ref[...]
    softplus = jnp.maximum(z, 0.0) + jnp.log(1.0 + jnp.exp(-jnp.abs(z)))
    g = -jnp.exp(alog_ref[...]) * softplus
    r = lax.broadcasted_iota(jnp.int32, (tm, tm), 0)
    c = lax.broadcasted_iota(jnp.int32, (tm, tm), 1)
    same = (r // CHUNK) == (c // CHUNK)
    tri_f = jnp.where(same & (c <= r), 1.0, 0.0).astype(F32)
    tri_b = jnp.where(same & (c >= r), 1.0, 0.0).astype(F32)
    gc_f = jnp.dot(tri_f, g, precision=HI, preferred_element_type=F32)
    gc_b = jnp.dot(tri_b, g, precision=HI, preferred_element_type=F32)
    lane = lax.broadcasted_iota(jnp.int32, (tm, LANES), 1)
    gc = jnp.where(lane < DECAY_LANE + DN_HEADS, gc_f, gc_b)
    gb_ref[...] = jnp.where(lane < DECAY_LANE, beta, gc)


def _dn_prep(qkv, ba, cw, alog_row, dtb_row, tm):
    b, t, _ = qkv.shape
    nb = tm // SUBLANES
    last = t // SUBLANES - 1
    kern = functools.partial(_dn_prep_kernel, tm=tm)
    return pl.pallas_call(
        kern,
        grid=(b, t // tm),
        in_specs=[pl.BlockSpec((None, tm, DN_CONV_DIM), lambda bi, i: (bi, i, 0)),
                  pl.BlockSpec((None, SUBLANES, DN_CONV_DIM), lambda bi, i: (bi, jnp.maximum(i * nb - 1, 0), 0)),
                  pl.BlockSpec((None, SUBLANES, DN_CONV_DIM), lambda bi, i: (bi, jnp.minimum((i + 1) * nb, last), 0)),
                  pl.BlockSpec((None, tm, LANES), lambda bi, i: (bi, i, 0)),
                  pl.BlockSpec((SUBLANES, DN_CONV_DIM), lambda bi, i: (0, 0)),
                  pl.BlockSpec((1, LANES), lambda bi, i: (0, 0)),
                  pl.BlockSpec((1, LANES), lambda bi, i: (0, 0))],
        out_specs=[pl.BlockSpec((None, tm, DN_QK), lambda bi, i: (bi, i, 0))] * 3
        + [pl.BlockSpec((None, tm, LANES), lambda bi, i: (bi, i, 0))],
        out_shape=[jax.ShapeDtypeStruct((b, t, DN_QK), F32)] * 3 + [jax.ShapeDtypeStruct((b, t, LANES), F32)],
        scratch_shapes=[pltpu.VMEM((tm + 2 * SUBLANES, DN_CONV_DIM), F32)],
        compiler_params=_params(("parallel", "parallel")),
        name="dn_prep",
    )(qkv, qkv, qkv, ba, cw, alog_row, dtb_row)


def _mm(a, b):
    return jnp.dot(a.astype(BF16), b.astype(BF16), preferred_element_type=F32)


def _bmm(a, b):
    return jnp.einsum('cij,cjk->cik', a.astype(BF16), b.astype(BF16), preferred_element_type=F32)


def _bmm_nt(a, b):
    return jnp.einsum('cid,cjd->cij', a.astype(BF16), b.astype(BF16), preferred_element_type=F32)


def _dn_chunk_terms(d, q_ref, k_ref, v_ref, gb_ref, gr_ref, cb):
    ii = lax.broadcasted_iota(jnp.int32, (CHUNK, CHUNK), 0)
    jj = lax.broadcasted_iota(jnp.int32, (CHUNK, CHUNK), 1)
    if d == 1:
        ii, jj = jj, ii
    incl = ii >= jj
    strict = ii > jj
    eye = jnp.where(ii == jj, 1.0, 0.0).astype(F32)
    merge = []
    s = 1
    while s < CHUNK:
        merge.append((ii // (2 * s) == jj // (2 * s)) & ((ii // s) % 2 == 1) & ((jj // s) % 2 == 0))
        s *= 2
    gb = gb_ref[...]
    gr = gr_ref[...]

    def heads(ref):
        return jnp.concatenate([ref[:, hd * DN_HEAD:(hd + 1) * DN_HEAD].reshape(cb, CHUNK, DN_HEAD)
                                for hd in range(DN_HEADS)], axis=0)

    def gate_cols(lane0):
        return jnp.concatenate([gb[:, lane0 + hd:lane0 + hd + 1].reshape(cb, CHUNK, 1) for hd in range(DN_HEADS)], axis=0)

    q, k, v = heads(q_ref), heads(k_ref), heads(v_ref)
    beta = gate_cols(BETA_LANE + d * DN_HEADS)
    gcol = gate_cols(DECAY_LANE + d * DN_HEADS)
    gl = DECAY_LANE + d * DN_HEADS
    grow = jnp.concatenate([gr[:, gl + hd:gl + hd + 1, :] for hd in range(DN_HEADS)], axis=0)
    glast = grow[:, :, CHUNK - 1:CHUNK] if d == 0 else grow[:, :, 0:1]
    decay = jnp.where(incl, jnp.exp(jnp.minimum(gcol - grow, 0.0)), 0.0)
    kb = k * beta
    lower = jnp.where(strict, _bmm_nt(kb, k) * decay, 0.0)
    tinv = eye - jnp.where(merge[0], lower, 0.0)
    for m in merge[1:]:
        tinv = tinv - _bmm(_bmm(tinv, jnp.where(m, lower, 0.0)), tinv)
    rhs = jnp.concatenate([v * beta, kb * jnp.exp(gcol)], axis=-1)
    sol = _bmm(tinv, rhs)
    u, w = sol[..., :DN_HEAD], sol[..., DN_HEAD:]
    attn = (_bmm_nt(q, k) * decay).astype(BF16)
    wq = jnp.concatenate([w, q * jnp.exp(gcol)], axis=1).astype(BF16)
    kdec = (k * jnp.exp(glast - gcol)).astype(BF16)
    return u, wq, attn, kdec, jnp.exp(glast)


def _dn_step(d, c, pre, o_ref, s_ref, cb):
    u, wq, attn, kdec, eg = pre
    for hd in range(DN_HEADS):
        n = hd * cb + c
        idx = d * DN_HEADS + hd
        st = s_ref[idx]
        ws = _mm(wq[n], st)
        v_new = (u[n] - ws[:CHUNK]).astype(BF16)
        o = ws[CHUNK:] + _mm(attn[n], v_new)
        s_ref[idx] = st * eg[n] + lax.dot_general(kdec[n], v_new, (((0,), (0,)), ((), ())),
                                                  preferred_element_type=F32)
        o_ref[c * CHUNK:(c + 1) * CHUNK, hd * DN_HEAD:(hd + 1) * DN_HEAD] = o


def _dn_main_kernel(qf, kf, vf, gbf, grf, qb, kb, vb, gbb, grb, of_ref, ob_ref, s_ref, *, cb):
    @pl.when(pl.program_id(1) == 0)
    def _():
        s_ref[...] = jnp.zeros_like(s_ref)

    pre_f = _dn_chunk_terms(0, qf, kf, vf, gbf, grf, cb)
    pre_b = _dn_chunk_terms(1, qb, kb, vb, gbb, grb, cb)
    for i in range(cb):
        _dn_step(0, i, pre_f, of_ref, s_ref, cb)
        _dn_step(1, cb - 1 - i, pre_b, ob_ref, s_ref, cb)


def _dn_main(qn, kn, vv, gb, gr, cb):
    b, t, _ = qn.shape
    cbt = cb * CHUNK
    ng = t // cbt
    fwd = lambda bi, j: (bi, j, 0)
    bwd = lambda bi, j: (bi, ng - 1 - j, 0)
    fwd4 = lambda bi, j: (bi, j, 0, 0)
    bwd4 = lambda bi, j: (bi, ng - 1 - j, 0, 0)

    def specs(im3, im4):
        return ([pl.BlockSpec((None, cbt, DN_QK), im3)] * 3
                + [pl.BlockSpec((None, cbt, LANES), im3), pl.BlockSpec((None, cb, 2 * SUBLANES, CHUNK), im4)])

    kern = functools.partial(_dn_main_kernel, cb=cb)
    return pl.pallas_call(
        kern,
        grid=(b, ng),
        in_specs=specs(fwd, fwd4) + specs(bwd, bwd4),
        out_specs=[pl.BlockSpec((None, cbt, DN_QK), fwd), pl.BlockSpec((None, cbt, DN_QK), bwd)],
        out_shape=[jax.ShapeDtypeStruct((b, t, DN_QK), F32)] * 2,
        scratch_shapes=[pltpu.VMEM((N_DIR * DN_HEADS, DN_HEAD, DN_HEAD), F32)],
        compiler_params=_params(("parallel", "arbitrary")),
        name="dn_main",
    )(qn, kn, vv, gb, gr, qn, kn, vv, gb, gr)


def _rope_norm(x, w, seg, cos, sin):
    ss = jnp.dot(x * x, seg, precision=HI, preferred_element_type=F32)
    xn = x * lax.rsqrt(ss * (1.0 / ATT_HEAD_DIM) + EPS) * w
    quarter = ATT_HEAD_DIM // 4
    outs = []
    for t in range(x.shape[-1] // LANES):
        xt = xn[:, t * LANES:(t + 1) * LANES]
        lane = lax.broadcasted_iota(jnp.int32, xt.shape, 1)
        rot = jnp.where(lane % (2 * quarter) < quarter,
                        pltpu.roll(xt, LANES - quarter, axis=1), pltpu.roll(xt, quarter, axis=1))
        outs.append(xt * cos + rot * sin)
    return outs[0] if len(outs) == 1 else jnp.concatenate(outs, axis=-1)


def _attn_prep_kernel(q_ref, k_ref, v_ref, qw_ref, kw_ref, seg_ref, cos_ref, sin_ref, qo_ref, ko_ref, vo_ref):
    cos = cos_ref[...]
    sin = sin_ref[...]
    q = _rope_norm(q_ref[...], qw_ref[...], seg_ref[...], cos, sin)
    qt = (q * (LOG2E * ATT_HEAD_DIM ** -0.5)).T.astype(BF16)
    for kvh in range(ATT_KV_HEADS):
        rows = [qt[(kvh * ATT_GROUP + g) * ATT_HEAD_DIM:(kvh * ATT_GROUP + g + 1) * ATT_HEAD_DIM] for g in range(ATT_GROUP)]
        qo_ref[kvh] = jnp.concatenate(rows, axis=1)
    k = _rope_norm(k_ref[...], kw_ref[...], seg_ref[0:LANES, 0:LANES], cos, sin).astype(BF16)
    vt = v_ref[...].T.astype(BF16)
    for kvh in range(ATT_KV_HEADS):
        ko_ref[kvh] = k[:, kvh * ATT_HEAD_DIM:(kvh + 1) * ATT_HEAD_DIM]
        vo_ref[kvh] = vt[kvh * ATT_HEAD_DIM:(kvh + 1) * ATT_HEAD_DIM]


def _attn_prep(atq, atk, atv, qw_row, kw_row, seg, cos_t, sin_t, tq):
    b, t, _ = atq.shape
    nq = t // tq
    row = lambda bi, i: (bi, i, 0)
    const = lambda bi, i: (0, 0)
    return pl.pallas_call(
        _attn_prep_kernel,
        grid=(b, nq),
        in_specs=[pl.BlockSpec((None, tq, ATT_Q), row), pl.BlockSpec((None, tq, ATT_KV), row),
                  pl.BlockSpec((None, tq, ATT_KV), row),
                  pl.BlockSpec((1, ATT_Q), const), pl.BlockSpec((1, ATT_KV), const),
                  pl.BlockSpec((ATT_Q, ATT_Q), const),
                  pl.BlockSpec((tq, LANES), lambda bi, i: (i, 0)), pl.BlockSpec((tq, LANES), lambda bi, i: (i, 0))],
        out_specs=[pl.BlockSpec((None, ATT_KV_HEADS, None, ATT_HEAD_DIM, ATT_GROUP * tq), lambda bi, i: (bi, 0, i, 0, 0)),
                   pl.BlockSpec((None, ATT_KV_HEADS, tq, ATT_HEAD_DIM), lambda bi, i: (bi, 0, i, 0)),
                   pl.BlockSpec((None, ATT_KV_HEADS, ATT_HEAD_DIM, tq), lambda bi, i: (bi, 0, 0, i))],
        out_shape=[jax.ShapeDtypeStruct((b, ATT_KV_HEADS, nq, ATT_HEAD_DIM, ATT_GROUP * tq), BF16),
                   jax.ShapeDtypeStruct((b, ATT_KV_HEADS, t, ATT_HEAD_DIM), BF16),
                   jax.ShapeDtypeStruct((b, ATT_KV_HEADS, ATT_HEAD_DIM, t), BF16)],
        compiler_params=_params(("parallel", "parallel")),
        name="attn_prep",
    )(atq, atk, atv, qw_row, kw_row, seg, cos_t, sin_t)


def _attn_kernel(qt_ref, k_ref, vt_ref, o_ref, s_buf, p_buf, *, tk):
    m_cols = qt_ref.shape[1]
    qt = qt_ref[...]
    nk = k_ref.shape[0] // tk
    ones = jnp.ones((2 * SUBLANES, tk), BF16)

    def scores(j, slot):
        start = pl.multiple_of(j * tk, tk)
        s = jnp.dot(k_ref[pl.ds(start, tk), :], qt, preferred_element_type=F32)
        s_buf[slot] = s
        return jnp.max(jnp.max(s.reshape(tk // SUBLANES, SUBLANES, m_cols), axis=0), axis=0, keepdims=True)

    def softmax(slot, m, chunk_max):
        m_new = jnp.maximum(m, chunk_max)
        p_buf[slot] = jnp.exp2(s_buf[slot] - m_new).astype(BF16)
        return m_new, jnp.exp2(m - m_new)

    def values(j, slot, alpha, acc):
        start = pl.multiple_of(j * tk, tk)
        vt = jnp.concatenate([vt_ref[:, pl.ds(start, tk)], ones], axis=0)
        return alpha * acc + jnp.dot(vt, p_buf[slot], preferred_element_type=F32)

    cmax0 = scores(0, 0)
    p_buf[1] = jnp.zeros((tk, m_cols), BF16)

    def body(i, carry):
        m, cmax, alpha_prev, acc = carry
        j0 = 2 * i
        cmax1 = scores(j0 + 1, 1)
        m, alpha0 = softmax(0, m, cmax)
        acc = values(jnp.maximum(j0 - 1, 0), 1, alpha_prev, acc)
        cmax2 = scores(jnp.minimum(j0 + 2, nk - 1), 0)
        m, alpha1 = softmax(1, m, cmax1)
        acc = values(j0, 0, alpha0, acc)
        return m, cmax2, alpha1, acc

    init = (jnp.full((1, m_cols), jnp.finfo(F32).min, F32), cmax0, jnp.ones((1, m_cols), F32),
            jnp.zeros((ATT_HEAD_DIM + 2 * SUBLANES, m_cols), F32))
    _, _, alpha, acc = lax.fori_loop(0, nk // 2, body, init)
    acc = values(nk - 1, 1, alpha, acc)
    o = acc[:ATT_HEAD_DIM] / acc[ATT_HEAD_DIM:ATT_HEAD_DIM + 1]
    tq = m_cols // ATT_GROUP
    o = jnp.concatenate([o[:, g * tq:(g + 1) * tq] for g in range(ATT_GROUP)], axis=0)
    o_ref[...] = o.T


def _attention(qt5, k4, vt4, tk):
    b, _, nq, _, m_cols = qt5.shape
    t = k4.shape[2]
    tq = m_cols // ATT_GROUP
    assert t % (2 * tk) == 0
    kern = functools.partial(_attn_kernel, tk=tk)
    return pl.pallas_call(
        kern,
        grid=(b, ATT_KV_HEADS, nq),
        in_specs=[pl.BlockSpec((None, None, None, ATT_HEAD_DIM, m_cols), lambda bi, h, i: (bi, h, i, 0, 0)),
                  pl.BlockSpec((None, None, t, ATT_HEAD_DIM), lambda bi, h, i: (bi, h, 0, 0)),
                  pl.BlockSpec((None, None, ATT_HEAD_DIM, t), lambda bi, h, i: (bi, h, 0, 0))],
        out_specs=pl.BlockSpec((None, tq, ATT_GROUP * ATT_HEAD_DIM), lambda bi, h, i: (bi, i, h)),
        out_shape=jax.ShapeDtypeStruct((b, t, ATT_Q), F32),
        scratch_shapes=[pltpu.VMEM((2, tk, m_cols), F32), pltpu.VMEM((2, tk, m_cols), BF16)],
        compiler_params=_params(("parallel", "parallel", "parallel")),
        name="attention",
    )(qt5, k4, vt4)


def _out_proj_kernel(of_ref, ob_ref, dnz_ref, dnw_ref, oat_ref, atz_ref, x_ref, w_ref, fw_ref, y_ref, *, final):
    o = of_ref[...] + ob_ref[...]
    parts = []
    for hd in range(DN_HEADS):
        oh = o[:, hd * DN_HEAD:(hd + 1) * DN_HEAD]
        parts.append(oh * lax.rsqrt(jnp.mean(oh * oh, axis=-1, keepdims=True) + EPS) * dnw_ref[...])
    y_dn = jnp.concatenate(parts, axis=-1) * _silu(dnz_ref[...])
    y_at = oat_ref[...] * _silu(atz_ref[...])
    y = jnp.concatenate([y_dn, y_at], axis=-1).astype(BF16)
    out = x_ref[...] + jnp.dot(y, w_ref[...], preferred_element_type=F32)
    if final:
        out = out * lax.rsqrt(jnp.mean(out * out, axis=-1, keepdims=True) + EPS) * fw_ref[...]
    y_ref[...] = out


def _out_proj(o_f, o_b, dnz, dnw_row, o_at, atz, x2, w_out, fw_row, tm, final):
    bt = x2.shape[0]
    row = lambda i: (i, 0)
    const = lambda i: (0, 0)
    kern = functools.partial(_out_proj_kernel, final=final)
    return pl.pallas_call(
        kern,
        grid=(bt // tm,),
        in_specs=[pl.BlockSpec((tm, DN_QK), row), pl.BlockSpec((tm, DN_QK), row), pl.BlockSpec((tm, DN_QK), row),
                  pl.BlockSpec((1, DN_HEAD), const),
                  pl.BlockSpec((tm, ATT_Q), row), pl.BlockSpec((tm, ATT_Q), row),
                  pl.BlockSpec((tm, D_MODEL), row),
                  pl.BlockSpec((D_MODEL, D_MODEL), const), pl.BlockSpec((1, D_MODEL), const)],
        out_specs=pl.BlockSpec((tm, D_MODEL), row),
        out_shape=jax.ShapeDtypeStruct((bt, D_MODEL), F32),
        compiler_params=_params(("parallel",)),
        name="out_proj_final" if final else "out_proj",
    )(o_f, o_b, dnz, dnw_row, o_at, atz, x2, w_out, fw_row)


def _pack_w_in(w):
    splits = np.cumsum(IN_SIZES)[:-1].tolist()
    qkv, dnz, bb, aa, atq, atk, atv, atz = jnp.split(w, splits, axis=-1)
    pad = jnp.zeros((w.shape[0], LANES - 2 * N_DIR * DN_HEADS), w.dtype)
    return jnp.concatenate([qkv, dnz, atq, atk, atv, atz, bb, aa, pad], axis=-1).astype(BF16)


def _rope_tables(t):
    half = ATT_HEAD_DIM // 2
    pos = np.arange(t)
    row = (pos // GRID_W).astype(np.float32)
    col = (pos % GRID_W).astype(np.float32)
    inv = jnp.asarray(ROPE_THETA, F32) ** (-jnp.arange(0, half, 2, dtype=F32) / half)
    ang_r = jnp.asarray(row)[:, None] * inv[None, :]
    ang_c = jnp.asarray(col)[:, None] * inv[None, :]
    cos = jnp.concatenate([jnp.cos(ang_r)] * 2 + [jnp.cos(ang_c)] * 2, axis=-1)
    sin = jnp.concatenate([-jnp.sin(ang_r), jnp.sin(ang_r), -jnp.sin(ang_c), jnp.sin(ang_c)], axis=-1)
    reps = LANES // ATT_HEAD_DIM
    return jnp.tile(cos, (1, reps)), jnp.tile(sin, (1, reps))


def _gate_row(p):
    return jnp.zeros((1, LANES), F32).at[0, DECAY_LANE:DECAY_LANE + N_DIR * DN_HEADS].set(p.reshape(-1))


def kernel(x, norm_w, w_in, conv_w, a_log, dt_bias, dn_norm_w, q_norm_w, k_norm_w, w_out, final_norm_w):
    b, t, d = x.shape
    depth = w_in.shape[0]
    tm = 256
    cb = min(8, t // CHUNK)
    tq = min(512, t)
    tk = min(512, t // 2)
    cos_t, sin_t = _rope_tables(t)
    lane = np.arange(ATT_Q)
    seg = jnp.asarray((lane[:, None] // ATT_HEAD_DIM) == (lane[None, :] // ATT_HEAD_DIM), F32)
    x2 = x.reshape(b * t, d)
    for l in range(depth):
        qkv, dnz, atq, atk, atv, atz, ba = _in_proj(x2, norm_w[l][None, :], _pack_w_in(w_in[l]), tm)
        cw = jnp.zeros((SUBLANES, DN_CONV_DIM), F32).at[:CONV_K].set(conv_w[l])
        qn, kn, vv, gb = _dn_prep(qkv.reshape(b, t, -1), ba.reshape(b, t, -1), cw,
                                  _gate_row(a_log[l]), _gate_row(dt_bias[l]), tm)
        gr = gb[..., :2 * SUBLANES].reshape(b, t // CHUNK, CHUNK, 2 * SUBLANES).transpose(0, 1, 3, 2)
        o_f, o_b = _dn_main(qn, kn, vv, gb, gr, cb)
        qt5, k4, vt4 = _attn_prep(atq.reshape(b, t, -1), atk.reshape(b, t, -1), atv.reshape(b, t, -1),
                                  jnp.tile(q_norm_w[l], ATT_HEADS)[None, :], jnp.tile(k_norm_w[l], ATT_KV_HEADS)[None, :],
                                  seg, cos_t, sin_t, tq)
        o_at = _attention(qt5, k4, vt4, tk).reshape(b * t, ATT_Q)
        x2 = _out_proj(o_f.reshape(b * t, -1), o_b.reshape(b * t, -1), dnz, dn_norm_w[l][None, :], o_at, atz,
                       x2, w_out[l].astype(BF16), final_norm_w[None, :], tm, final=(l == depth - 1))
    return x2.reshape(b, t, d)
```

```python
import functools

import jax
import jax.numpy as jnp
import numpy as np
from jax import lax
from jax.experimental import pallas as pl
from jax.experimental.pallas import tpu as pltpu

D_MODEL = 1024
DN_HEADS = 4
DN_HEAD = 128
DN_QK = DN_HEADS * DN_HEAD
DN_CONV_DIM = 3 * DN_QK
CONV_K = 5
CHUNK = 128
N_DIR = 2
ATT_HEADS = 8
ATT_KV_HEADS = 2
ATT_GROUP = ATT_HEADS // ATT_KV_HEADS
ATT_HEAD_DIM = 64
ATT_Q = ATT_HEADS * ATT_HEAD_DIM
ATT_KV = ATT_KV_HEADS * ATT_HEAD_DIM
ROPE_THETA = 10000.0
GRID_W = 64
EPS = 1e-6
IN_SIZES = (DN_CONV_DIM, DN_QK, N_DIR * DN_HEADS, N_DIR * DN_HEADS, ATT_Q, ATT_KV, ATT_KV, ATT_Q)
LANES = 128
SUBLANES = 8
VMEM_LIMIT = 48 * 1024 * 1024
PACK_COLS = DN_CONV_DIM + DN_QK + ATT_Q + ATT_KV + ATT_KV + ATT_Q + LANES
BETA_LANE = 0
DECAY_LANE = N_DIR * DN_HEADS
DN_STREAM_HEADS = 4
MXU_WIDTH = 256
DN_PACK = MXU_WIDTH // CHUNK

F32 = jnp.float32
BF16 = jnp.bfloat16
HI = lax.Precision.HIGHEST
LOG2E = 1.4426950408889634


def _params(sem):
    return pltpu.CompilerParams(dimension_semantics=sem, vmem_limit_bytes=VMEM_LIMIT)


def _silu(x):
    return x / (1.0 + jnp.exp(-x))


def _in_proj_kernel(x_ref, nw_ref, w_ref, qkv_ref, dnz_ref, atq_ref, atk_ref, atv_ref, atz_ref, ba_ref):
    x = x_ref[...]
    h = x * lax.rsqrt(jnp.mean(x * x, axis=-1, keepdims=True) + EPS) * nw_ref[...]
    p = jnp.dot(h.astype(BF16), w_ref[...], preferred_element_type=F32)
    o = 0
    for ref in (qkv_ref, dnz_ref, atq_ref, atk_ref, atv_ref, atz_ref, ba_ref):
        n = ref.shape[-1]
        ref[...] = p[:, o:o + n]
        o += n


def _in_proj(x2, nw, w_packed, tm):
    bt = x2.shape[0]
    widths = (DN_CONV_DIM, DN_QK, ATT_Q, ATT_KV, ATT_KV, ATT_Q, LANES)
    return pl.pallas_call(
        _in_proj_kernel,
        grid=(bt // tm,),
        in_specs=[pl.BlockSpec((tm, D_MODEL), lambda i: (i, 0)),
                  pl.BlockSpec((1, D_MODEL), lambda i: (0, 0)),
                  pl.BlockSpec((D_MODEL, PACK_COLS), lambda i: (0, 0))],
        out_specs=[pl.BlockSpec((tm, n), lambda i: (i, 0)) for n in widths],
        out_shape=[jax.ShapeDtypeStruct((bt, n), F32) for n in widths],
        compiler_params=_params(("parallel",)),
        name="in_proj",
    )(x2, nw, w_packed)


def _dn_prep_kernel(x_ref, prev_ref, next_ref, ba_ref, cw_ref, alog_ref, dtb_ref,
                    q_ref, k_ref, v_ref, gb_ref, xe_ref, *, tm):
    i = pl.program_id(1)
    n = pl.num_programs(1)
    halo = SUBLANES
    xe_ref[0:halo, :] = jnp.where(i > 0, prev_ref[...], 0.0)
    xe_ref[halo:halo + tm, :] = x_ref[...]
    xe_ref[halo + tm:2 * halo + tm, :] = jnp.where(i < n - 1, next_ref[...], 0.0)
    pad = CONV_K // 2
    y = jnp.zeros((tm, DN_CONV_DIM), F32)
    for j in range(CONV_K):
        y = y + xe_ref[halo + j - pad:halo + j - pad + tm, :] * cw_ref[j:j + 1, :]
    y = _silu(y)
    for hd in range(DN_HEADS):
        lo = hd * DN_HEAD
        q = y[:, lo:lo + DN_HEAD]
        k = y[:, DN_QK + lo:DN_QK + lo + DN_HEAD]
        q_ref[:, lo:lo + DN_HEAD] = q * lax.rsqrt(jnp.sum(q * q, axis=-1, keepdims=True) + EPS) * (DN_HEAD ** -0.5)
        k_ref[:, lo:lo + DN_HEAD] = k * lax.rsqrt(jnp.sum(k * k, axis=-1, keepdims=True) + EPS)
    v_ref[...] = y[:, 2 * DN_QK:]

    ba = ba_ref[...]
    beta = 1.0 / (1.0 + jnp.exp(-ba))
    z = ba + dtb_ref[...]
    softplus = jnp.maximum(z, 0.0) + jnp.log(1.0 + jnp.exp(-jnp.abs(z)))
    g = -jnp.exp(alog_ref[...]) * softplus
    r = lax.broadcasted_iota(jnp.int32, (tm, tm), 0)
    c = lax.broadcasted_iota(jnp.int32, (tm, tm), 1)
    same = (r // CHUNK) == (c // CHUNK)
    tri_f = jnp.where(same & (c <= r), 1.0, 0.0).astype(F32)
    tri_b = jnp.where(same & (c >= r), 1.0, 0.0).astype(F32)
    gc_f = jnp.dot(tri_f, g, precision=HI, preferred_element_type=F32)
    gc_b = jnp.dot(tri_b, g, precision=HI, preferred_element_type=F32)
    lane = lax.broadcasted_iota(jnp.int32, (tm, LANES), 1)
    gc = jnp.where(lane < DECAY_LANE + DN_HEADS, gc_f, gc_b)
    gb_ref[...] = jnp.where(lane < DECAY_LANE, beta, gc)


def _dn_prep(qkv, ba, cw, alog_row, dtb_row, tm):
    b, t, _ = qkv.shape
    nb = tm // SUBLANES
    last = t // SUBLANES - 1
    kern = functools.partial(_dn_prep_kernel, tm=tm)
    return pl.pallas_call(
        kern,
        grid=(b, t // tm),
        in_specs=[pl.BlockSpec((None, tm, DN_CONV_DIM), lambda bi, i: (bi, i, 0)),
                  pl.BlockSpec((None, SUBLANES, DN_CONV_DIM), lambda bi, i: (bi, jnp.maximum(i * nb - 1, 0), 0)),
                  pl.BlockSpec((None, SUBLANES, DN_CONV_DIM), lambda bi, i: (bi, jnp.minimum((i + 1) * nb, last), 0)),
                  pl.BlockSpec((None, tm, LANES), lambda bi, i: (bi, i, 0)),
                  pl.BlockSpec((SUBLANES, DN_CONV_DIM), lambda bi, i: (0, 0)),
                  pl.BlockSpec((1, LANES), lambda bi, i: (0, 0)),
                  pl.BlockSpec((1, LANES), lambda bi, i: (0, 0))],
        out_specs=[pl.BlockSpec((None, tm, DN_QK), lambda bi, i: (bi, i, 0))] * 3
        + [pl.BlockSpec((None, tm, LANES), lambda bi, i: (bi, i, 0))],
        out_shape=[jax.ShapeDtypeStruct((b, t, DN_QK), F32)] * 3 + [jax.ShapeDtypeStruct((b, t, LANES), F32)],
        scratch_shapes=[pltpu.VMEM((tm + 2 * SUBLANES, DN_CONV_DIM), F32)],
        compiler_params=_params(("parallel", "parallel")),
        name="dn_prep",
    )(qkv, qkv, qkv, ba, cw, alog_row, dtb_row)


def _mm(a, b):
    return jnp.dot(a.astype(BF16), b.astype(BF16), preferred_element_type=F32)


def _bmm(a, b):
    return jnp.einsum('cij,cjk->cik', a.astype(BF16), b.astype(BF16), preferred_element_type=F32)


def _bmm_nt(a, b):
    return jnp.einsum('cid,cjd->cij', a.astype(BF16), b.astype(BF16), preferred_element_type=F32)


def _dn_masks(d, pack=1):
    ii = lax.broadcasted_iota(jnp.int32, (CHUNK, pack * CHUNK), 0)
    jj = lax.broadcasted_iota(jnp.int32, (CHUNK, pack * CHUNK), 1) % CHUNK
    if d == 1:
        ii, jj = jj, ii
    eye = jnp.where(ii == jj, 1.0, 0.0).astype(F32)
    merge = []
    s = 1
    while s < CHUNK:
        merge.append((ii // (2 * s) == jj // (2 * s)) & ((ii // s) % 2 == 1) & ((jj // s) % 2 == 0))
        s *= 2
    return ii >= jj, ii > jj, eye, merge


def _dn_chunk_terms(streams, cb):
    masks = {d: _dn_masks(d) for d in sorted({s[0] for s in streams})}
    data = []
    for d, hds, (q_ref, k_ref, v_ref, gb_ref, gr_ref) in streams:
        gb = gb_ref[...]
        gr = gr_ref[...]

        def heads(ref):
            return jnp.concatenate([ref[:, hd * DN_HEAD:(hd + 1) * DN_HEAD].reshape(cb, CHUNK, DN_HEAD)
                                    for hd in hds], axis=0)

        def gate_cols(lane0):
            return jnp.concatenate([gb[:, lane0 + hd:lane0 + hd + 1].reshape(cb, CHUNK, 1) for hd in hds], axis=0)

        q, k, v = heads(q_ref), heads(k_ref), heads(v_ref)
        beta = gate_cols(BETA_LANE + d * DN_HEADS)
        gcol = gate_cols(DECAY_LANE + d * DN_HEADS)
        gl = DECAY_LANE + d * DN_HEADS
        grow = jnp.concatenate([gr[:, gl + hd:gl + hd + 1, :] for hd in hds], axis=0)
        glast = grow[:, :, CHUNK - 1:CHUNK] if d == 0 else grow[:, :, 0:1]
        decay = jnp.where(masks[d][0], jnp.exp(jnp.minimum(gcol - grow, 0.0)), 0.0)
        data.append(dict(d=d, q=q, k=k, v=v, beta=beta, gcol=gcol, glast=glast, decay=decay, kb=k * beta))
    n = range(len(data))
    lower = [jnp.where(masks[x['d']][1], _bmm_nt(x['kb'], x['k']) * x['decay'], 0.0) for x in data]
    wmask = {d: _dn_masks(d, DN_PACK) for d in masks}
    unit = lax.broadcasted_iota(jnp.int32, (CHUNK, DN_PACK * CHUNK), 1) // CHUNK

    def wide(x):
        x4 = x.reshape(x.shape[0] // DN_PACK, DN_PACK, CHUNK, CHUNK)
        return jnp.concatenate([x4[:, j] for j in range(DN_PACK)], axis=-1)

    def block_diag(xw):
        return jnp.concatenate([jnp.where(unit == j, xw, 0.0) for j in range(DN_PACK)], axis=1)

    lower_w = [wide(lower[i]) for i in n]
    tinv = [wmask[data[i]['d']][2] - jnp.where(wmask[data[i]['d']][3][0], lower_w[i], 0.0) for i in n]
    for lvl in range(1, len(wmask[data[0]['d']][3])):
        te = [_bmm(tinv[i], block_diag(jnp.where(wmask[data[i]['d']][3][lvl], lower_w[i], 0.0))) for i in n]
        tinv = [tinv[i] - _bmm(te[i], block_diag(tinv[i])) for i in n]
    sol = []
    for i, x in enumerate(data):
        rhs = jnp.concatenate([x['v'] * x['beta'], x['kb'] * jnp.exp(x['gcol'])], axis=-1)
        units = rhs.shape[0]
        s_tall = _bmm(block_diag(tinv[i]), rhs.reshape(units // DN_PACK, DN_PACK * CHUNK, 2 * DN_HEAD))
        sol.append(s_tall.reshape(units, CHUNK, 2 * DN_HEAD))
    out = []
    for i, x in enumerate(data):
        u, w = sol[i][..., :DN_HEAD], sol[i][..., DN_HEAD:]
        attn = (_bmm_nt(x['q'], x['k']) * x['decay']).astype(BF16)
        wq = jnp.concatenate([w, x['q'] * jnp.exp(x['gcol'])], axis=1).astype(BF16)
        kdec = (x['k'] * jnp.exp(x['glast'] - x['gcol'])).astype(BF16)
        out.append((u, wq, attn, kdec, jnp.exp(x['glast'])))
    return out


def _dn_step(d, hds, c, pre, o_ref, s_ref, cb):
    u, wq, attn, kdec, eg = pre
    for i, hd in enumerate(hds):
        n = i * cb + c
        idx = d * DN_HEADS + hd
        st = s_ref[idx]
        ws = _mm(wq[n], st)
        v_new = (u[n] - ws[:CHUNK]).astype(BF16)
        o = ws[CHUNK:] + _mm(attn[n], v_new)
        s_ref[idx] = st * eg[n] + lax.dot_general(kdec[n], v_new, (((0,), (0,)), ((), ())),
                                                  preferred_element_type=F32)
        o_ref[c * CHUNK:(c + 1) * CHUNK, hd * DN_HEAD:(hd + 1) * DN_HEAD] = o


def _dn_main_kernel(qf, kf, vf, gbf, grf, qb, kb, vb, gbb, grb, of_ref, ob_ref, s_ref, *, cb):
    @pl.when(pl.program_id(1) == 0)
    def _():
        s_ref[...] = jnp.zeros_like(s_ref)

    groups = [tuple(range(h, h + DN_STREAM_HEADS)) for h in range(0, DN_HEADS, DN_STREAM_HEADS)]
    streams = ([(0, hds, (qf, kf, vf, gbf, grf)) for hds in groups]
               + [(1, hds, (qb, kb, vb, gbb, grb)) for hds in groups])
    pre = _dn_chunk_terms(streams, cb)
    o_refs = (of_ref, ob_ref)
    for i in range(cb):
        for (d, hds, _), p in zip(streams, pre):
            _dn_step(d, hds, i if d == 0 else cb - 1 - i, p, o_refs[d], s_ref, cb)


def _dn_main(qn, kn, vv, gb, gr, cb):
    b, t, _ = qn.shape
    cbt = cb * CHUNK
    ng = t // cbt
    fwd = lambda bi, j: (bi, j, 0)
    bwd = lambda bi, j: (bi, ng - 1 - j, 0)
    fwd4 = lambda bi, j: (bi, j, 0, 0)
    bwd4 = lambda bi, j: (bi, ng - 1 - j, 0, 0)

    def specs(im3, im4):
        return ([pl.BlockSpec((None, cbt, DN_QK), im3)] * 3
                + [pl.BlockSpec((None, cbt, LANES), im3), pl.BlockSpec((None, cb, 2 * SUBLANES, CHUNK), im4)])

    kern = functools.partial(_dn_main_kernel, cb=cb)
    return pl.pallas_call(
        kern,
        grid=(b, ng),
        in_specs=specs(fwd, fwd4) + specs(bwd, bwd4),
        out_specs=[pl.BlockSpec((None, cbt, DN_QK), fwd), pl.BlockSpec((None, cbt, DN_QK), bwd)],
        out_shape=[jax.ShapeDtypeStruct((b, t, DN_QK), F32)] * 2,
        scratch_shapes=[pltpu.VMEM((N_DIR * DN_HEADS, DN_HEAD, DN_HEAD), F32)],
        compiler_params=_params(("parallel", "arbitrary")),
        name="dn_main",
    )(qn, kn, vv, gb, gr, qn, kn, vv, gb, gr)


def _rope_norm(x, w, seg, cos, sin):
    sq = x * x
    sq_hi = sq.astype(BF16)
    sq_lo = (sq - sq_hi.astype(F32)).astype(BF16)
    ss = jnp.dot(sq_hi, seg, preferred_element_type=F32) + jnp.dot(sq_lo, seg, preferred_element_type=F32)
    xn = x * lax.rsqrt(ss * (1.0 / ATT_HEAD_DIM) + EPS) * w
    quarter = ATT_HEAD_DIM // 4
    outs = []
    for t in range(x.shape[-1] // LANES):
        xt = xn[:, t * LANES:(t + 1) * LANES]
        lane = lax.broadcasted_iota(jnp.int32, xt.shape, 1)
        rot = jnp.where(lane % (2 * quarter) < quarter,
                        pltpu.roll(xt, LANES - quarter, axis=1), pltpu.roll(xt, quarter, axis=1))
        outs.append(xt * cos + rot * sin)
    return outs[0] if len(outs) == 1 else jnp.concatenate(outs, axis=-1)


def _attn_prep_kernel(q_ref, k_ref, v_ref, qw_ref, kw_ref, seg_ref, cos_ref, sin_ref, qo_ref, ko_ref, vo_ref):
    cos = cos_ref[...]
    sin = sin_ref[...]
    q = _rope_norm(q_ref[...], qw_ref[...], seg_ref[...], cos, sin)
    qt = (q * (LOG2E * ATT_HEAD_DIM ** -0.5)).T.astype(BF16)
    for kvh in range(ATT_KV_HEADS):
        rows = [qt[(kvh * ATT_GROUP + g) * ATT_HEAD_DIM:(kvh * ATT_GROUP + g + 1) * ATT_HEAD_DIM] for g in range(ATT_GROUP)]
        qo_ref[kvh] = jnp.concatenate(rows, axis=1)
    k = _rope_norm(k_ref[...], kw_ref[...], seg_ref[0:LANES, 0:LANES], cos, sin).astype(BF16)
    vt = v_ref[...].T.astype(BF16)
    for kvh in range(ATT_KV_HEADS):
        ko_ref[kvh] = k[:, kvh * ATT_HEAD_DIM:(kvh + 1) * ATT_HEAD_DIM]
        vo_ref[kvh] = vt[kvh * ATT_HEAD_DIM:(kvh + 1) * ATT_HEAD_DIM]


def _attn_prep(atq, atk, atv, qw_row, kw_row, seg, cos_t, sin_t, tq):
    b, t, _ = atq.shape
    nq = t // tq
    row = lambda bi, i: (bi, i, 0)
    const = lambda bi, i: (0, 0)
    return pl.pallas_call(
        _attn_prep_kernel,
        grid=(b, nq),
        in_specs=[pl.BlockSpec((None, tq, ATT_Q), row), pl.BlockSpec((None, tq, ATT_KV), row),
                  pl.BlockSpec((None, tq, ATT_KV), row),
                  pl.BlockSpec((1, ATT_Q), const), pl.BlockSpec((1, ATT_KV), const),
                  pl.BlockSpec((ATT_Q, ATT_Q), const),
                  pl.BlockSpec((tq, LANES), lambda bi, i: (i, 0)), pl.BlockSpec((tq, LANES), lambda bi, i: (i, 0))],
        out_specs=[pl.BlockSpec((None, ATT_KV_HEADS, None, ATT_HEAD_DIM, ATT_GROUP * tq), lambda bi, i: (bi, 0, i, 0, 0)),
                   pl.BlockSpec((None, ATT_KV_HEADS, tq, ATT_HEAD_DIM), lambda bi, i: (bi, 0, i, 0)),
                   pl.BlockSpec((None, ATT_KV_HEADS, ATT_HEAD_DIM, tq), lambda bi, i: (bi, 0, 0, i))],
        out_shape=[jax.ShapeDtypeStruct((b, ATT_KV_HEADS, nq, ATT_HEAD_DIM, ATT_GROUP * tq), BF16),
                   jax.ShapeDtypeStruct((b, ATT_KV_HEADS, t, ATT_HEAD_DIM), BF16),
                   jax.ShapeDtypeStruct((b, ATT_KV_HEADS, ATT_HEAD_DIM, t), BF16)],
        compiler_params=_params(("parallel", "parallel")),
        name="attn_prep",
    )(atq, atk, atv, qw_row, kw_row, seg, cos_t, sin_t)


def _attn_kernel(qt_ref, k_ref, vt_ref, o_ref, s_buf, p_buf, *, tk):
    m_cols = qt_ref.shape[1]
    qt = qt_ref[...]
    nk = k_ref.shape[0] // tk
    ones = jnp.ones((2 * SUBLANES, tk), BF16)

    def scores(j, slot):
        start = pl.multiple_of(j * tk, tk)
        s = jnp.dot(k_ref[pl.ds(start, tk), :], qt, preferred_element_type=F32)
        s_buf[slot] = s
        return jnp.max(jnp.max(s.reshape(tk // SUBLANES, SUBLANES, m_cols), axis=0), axis=0, keepdims=True)

    def softmax(slot, m, chunk_max):
        m_new = jnp.maximum(m, chunk_max)
        p_buf[slot] = jnp.exp2(s_buf[slot] - m_new).astype(BF16)
        return m_new, jnp.exp2(m - m_new)

    def values(j, slot, alpha, acc):
        start = pl.multiple_of(j * tk, tk)
        vt = jnp.concatenate([vt_ref[:, pl.ds(start, tk)], ones], axis=0)
        return alpha * acc + jnp.dot(vt, p_buf[slot], preferred_element_type=F32)

    cmax0 = scores(0, 0)
    p_buf[1] = jnp.zeros((tk, m_cols), BF16)

    def body(i, carry):
        m, cmax, alpha_prev, acc = carry
        j0 = 2 * i
        cmax1 = scores(j0 + 1, 1)
        m, alpha0 = softmax(0, m, cmax)
        acc = values(jnp.maximum(j0 - 1, 0), 1, alpha_prev, acc)
        cmax2 = scores(jnp.minimum(j0 + 2, nk - 1), 0)
        m, alpha1 = softmax(1, m, cmax1)
        acc = values(j0, 0, alpha0, acc)
        return m, cmax2, alpha1, acc

    init = (jnp.full((1, m_cols), jnp.finfo(F32).min, F32), cmax0, jnp.ones((1, m_cols), F32),
            jnp.zeros((ATT_HEAD_DIM + 2 * SUBLANES, m_cols), F32))
    _, _, alpha, acc = lax.fori_loop(0, nk // 2, body, init)
    acc = values(nk - 1, 1, alpha, acc)
    o = acc[:ATT_HEAD_DIM] / acc[ATT_HEAD_DIM:ATT_HEAD_DIM + 1]
    tq = m_cols // ATT_GROUP
    o = jnp.concatenate([o[:, g * tq:(g + 1) * tq] for g in range(ATT_GROUP)], axis=0)
    o_ref[...] = o.T


def _attention(qt5, k4, vt4, tk):
    b, _, nq, _, m_cols = qt5.shape
    t = k4.shape[2]
    tq = m_cols // ATT_GROUP
    assert t % (2 * tk) == 0
    kern = functools.partial(_attn_kernel, tk=tk)
    return pl.pallas_call(
        kern,
        grid=(b, ATT_KV_HEADS, nq),
        in_specs=[pl.BlockSpec((None, None, None, ATT_HEAD_DIM, m_cols), lambda bi, h, i: (bi, h, i, 0, 0)),
                  pl.BlockSpec((None, None, t, ATT_HEAD_DIM), lambda bi, h, i: (bi, h, 0, 0)),
                  pl.BlockSpec((None, None, ATT_HEAD_DIM, t), lambda bi, h, i: (bi, h, 0, 0))],
        out_specs=pl.BlockSpec((None, tq, ATT_GROUP * ATT_HEAD_DIM), lambda bi, h, i: (bi, i, h)),
        out_shape=jax.ShapeDtypeStruct((b, t, ATT_Q), F32),
        scratch_shapes=[pltpu.VMEM((2, tk, m_cols), F32), pltpu.VMEM((2, tk, m_cols), BF16)],
        compiler_params=_params(("parallel", "parallel", "parallel")),
        name="attention",
    )(qt5, k4, vt4)


def _out_proj_kernel(of_ref, ob_ref, dnz_ref, dnw_ref, oat_ref, atz_ref, x_ref, w_ref, fw_ref, y_ref, *, final):
    o = of_ref[...] + ob_ref[...]
    parts = []
    for hd in range(DN_HEADS):
        oh = o[:, hd * DN_HEAD:(hd + 1) * DN_HEAD]
        parts.append(oh * lax.rsqrt(jnp.mean(oh * oh, axis=-1, keepdims=True) + EPS) * dnw_ref[...])
    y_dn = jnp.concatenate(parts, axis=-1) * _silu(dnz_ref[...])
    y_at = oat_ref[...] * _silu(atz_ref[...])
    y = jnp.concatenate([y_dn, y_at], axis=-1).astype(BF16)
    out = x_ref[...] + jnp.dot(y, w_ref[...], preferred_element_type=F32)
    if final:
        out = out * lax.rsqrt(jnp.mean(out * out, axis=-1, keepdims=True) + EPS) * fw_ref[...]
    y_ref[...] = out


def _out_proj(o_f, o_b, dnz, dnw_row, o_at, atz, x2, w_out, fw_row, tm, final):
    bt = x2.shape[0]
    row = lambda i: (i, 0)
    const = lambda i: (0, 0)
    kern = functools.partial(_out_proj_kernel, final=final)
    return pl.pallas_call(
        kern,
        grid=(bt // tm,),
        in_specs=[pl.BlockSpec((tm, DN_QK), row), pl.BlockSpec((tm, DN_QK), row), pl.BlockSpec((tm, DN_QK), row),
                  pl.BlockSpec((1, DN_HEAD), const),
                  pl.BlockSpec((tm, ATT_Q), row), pl.BlockSpec((tm, ATT_Q), row),
                  pl.BlockSpec((tm, D_MODEL), row),
                  pl.BlockSpec((D_MODEL, D_MODEL), const), pl.BlockSpec((1, D_MODEL), const)],
        out_specs=pl.BlockSpec((tm, D_MODEL), row),
        out_shape=jax.ShapeDtypeStruct((bt, D_MODEL), F32),
        compiler_params=_params(("parallel",)),
        name="out_proj_final" if final else "out_proj",
    )(o_f, o_b, dnz, dnw_row, o_at, atz, x2, w_out, fw_row)


def _pack_w_in(w):
    splits = np.cumsum(IN_SIZES)[:-1].tolist()
    qkv, dnz, bb, aa, atq, atk, atv, atz = jnp.split(w, splits, axis=-1)
    pad = jnp.zeros((w.shape[0], LANES - 2 * N_DIR * DN_HEADS), w.dtype)
    return jnp.concatenate([qkv, dnz, atq, atk, atv, atz, bb, aa, pad], axis=-1).astype(BF16)


def _rope_tables(t):
    half = ATT_HEAD_DIM // 2
    pos = np.arange(t)
    row = (pos // GRID_W).astype(np.float32)
    col = (pos % GRID_W).astype(np.float32)
    inv = jnp.asarray(ROPE_THETA, F32) ** (-jnp.arange(0, half, 2, dtype=F32) / half)
    ang_r = jnp.asarray(row)[:, None] * inv[None, :]
    ang_c = jnp.asarray(col)[:, None] * inv[None, :]
    cos = jnp.concatenate([jnp.cos(ang_r)] * 2 + [jnp.cos(ang_c)] * 2, axis=-1)
    sin = jnp.concatenate([-jnp.sin(ang_r), jnp.sin(ang_r), -jnp.sin(ang_c), jnp.sin(ang_c)], axis=-1)
    reps = LANES // ATT_HEAD_DIM
    return jnp.tile(cos, (1, reps)), jnp.tile(sin, (1, reps))


def _gate_row(p):
    return jnp.zeros((1, LANES), F32).at[0, DECAY_LANE:DECAY_LANE + N_DIR * DN_HEADS].set(p.reshape(-1))


def kernel(x, norm_w, w_in, conv_w, a_log, dt_bias, dn_norm_w, q_norm_w, k_norm_w, w_out, final_norm_w):
    b, t, d = x.shape
    depth = w_in.shape[0]
    tm = 256
    cb = min(4, t // CHUNK)
    tq = min(512, t)
    tk = min(512, t // 2)
    cos_t, sin_t = _rope_tables(t)
    lane = np.arange(ATT_Q)
    seg = jnp.asarray((lane[:, None] // ATT_HEAD_DIM) == (lane[None, :] // ATT_HEAD_DIM), BF16)
    x2 = x.reshape(b * t, d)
    for l in range(depth):
        qkv, dnz, atq, atk, atv, atz, ba = _in_proj(x2, norm_w[l][None, :], _pack_w_in(w_in[l]), tm)
        cw = jnp.zeros((SUBLANES, DN_CONV_DIM), F32).at[:CONV_K].set(conv_w[l])
        qn, kn, vv, gb = _dn_prep(qkv.reshape(b, t, -1), ba.reshape(b, t, -1), cw,
                                  _gate_row(a_log[l]), _gate_row(dt_bias[l]), tm)
        gr = gb[..., :2 * SUBLANES].reshape(b, t // CHUNK, CHUNK, 2 * SUBLANES).transpose(0, 1, 3, 2)
        o_f, o_b = _dn_main(qn, kn, vv, gb, gr, cb)
        qt5, k4, vt4 = _attn_prep(atq.reshape(b, t, -1), atk.reshape(b, t, -1), atv.reshape(b, t, -1),
                                  jnp.tile(q_norm_w[l], ATT_HEADS)[None, :], jnp.tile(k_norm_w[l], ATT_KV_HEADS)[None, :],
                                  seg, cos_t, sin_t, tq)
        o_at = _attention(qt5, k4, vt4, tk).reshape(b * t, ATT_Q)
        x2 = _out_proj(o_f.reshape(b * t, -1), o_b.reshape(b * t, -1), dnz, dn_norm_w[l][None, :], o_at, atz,
                       x2, w_out[l].astype(BF16), final_norm_w[None, :], tm, final=(l == depth - 1))
    return x2.reshape(b, t, d)
```

```python
import functools

import jax
import jax.numpy as jnp
import numpy as np
from jax import lax
from jax.experimental import pallas as pl
from jax.experimental.pallas import tpu as pltpu

D_MODEL = 1024
DN_HEADS = 4
DN_HEAD = 128
DN_QK = DN_HEADS * DN_HEAD
DN_CONV_DIM = 3 * DN_QK
CONV_K = 5
CHUNK = 128
N_DIR = 2
ATT_HEADS = 8
ATT_KV_HEADS = 2
ATT_GROUP = ATT_HEADS // ATT_KV_HEADS
ATT_HEAD_DIM = 64
ATT_Q = ATT_HEADS * ATT_HEAD_DIM
ATT_KV = ATT_KV_HEADS * ATT_HEAD_DIM
ROPE_THETA = 10000.0
GRID_W = 64
EPS = 1e-6
IN_SIZES = (DN_CONV_DIM, DN_QK, N_DIR * DN_HEADS, N_DIR * DN_HEADS, ATT_Q, ATT_KV, ATT_KV, ATT_Q)
LANES = 128
SUBLANES = 8
VMEM_LIMIT = 48 * 1024 * 1024
PACK_COLS = DN_CONV_DIM + DN_QK + ATT_Q + ATT_KV + ATT_KV + ATT_Q + LANES
BETA_LANE = 0
DECAY_LANE = N_DIR * DN_HEADS
DN_STREAM_HEADS = 4
MXU_WIDTH = 256
DN_PACK = MXU_WIDTH // CHUNK

F32 = jnp.float32
BF16 = jnp.bfloat16
HI = lax.Precision.HIGHEST
LOG2E = 1.4426950408889634


def _params(sem):
    return pltpu.CompilerParams(dimension_semantics=sem, vmem_limit_bytes=VMEM_LIMIT)


def _silu(x):
    return x / (1.0 + jnp.exp(-x))


def _in_proj_kernel(x_ref, nw_ref, w_ref, qkv_ref, dnz_ref, atq_ref, atk_ref, atv_ref, atz_ref, ba_ref):
    x = x_ref[...]
    h = x * lax.rsqrt(jnp.mean(x * x, axis=-1, keepdims=True) + EPS) * nw_ref[...]
    p = jnp.dot(h.astype(BF16), w_ref[...], preferred_element_type=F32)
    o = 0
    for ref in (qkv_ref, dnz_ref, atq_ref, atk_ref, atv_ref, atz_ref, ba_ref):
        n = ref.shape[-1]
        ref[...] = p[:, o:o + n].astype(ref.dtype)
        o += n


def _in_proj(x2, nw, w_packed, tm):
    bt = x2.shape[0]
    widths = (DN_CONV_DIM, DN_QK, ATT_Q, ATT_KV, ATT_KV, ATT_Q, LANES)
    dtypes = (F32, BF16, BF16, BF16, BF16, BF16, F32)
    return pl.pallas_call(
        _in_proj_kernel,
        grid=(bt // tm,),
        in_specs=[pl.BlockSpec((tm, D_MODEL), lambda i: (i, 0)),
                  pl.BlockSpec((1, D_MODEL), lambda i: (0, 0)),
                  pl.BlockSpec((D_MODEL, PACK_COLS), lambda i: (0, 0))],
        out_specs=[pl.BlockSpec((tm, n), lambda i: (i, 0)) for n in widths],
        out_shape=[jax.ShapeDtypeStruct((bt, n), dt) for n, dt in zip(widths, dtypes)],
        compiler_params=_params(("parallel",)),
        name="in_proj",
    )(x2, nw, w_packed)


def _dn_prep_kernel(x_ref, prev_ref, next_ref, ba_ref, cw_ref, alog_ref, dtb_ref,
                    q_ref, k_ref, v_ref, gb_ref, xe_ref, *, tm):
    i = pl.program_id(1)
    n = pl.num_programs(1)
    halo = SUBLANES
    xe_ref[0:halo, :] = jnp.where(i > 0, prev_ref[...], 0.0)
    xe_ref[halo:halo + tm, :] = x_ref[...]
    xe_ref[halo + tm:2 * halo + tm, :] = jnp.where(i < n - 1, next_ref[...], 0.0)
    pad = CONV_K // 2
    y = jnp.zeros((tm, DN_CONV_DIM), F32)
    for j in range(CONV_K):
        y = y + xe_ref[halo + j - pad:halo + j - pad + tm, :] * cw_ref[j:j + 1, :]
    y = _silu(y)
    for hd in range(DN_HEADS):
        lo = hd * DN_HEAD
        q = y[:, lo:lo + DN_HEAD]
        k = y[:, DN_QK + lo:DN_QK + lo + DN_HEAD]
        q_ref[:, lo:lo + DN_HEAD] = q * lax.rsqrt(jnp.sum(q * q, axis=-1, keepdims=True) + EPS) * (DN_HEAD ** -0.5)
        k_ref[:, lo:lo + DN_HEAD] = k * lax.rsqrt(jnp.sum(k * k, axis=-1, keepdims=True) + EPS)
    v_ref[...] = y[:, 2 * DN_QK:]

    ba = ba_ref[...]
    beta = 1.0 / (1.0 + jnp.exp(-ba))
    z = ba + dtb_ref[...]
    softplus = jnp.maximum(z, 0.0) + jnp.log(1.0 + jnp.exp(-jnp.abs(z)))
    g = -jnp.exp(alog_ref[...]) * softplus
    r = lax.broadcasted_iota(jnp.int32, (tm, tm), 0)
    c = lax.broadcasted_iota(jnp.int32, (tm, tm), 1)
    same = (r // CHUNK) == (c // CHUNK)
    tri_f = jnp.where(same & (c <= r), 1.0, 0.0).astype(F32)
    tri_b = jnp.where(same & (c >= r), 1.0, 0.0).astype(F32)
    gc_f = jnp.dot(tri_f, g, precision=HI, preferred_element_type=F32)
    gc_b = jnp.dot(tri_b, g, precision=HI, preferred_element_type=F32)
    lane = lax.broadcasted_iota(jnp.int32, (tm, LANES), 1)
    gc = jnp.where(lane < DECAY_LANE + DN_HEADS, gc_f, gc_b)
    gb_ref[...] = jnp.where(lane < DECAY_LANE, beta, gc)


def _dn_prep(qkv, ba, cw, alog_row, dtb_row, tm):
    b, t, _ = qkv.shape
    nb = tm // SUBLANES
    last = t // SUBLANES - 1
    kern = functools.partial(_dn_prep_kernel, tm=tm)
    return pl.pallas_call(
        kern,
        grid=(b, t // tm),
        in_specs=[pl.BlockSpec((None, tm, DN_CONV_DIM), lambda bi, i: (bi, i, 0)),
                  pl.BlockSpec((None, SUBLANES, DN_CONV_DIM), lambda bi, i: (bi, jnp.maximum(i * nb - 1, 0), 0)),
                  pl.BlockSpec((None, SUBLANES, DN_CONV_DIM), lambda bi, i: (bi, jnp.minimum((i + 1) * nb, last), 0)),
                  pl.BlockSpec((None, tm, LANES), lambda bi, i: (bi, i, 0)),
                  pl.BlockSpec((SUBLANES, DN_CONV_DIM), lambda bi, i: (0, 0)),
                  pl.BlockSpec((1, LANES), lambda bi, i: (0, 0)),
                  pl.BlockSpec((1, LANES), lambda bi, i: (0, 0))],
        out_specs=[pl.BlockSpec((None, tm, DN_QK), lambda bi, i: (bi, i, 0))] * 3
        + [pl.BlockSpec((None, tm, LANES), lambda bi, i: (bi, i, 0))],
        out_shape=[jax.ShapeDtypeStruct((b, t, DN_QK), F32)] * 3 + [jax.ShapeDtypeStruct((b, t, LANES), F32)],
        scratch_shapes=[pltpu.VMEM((tm + 2 * SUBLANES, DN_CONV_DIM), F32)],
        compiler_params=_params(("parallel", "parallel")),
        name="dn_prep",
    )(qkv, qkv, qkv, ba, cw, alog_row, dtb_row)


def _mm(a, b):
    return jnp.dot(a.astype(BF16), b.astype(BF16), preferred_element_type=F32)


def _bmm(a, b):
    return jnp.einsum('cij,cjk->cik', a.astype(BF16), b.astype(BF16), preferred_element_type=F32)


def _bmm_nt(a, b):
    return jnp.einsum('cid,cjd->cij', a.astype(BF16), b.astype(BF16), preferred_element_type=F32)


def _dn_masks(d, pack=1):
    ii = lax.broadcasted_iota(jnp.int32, (CHUNK, pack * CHUNK), 0)
    jj = lax.broadcasted_iota(jnp.int32, (CHUNK, pack * CHUNK), 1) % CHUNK
    if d == 1:
        ii, jj = jj, ii
    eye = jnp.where(ii == jj, 1.0, 0.0).astype(F32)
    merge = []
    s = 1
    while s < CHUNK:
        merge.append((ii // (2 * s) == jj // (2 * s)) & ((ii // s) % 2 == 1) & ((jj // s) % 2 == 0))
        s *= 2
    return ii >= jj, ii > jj, eye, merge


def _dn_chunk_terms(streams, cb):
    masks = {d: _dn_masks(d) for d in sorted({s[0] for s in streams})}
    data = []
    for d, hds, (q_ref, k_ref, v_ref, gb_ref, gr_ref) in streams:
        gb = gb_ref[...]
        gr = gr_ref[...]

        def heads(ref):
            return jnp.concatenate([ref[:, hd * DN_HEAD:(hd + 1) * DN_HEAD].reshape(cb, CHUNK, DN_HEAD)
                                    for hd in hds], axis=0)

        def gate_cols(lane0):
            return jnp.concatenate([gb[:, lane0 + hd:lane0 + hd + 1].reshape(cb, CHUNK, 1) for hd in hds], axis=0)

        q, k, v = heads(q_ref), heads(k_ref), heads(v_ref)
        beta = gate_cols(BETA_LANE + d * DN_HEADS)
        gcol = gate_cols(DECAY_LANE + d * DN_HEADS)
        gl = DECAY_LANE + d * DN_HEADS
        grow = jnp.concatenate([gr[:, gl + hd:gl + hd + 1, :] for hd in hds], axis=0)
        glast = grow[:, :, CHUNK - 1:CHUNK] if d == 0 else grow[:, :, 0:1]
        decay = jnp.where(masks[d][0], jnp.exp(jnp.minimum(gcol - grow, 0.0)), 0.0)
        data.append(dict(d=d, q=q, k=k, v=v, beta=beta, gcol=gcol, glast=glast, decay=decay, kb=k * beta))
    n = range(len(data))
    lower = [jnp.where(masks[x['d']][1], _bmm_nt(x['kb'], x['k']) * x['decay'], 0.0) for x in data]
    wmask = {d: _dn_masks(d, DN_PACK) for d in masks}
    unit = lax.broadcasted_iota(jnp.int32, (CHUNK, DN_PACK * CHUNK), 1) // CHUNK

    def wide(x):
        x4 = x.reshape(x.shape[0] // DN_PACK, DN_PACK, CHUNK, CHUNK)
        return jnp.concatenate([x4[:, j] for j in range(DN_PACK)], axis=-1)

    def block_diag(xw):
        return jnp.concatenate([jnp.where(unit == j, xw, 0.0) for j in range(DN_PACK)], axis=1)

    lower_w = [wide(lower[i]) for i in n]
    tinv = [wmask[data[i]['d']][2] - jnp.where(wmask[data[i]['d']][3][0], lower_w[i], 0.0) for i in n]
    for lvl in range(1, len(wmask[data[0]['d']][3])):
        te = [_bmm(tinv[i], block_diag(jnp.where(wmask[data[i]['d']][3][lvl], lower_w[i], 0.0))) for i in n]
        tinv = [tinv[i] - _bmm(te[i], block_diag(tinv[i])) for i in n]
    sol = []
    for i, x in enumerate(data):
        rhs = jnp.concatenate([x['v'] * x['beta'], x['kb'] * jnp.exp(x['gcol'])], axis=-1)
        units = rhs.shape[0]
        s_tall = _bmm(block_diag(tinv[i]), rhs.reshape(units // DN_PACK, DN_PACK * CHUNK, 2 * DN_HEAD))
        sol.append(s_tall.reshape(units, CHUNK, 2 * DN_HEAD))
    out = []
    for i, x in enumerate(data):
        u, w = sol[i][..., :DN_HEAD], sol[i][..., DN_HEAD:]
        attn = (_bmm_nt(x['q'], x['k']) * x['decay']).astype(BF16)
        wq = jnp.concatenate([w, x['q'] * jnp.exp(x['gcol'])], axis=1).astype(BF16)
        kdec = (x['k'] * jnp.exp(x['glast'] - x['gcol'])).astype(BF16)
        out.append((u, wq, attn, kdec, jnp.exp(x['glast'])))
    return out


def _dn_step(d, hds, c, pre, o_ref, s_ref, cb):
    u, wq, attn, kdec, eg = pre
    for i, hd in enumerate(hds):
        n = i * cb + c
        idx = d * DN_HEADS + hd
        st = s_ref[idx]
        ws = _mm(wq[n], st)
        v_new = (u[n] - ws[:CHUNK]).astype(BF16)
        o = ws[CHUNK:] + _mm(attn[n], v_new)
        s_ref[idx] = st * eg[n] + lax.dot_general(kdec[n], v_new, (((0,), (0,)), ((), ())),
                                                  preferred_element_type=F32)
        o_ref[c * CHUNK:(c + 1) * CHUNK, hd * DN_HEAD:(hd + 1) * DN_HEAD] = o.astype(o_ref.dtype)


def _dn_main_kernel(qf, kf, vf, gbf, grf, qb, kb, vb, gbb, grb, of_ref, ob_ref, s_ref, *, cb):
    @pl.when(pl.program_id(1) == 0)
    def _():
        s_ref[...] = jnp.zeros_like(s_ref)

    groups = [tuple(range(h, h + DN_STREAM_HEADS)) for h in range(0, DN_HEADS, DN_STREAM_HEADS)]
    streams = ([(0, hds, (qf, kf, vf, gbf, grf)) for hds in groups]
               + [(1, hds, (qb, kb, vb, gbb, grb)) for hds in groups])
    pre = _dn_chunk_terms(streams, cb)
    o_refs = (of_ref, ob_ref)
    for i in range(cb):
        for (d, hds, _), p in zip(streams, pre):
            _dn_step(d, hds, i if d == 0 else cb - 1 - i, p, o_refs[d], s_ref, cb)


def _dn_main(qn, kn, vv, gb, gr, cb):
    b, t, _ = qn.shape
    cbt = cb * CHUNK
    ng = t // cbt
    fwd = lambda bi, j: (bi, j, 0)
    bwd = lambda bi, j: (bi, ng - 1 - j, 0)
    fwd4 = lambda bi, j: (bi, j, 0, 0)
    bwd4 = lambda bi, j: (bi, ng - 1 - j, 0, 0)

    def specs(im3, im4):
        return ([pl.BlockSpec((None, cbt, DN_QK), im3)] * 3
                + [pl.BlockSpec((None, cbt, LANES), im3), pl.BlockSpec((None, cb, 2 * SUBLANES, CHUNK), im4)])

    kern = functools.partial(_dn_main_kernel, cb=cb)
    return pl.pallas_call(
        kern,
        grid=(b, ng),
        in_specs=specs(fwd, fwd4) + specs(bwd, bwd4),
        out_specs=[pl.BlockSpec((None, cbt, DN_QK), fwd), pl.BlockSpec((None, cbt, DN_QK), bwd)],
        out_shape=[jax.ShapeDtypeStruct((b, t, DN_QK), BF16)] * 2,
        scratch_shapes=[pltpu.VMEM((N_DIR * DN_HEADS, DN_HEAD, DN_HEAD), F32)],
        compiler_params=_params(("parallel", "arbitrary")),
        name="dn_main",
    )(qn, kn, vv, gb, gr, qn, kn, vv, gb, gr)


def _rope_norm(x, w, seg, cos, sin):
    sq = x * x
    sq_hi = sq.astype(BF16)
    sq_lo = (sq - sq_hi.astype(F32)).astype(BF16)
    ss = jnp.dot(sq_hi, seg, preferred_element_type=F32) + jnp.dot(sq_lo, seg, preferred_element_type=F32)
    xn = x * lax.rsqrt(ss * (1.0 / ATT_HEAD_DIM) + EPS) * w
    quarter = ATT_HEAD_DIM // 4
    outs = []
    for t in range(x.shape[-1] // LANES):
        xt = xn[:, t * LANES:(t + 1) * LANES]
        lane = lax.broadcasted_iota(jnp.int32, xt.shape, 1)
        rot = jnp.where(lane % (2 * quarter) < quarter,
                        pltpu.roll(xt, LANES - quarter, axis=1), pltpu.roll(xt, quarter, axis=1))
        outs.append(xt * cos + rot * sin)
    return outs[0] if len(outs) == 1 else jnp.concatenate(outs, axis=-1)


def _attn_prep_kernel(q_ref, k_ref, v_ref, qw_ref, kw_ref, seg_ref, cos_ref, sin_ref, qo_ref, ko_ref, vo_ref):
    cos = cos_ref[...]
    sin = sin_ref[...]
    q = _rope_norm(q_ref[...].astype(F32), qw_ref[...], seg_ref[...], cos, sin)
    qt = (q * (LOG2E * ATT_HEAD_DIM ** -0.5)).T.astype(BF16)
    for kvh in range(ATT_KV_HEADS):
        rows = [qt[(kvh * ATT_GROUP + g) * ATT_HEAD_DIM:(kvh * ATT_GROUP + g + 1) * ATT_HEAD_DIM] for g in range(ATT_GROUP)]
        qo_ref[kvh] = jnp.concatenate(rows, axis=1)
    k = _rope_norm(k_ref[...].astype(F32), kw_ref[...], seg_ref[0:LANES, 0:LANES], cos, sin).astype(BF16)
    vt = v_ref[...].astype(F32).T.astype(BF16)
    for kvh in range(ATT_KV_HEADS):
        ko_ref[kvh] = k[:, kvh * ATT_HEAD_DIM:(kvh + 1) * ATT_HEAD_DIM]
        vo_ref[kvh] = vt[kvh * ATT_HEAD_DIM:(kvh + 1) * ATT_HEAD_DIM]


def _attn_prep(atq, atk, atv, qw_row, kw_row, seg, cos_t, sin_t, tq):
    b, t, _ = atq.shape
    nq = t // tq
    row = lambda bi, i: (bi, i, 0)
    const = lambda bi, i: (0, 0)
    return pl.pallas_call(
        _attn_prep_kernel,
        grid=(b, nq),
        in_specs=[pl.BlockSpec((None, tq, ATT_Q), row), pl.BlockSpec((None, tq, ATT_KV), row),
                  pl.BlockSpec((None, tq, ATT_KV), row),
                  pl.BlockSpec((1, ATT_Q), const), pl.BlockSpec((1, ATT_KV), const),
                  pl.BlockSpec((ATT_Q, ATT_Q), const),
                  pl.BlockSpec((tq, LANES), lambda bi, i: (i, 0)), pl.BlockSpec((tq, LANES), lambda bi, i: (i, 0))],
        out_specs=[pl.BlockSpec((None, ATT_KV_HEADS, None, ATT_HEAD_DIM, ATT_GROUP * tq), lambda bi, i: (bi, 0, i, 0, 0)),
                   pl.BlockSpec((None, ATT_KV_HEADS, tq, ATT_HEAD_DIM), lambda bi, i: (bi, 0, i, 0)),
                   pl.BlockSpec((None, ATT_KV_HEADS, ATT_HEAD_DIM, tq), lambda bi, i: (bi, 0, 0, i))],
        out_shape=[jax.ShapeDtypeStruct((b, ATT_KV_HEADS, nq, ATT_HEAD_DIM, ATT_GROUP * tq), BF16),
                   jax.ShapeDtypeStruct((b, ATT_KV_HEADS, t, ATT_HEAD_DIM), BF16),
                   jax.ShapeDtypeStruct((b, ATT_KV_HEADS, ATT_HEAD_DIM, t), BF16)],
        compiler_params=_params(("parallel", "parallel")),
        name="attn_prep",
    )(atq, atk, atv, qw_row, kw_row, seg, cos_t, sin_t)


def _attn_kernel(qt_ref, k_ref, vt_ref, o_ref, s_buf, p_buf, *, tk):
    m_cols = qt_ref.shape[1]
    qt = qt_ref[...]
    nk = k_ref.shape[0] // tk
    ones = jnp.ones((2 * SUBLANES, tk), BF16)

    def scores(j, slot):
        start = pl.multiple_of(j * tk, tk)
        s = jnp.dot(k_ref[pl.ds(start, tk), :], qt, preferred_element_type=F32)
        s_buf[slot] = s
        return jnp.max(jnp.max(s.reshape(tk // SUBLANES, SUBLANES, m_cols), axis=0), axis=0, keepdims=True)

    def softmax(slot, m, chunk_max):
        m_new = jnp.maximum(m, chunk_max)
        p_buf[slot] = jnp.exp2(s_buf[slot] - m_new).astype(BF16)
        return m_new, jnp.exp2(m - m_new)

    def values(j, slot, alpha, acc):
        start = pl.multiple_of(j * tk, tk)
        vt = jnp.concatenate([vt_ref[:, pl.ds(start, tk)], ones], axis=0)
        return alpha * acc + jnp.dot(vt, p_buf[slot], preferred_element_type=F32)

    cmax0 = scores(0, 0)
    p_buf[1] = jnp.zeros((tk, m_cols), BF16)

    def body(i, carry):
        m, cmax, alpha_prev, acc = carry
        j0 = 2 * i
        cmax1 = scores(j0 + 1, 1)
        m, alpha0 = softmax(0, m, cmax)
        acc = values(jnp.maximum(j0 - 1, 0), 1, alpha_prev, acc)
        cmax2 = scores(jnp.minimum(j0 + 2, nk - 1), 0)
        m, alpha1 = softmax(1, m, cmax1)
        acc = values(j0, 0, alpha0, acc)
        return m, cmax2, alpha1, acc

    init = (jnp.full((1, m_cols), jnp.finfo(F32).min, F32), cmax0, jnp.ones((1, m_cols), F32),
            jnp.zeros((ATT_HEAD_DIM + 2 * SUBLANES, m_cols), F32))
    _, _, alpha, acc = lax.fori_loop(0, nk // 2, body, init)
    acc = values(nk - 1, 1, alpha, acc)
    o = acc[:ATT_HEAD_DIM] / acc[ATT_HEAD_DIM:ATT_HEAD_DIM + 1]
    tq = m_cols // ATT_GROUP
    o = jnp.concatenate([o[:, g * tq:(g + 1) * tq] for g in range(ATT_GROUP)], axis=0)
    o_ref[...] = o.T.astype(o_ref.dtype)


def _attention(qt5, k4, vt4, tk):
    b, _, nq, _, m_cols = qt5.shape
    t = k4.shape[2]
    tq = m_cols // ATT_GROUP
    assert t % (2 * tk) == 0
    kern = functools.partial(_attn_kernel, tk=tk)
    return pl.pallas_call(
        kern,
        grid=(b, ATT_KV_HEADS, nq),
        in_specs=[pl.BlockSpec((None, None, None, ATT_HEAD_DIM, m_cols), lambda bi, h, i: (bi, h, i, 0, 0)),
                  pl.BlockSpec((None, None, t, ATT_HEAD_DIM), lambda bi, h, i: (bi, h, 0, 0)),
                  pl.BlockSpec((None, None, ATT_HEAD_DIM, t), lambda bi, h, i: (bi, h, 0, 0))],
        out_specs=pl.BlockSpec((None, tq, ATT_GROUP * ATT_HEAD_DIM), lambda bi, h, i: (bi, i, h)),
        out_shape=jax.ShapeDtypeStruct((b, t, ATT_Q), BF16),
        scratch_shapes=[pltpu.VMEM((2, tk, m_cols), F32), pltpu.VMEM((2, tk, m_cols), BF16)],
        compiler_params=_params(("parallel", "parallel", "parallel")),
        name="attention",
    )(qt5, k4, vt4)


def _out_proj_kernel(of_ref, ob_ref, dnz_ref, dnw_ref, oat_ref, atz_ref, x_ref, w_ref, fw_ref, y_ref, *, final):
    o = of_ref[...].astype(F32) + ob_ref[...].astype(F32)
    parts = []
    for hd in range(DN_HEADS):
        oh = o[:, hd * DN_HEAD:(hd + 1) * DN_HEAD]
        parts.append(oh * lax.rsqrt(jnp.mean(oh * oh, axis=-1, keepdims=True) + EPS) * dnw_ref[...])
    y_dn = jnp.concatenate(parts, axis=-1) * _silu(dnz_ref[...].astype(F32))
    y_at = oat_ref[...].astype(F32) * _silu(atz_ref[...].astype(F32))
    y = jnp.concatenate([y_dn, y_at], axis=-1).astype(BF16)
    out = x_ref[...] + jnp.dot(y, w_ref[...], preferred_element_type=F32)
    if final:
        out = out * lax.rsqrt(jnp.mean(out * out, axis=-1, keepdims=True) + EPS) * fw_ref[...]
    y_ref[...] = out


def _out_proj(o_f, o_b, dnz, dnw_row, o_at, atz, x2, w_out, fw_row, tm, final):
    bt = x2.shape[0]
    row = lambda i: (i, 0)
    const = lambda i: (0, 0)
    kern = functools.partial(_out_proj_kernel, final=final)
    return pl.pallas_call(
        kern,
        grid=(bt // tm,),
        in_specs=[pl.BlockSpec((tm, DN_QK), row), pl.BlockSpec((tm, DN_QK), row), pl.BlockSpec((tm, DN_QK), row),
                  pl.BlockSpec((1, DN_HEAD), const),
                  pl.BlockSpec((tm, ATT_Q), row), pl.BlockSpec((tm, ATT_Q), row),
                  pl.BlockSpec((tm, D_MODEL), row),
                  pl.BlockSpec((D_MODEL, D_MODEL), const), pl.BlockSpec((1, D_MODEL), const)],
        out_specs=pl.BlockSpec((tm, D_MODEL), row),
        out_shape=jax.ShapeDtypeStruct((bt, D_MODEL), F32),
        compiler_params=_params(("parallel",)),
        name="out_proj_final" if final else "out_proj",
    )(o_f, o_b, dnz, dnw_row, o_at, atz, x2, w_out, fw_row)


def _pack_w_in(w):
    splits = np.cumsum(IN_SIZES)[:-1].tolist()
    qkv, dnz, bb, aa, atq, atk, atv, atz = jnp.split(w, splits, axis=-1)
    pad = jnp.zeros((w.shape[0], LANES - 2 * N_DIR * DN_HEADS), w.dtype)
    return jnp.concatenate([qkv, dnz, atq, atk, atv, atz, bb, aa, pad], axis=-1).astype(BF16)


def _rope_tables(t):
    half = ATT_HEAD_DIM // 2
    pos = np.arange(t)
    row = (pos // GRID_W).astype(np.float32)
    col = (pos % GRID_W).astype(np.float32)
    inv = jnp.asarray(ROPE_THETA, F32) ** (-jnp.arange(0, half, 2, dtype=F32) / half)
    ang_r = jnp.asarray(row)[:, None] * inv[None, :]
    ang_c = jnp.asarray(col)[:, None] * inv[None, :]
    cos = jnp.concatenate([jnp.cos(ang_r)] * 2 + [jnp.cos(ang_c)] * 2, axis=-1)
    sin = jnp.concatenate([-jnp.sin(ang_r), jnp.sin(ang_r), -jnp.sin(ang_c), jnp.sin(ang_c)], axis=-1)
    reps = LANES // ATT_HEAD_DIM
    return jnp.tile(cos, (1, reps)), jnp.tile(sin, (1, reps))


def _gate_row(p):
    return jnp.zeros((1, LANES), F32).at[0, DECAY_LANE:DECAY_LANE + N_DIR * DN_HEADS].set(p.reshape(-1))


def kernel(x, norm_w, w_in, conv_w, a_log, dt_bias, dn_norm_w, q_norm_w, k_norm_w, w_out, final_norm_w):
    b, t, d = x.shape
    depth = w_in.shape[0]
    tm = 256
    cb = min(4, t // CHUNK)
    tq = min(1024, t)
    tk = min(512, t // 2)
    cos_t, sin_t = _rope_tables(t)
    lane = np.arange(ATT_Q)
    seg = jnp.asarray((lane[:, None] // ATT_HEAD_DIM) == (lane[None, :] // ATT_HEAD_DIM), BF16)
    x2 = x.reshape(b * t, d)
    for l in range(depth):
        qkv, dnz, atq, atk, atv, atz, ba = _in_proj(x2, norm_w[l][None, :], _pack_w_in(w_in[l]), tm)
        cw = jnp.zeros((SUBLANES, DN_CONV_DIM), F32).at[:CONV_K].set(conv_w[l])
        qn, kn, vv, gb = _dn_prep(qkv.reshape(b, t, -1), ba.reshape(b, t, -1), cw,
                                  _gate_row(a_log[l]), _gate_row(dt_bias[l]), tm)
        gr = gb[..., :2 * SUBLANES].reshape(b, t // CHUNK, CHUNK, 2 * SUBLANES).transpose(0, 1, 3, 2)
        o_f, o_b = _dn_main(qn, kn, vv, gb, gr, cb)
        qt5, k4, vt4 = _attn_prep(atq.reshape(b, t, -1), atk.reshape(b, t, -1), atv.reshape(b, t, -1),
                                  jnp.tile(q_norm_w[l], ATT_HEADS)[None, :], jnp.tile(k_norm_w[l], ATT_KV_HEADS)[None, :],
                                  seg, cos_t, sin_t, tq)
        o_at = _attention(qt5, k4, vt4, tk).reshape(b * t, ATT_Q)
        x2 = _out_proj(o_f.reshape(b * t, -1), o_b.reshape(b * t, -1), dnz, dn_norm_w[l][None, :], o_at, atz,
                       x2, w_out[l].astype(BF16), final_norm_w[None, :], tm, final=(l == depth - 1))
    return x2.reshape(b, t, d)
```

```python
import functools

import jax
import jax.numpy as jnp
import numpy as np
from jax import lax
from jax.experimental import pallas as pl
from jax.experimental.pallas import tpu as pltpu

D_MODEL = 1024
DN_HEADS = 4
DN_HEAD = 128
DN_QK = DN_HEADS * DN_HEAD
DN_CONV_DIM = 3 * DN_QK
CONV_K = 5
CHUNK = 128
N_DIR = 2
ATT_HEADS = 8
ATT_KV_HEADS = 2
ATT_GROUP = ATT_HEADS // ATT_KV_HEADS
ATT_HEAD_DIM = 64
ATT_Q = ATT_HEADS * ATT_HEAD_DIM
ATT_KV = ATT_KV_HEADS * ATT_HEAD_DIM
ROPE_THETA = 10000.0
GRID_W = 64
EPS = 1e-6
IN_SIZES = (DN_CONV_DIM, DN_QK, N_DIR * DN_HEADS, N_DIR * DN_HEADS, ATT_Q, ATT_KV, ATT_KV, ATT_Q)
LANES = 128
SUBLANES = 8
VMEM_LIMIT = 48 * 1024 * 1024
PACK_COLS = DN_CONV_DIM + DN_QK + ATT_Q + ATT_KV + ATT_KV + ATT_Q + LANES
BETA_LANE = 0
DECAY_LANE = N_DIR * DN_HEADS
DN_STREAM_HEADS = 4
MXU_WIDTH = 256
DN_PACK = MXU_WIDTH // CHUNK

F32 = jnp.float32
BF16 = jnp.bfloat16
HI = lax.Precision.HIGHEST
LOG2E = 1.4426950408889634


def _params(sem):
    return pltpu.CompilerParams(dimension_semantics=sem, vmem_limit_bytes=VMEM_LIMIT)


def _silu(x):
    return x / (1.0 + jnp.exp(-x))


def _in_proj_kernel(x_ref, nw_ref, w_ref, qkv_ref, dnz_ref, atq_ref, atk_ref, atv_ref, atz_ref, ba_ref):
    x = x_ref[...]
    h = x * lax.rsqrt(jnp.mean(x * x, axis=-1, keepdims=True) + EPS) * nw_ref[...]
    p = jnp.dot(h.astype(BF16), w_ref[...], preferred_element_type=F32)
    o = 0
    for ref in (qkv_ref, dnz_ref, atq_ref, atk_ref, atv_ref, atz_ref, ba_ref):
        n = ref.shape[-1]
        ref[...] = p[:, o:o + n].astype(ref.dtype)
        o += n


def _in_proj(x2, nw, w_packed, tm):
    bt = x2.shape[0]
    widths = (DN_CONV_DIM, DN_QK, ATT_Q, ATT_KV, ATT_KV, ATT_Q, LANES)
    dtypes = (F32, BF16, BF16, BF16, BF16, BF16, F32)
    return pl.pallas_call(
        _in_proj_kernel,
        grid=(bt // tm,),
        in_specs=[pl.BlockSpec((tm, D_MODEL), lambda i: (i, 0)),
                  pl.BlockSpec((1, D_MODEL), lambda i: (0, 0)),
                  pl.BlockSpec((D_MODEL, PACK_COLS), lambda i: (0, 0))],
        out_specs=[pl.BlockSpec((tm, n), lambda i: (i, 0)) for n in widths],
        out_shape=[jax.ShapeDtypeStruct((bt, n), dt) for n, dt in zip(widths, dtypes)],
        compiler_params=_params(("parallel",)),
        name="in_proj",
    )(x2, nw, w_packed)


def _dn_prep_kernel(x_ref, prev_ref, next_ref, ba_ref, cw_ref, alog_ref, dtb_ref,
                    q_ref, k_ref, v_ref, gb_ref, xe_ref, *, tm):
    i = pl.program_id(1)
    n = pl.num_programs(1)
    halo = SUBLANES
    xe_ref[0:halo, :] = jnp.where(i > 0, prev_ref[...], 0.0)
    xe_ref[halo:halo + tm, :] = x_ref[...]
    xe_ref[halo + tm:2 * halo + tm, :] = jnp.where(i < n - 1, next_ref[...], 0.0)
    pad = CONV_K // 2
    y = jnp.zeros((tm, DN_CONV_DIM), F32)
    for j in range(CONV_K):
        y = y + xe_ref[halo + j - pad:halo + j - pad + tm, :] * cw_ref[j:j + 1, :]
    y = _silu(y)
    for hd in range(DN_HEADS):
        lo = hd * DN_HEAD
        q = y[:, lo:lo + DN_HEAD]
        k = y[:, DN_QK + lo:DN_QK + lo + DN_HEAD]
        q_ref[:, lo:lo + DN_HEAD] = q * lax.rsqrt(jnp.sum(q * q, axis=-1, keepdims=True) + EPS) * (DN_HEAD ** -0.5)
        k_ref[:, lo:lo + DN_HEAD] = k * lax.rsqrt(jnp.sum(k * k, axis=-1, keepdims=True) + EPS)
    v_ref[...] = y[:, 2 * DN_QK:]

    ba = ba_ref[...]
    beta = 1.0 / (1.0 + jnp.exp(-ba))
    z = ba + dtb_ref[...]
    softplus = jnp.maximum(z, 0.0) + jnp.log(1.0 + jnp.exp(-jnp.abs(z)))
    g = -jnp.exp(alog_ref[...]) * softplus
    r = lax.broadcasted_iota(jnp.int32, (tm, tm), 0)
    c = lax.broadcasted_iota(jnp.int32, (tm, tm), 1)
    same = (r // CHUNK) == (c // CHUNK)
    tri_f = jnp.where(same & (c <= r), 1.0, 0.0).astype(F32)
    tri_b = jnp.where(same & (c >= r), 1.0, 0.0).astype(F32)
    gc_f = jnp.dot(tri_f, g, precision=HI, preferred_element_type=F32)
    gc_b = jnp.dot(tri_b, g, precision=HI, preferred_element_type=F32)
    lane = lax.broadcasted_iota(jnp.int32, (tm, LANES), 1)
    gc = jnp.where(lane < DECAY_LANE + DN_HEADS, gc_f, gc_b)
    gb_ref[...] = jnp.where(lane < DECAY_LANE, beta, gc)


def _dn_prep(qkv, ba, cw, alog_row, dtb_row, tm):
    b, t, _ = qkv.shape
    nb = tm // SUBLANES
    last = t // SUBLANES - 1
    kern = functools.partial(_dn_prep_kernel, tm=tm)
    return pl.pallas_call(
        kern,
        grid=(b, t // tm),
        in_specs=[pl.BlockSpec((None, tm, DN_CONV_DIM), lambda bi, i: (bi, i, 0)),
                  pl.BlockSpec((None, SUBLANES, DN_CONV_DIM), lambda bi, i: (bi, jnp.maximum(i * nb - 1, 0), 0)),
                  pl.BlockSpec((None, SUBLANES, DN_CONV_DIM), lambda bi, i: (bi, jnp.minimum((i + 1) * nb, last), 0)),
                  pl.BlockSpec((None, tm, LANES), lambda bi, i: (bi, i, 0)),
                  pl.BlockSpec((SUBLANES, DN_CONV_DIM), lambda bi, i: (0, 0)),
                  pl.BlockSpec((1, LANES), lambda bi, i: (0, 0)),
                  pl.BlockSpec((1, LANES), lambda bi, i: (0, 0))],
        out_specs=[pl.BlockSpec((None, tm, DN_QK), lambda bi, i: (bi, i, 0))] * 3
        + [pl.BlockSpec((None, tm, LANES), lambda bi, i: (bi, i, 0))],
        out_shape=[jax.ShapeDtypeStruct((b, t, DN_QK), F32)] * 3 + [jax.ShapeDtypeStruct((b, t, LANES), F32)],
        scratch_shapes=[pltpu.VMEM((tm + 2 * SUBLANES, DN_CONV_DIM), F32)],
        compiler_params=_params(("parallel", "parallel")),
        name="dn_prep",
    )(qkv, qkv, qkv, ba, cw, alog_row, dtb_row)


def _mm(a, b):
    return jnp.dot(a.astype(BF16), b.astype(BF16), preferred_element_type=F32)


def _bmm(a, b):
    return jnp.einsum('cij,cjk->cik', a.astype(BF16), b.astype(BF16), preferred_element_type=F32)


def _bmm_nt(a, b):
    return jnp.einsum('cid,cjd->cij', a.astype(BF16), b.astype(BF16), preferred_element_type=F32)


def _dn_masks(d, pack=1):
    ii = lax.broadcasted_iota(jnp.int32, (CHUNK, pack * CHUNK), 0)
    jj = lax.broadcasted_iota(jnp.int32, (CHUNK, pack * CHUNK), 1) % CHUNK
    if d == 1:
        ii, jj = jj, ii
    eye = jnp.where(ii == jj, 1.0, 0.0).astype(F32)
    merge = []
    s = 1
    while s < CHUNK:
        merge.append((ii // (2 * s) == jj // (2 * s)) & ((ii // s) % 2 == 1) & ((jj // s) % 2 == 0))
        s *= 2
    return ii >= jj, ii > jj, eye, merge


def _dn_chunk_terms(streams, cb, out):
    masks = {d: _dn_masks(d) for d in sorted({s[0] for s in streams})}
    data = []
    for d, hds, (q_ref, k_ref, v_ref, gb_ref, gr_ref) in streams:
        gb = gb_ref[...]
        gr = gr_ref[...]

        def heads(ref):
            return jnp.concatenate([ref[:, hd * DN_HEAD:(hd + 1) * DN_HEAD].reshape(cb, CHUNK, DN_HEAD)
                                    for hd in hds], axis=0)

        def gate_cols(lane0):
            return jnp.concatenate([gb[:, lane0 + hd:lane0 + hd + 1].reshape(cb, CHUNK, 1) for hd in hds], axis=0)

        q, k, v = heads(q_ref), heads(k_ref), heads(v_ref)
        beta = gate_cols(BETA_LANE + d * DN_HEADS)
        gcol = gate_cols(DECAY_LANE + d * DN_HEADS)
        gl = DECAY_LANE + d * DN_HEADS
        grow = jnp.concatenate([gr[:, gl + hd:gl + hd + 1, :] for hd in hds], axis=0)
        glast = grow[:, :, CHUNK - 1:CHUNK] if d == 0 else grow[:, :, 0:1]
        decay = jnp.where(masks[d][0], jnp.exp(jnp.minimum(gcol - grow, 0.0)), 0.0)
        data.append(dict(d=d, q=q, k=k, v=v, beta=beta, gcol=gcol, glast=glast, decay=decay, kb=k * beta))
    n = range(len(data))
    lower = [jnp.where(masks[x['d']][1], _bmm_nt(x['kb'], x['k']) * x['decay'], 0.0) for x in data]
    wmask = {d: _dn_masks(d, DN_PACK) for d in masks}
    unit = lax.broadcasted_iota(jnp.int32, (CHUNK, DN_PACK * CHUNK), 1) // CHUNK

    def wide(x):
        x4 = x.reshape(x.shape[0] // DN_PACK, DN_PACK, CHUNK, CHUNK)
        return jnp.concatenate([x4[:, j] for j in range(DN_PACK)], axis=-1)

    def block_diag(xw):
        return jnp.concatenate([jnp.where(unit == j, xw, 0.0) for j in range(DN_PACK)], axis=1)

    yield
    lower_w = [wide(lower[i]) for i in n]
    tinv = [wmask[data[i]['d']][2] - jnp.where(wmask[data[i]['d']][3][0], lower_w[i], 0.0) for i in n]
    for lvl in range(1, len(wmask[data[0]['d']][3])):
        te = [_bmm(tinv[i], block_diag(jnp.where(wmask[data[i]['d']][3][lvl], lower_w[i], 0.0))) for i in n]
        yield
        tinv = [tinv[i] - _bmm(te[i], block_diag(tinv[i])) for i in n]
        yield
    sol = []
    for i, x in enumerate(data):
        rhs = jnp.concatenate([x['v'] * x['beta'], x['kb'] * jnp.exp(x['gcol'])], axis=-1)
        units = rhs.shape[0]
        s_tall = _bmm(block_diag(tinv[i]), rhs.reshape(units // DN_PACK, DN_PACK * CHUNK, 2 * DN_HEAD))
        sol.append(s_tall.reshape(units, CHUNK, 2 * DN_HEAD))
    yield
    for i, x in enumerate(data):
        u, w = sol[i][..., :DN_HEAD], sol[i][..., DN_HEAD:]
        attn = (_bmm_nt(x['q'], x['k']) * x['decay']).astype(BF16)
        wq = jnp.concatenate([w, x['q'] * jnp.exp(x['gcol'])], axis=1).astype(BF16)
        kdec = (x['k'] * jnp.exp(x['glast'] - x['gcol'])).astype(BF16)
        out.append((u, wq, attn, kdec, jnp.exp(x['glast'])))


def _dn_scan(streams, slot, bufs, o_refs, s_ref, cb):
    u_buf, wq_buf, attn_buf, kdec_buf, eg_buf = bufs
    units = DN_STREAM_HEADS * cb
    for step in range(cb):
        chains = []
        for si, (d, hds, _) in enumerate(streams):
            c = step if d == 0 else cb - 1 - step
            chains += [(d, hd, c, si * units + i * cb + c) for i, hd in enumerate(hds)]
        states = [s_ref[d * DN_HEADS + hd] for d, hd, _, _ in chains]
        ws = [_mm(wq_buf[slot, n], st) for (_, _, _, n), st in zip(chains, states)]
        yield
        for (d, hd, c, n), st, w in zip(chains, states, ws):
            v_new = (u_buf[slot, n] - w[:CHUNK]).astype(BF16)
            o = w[CHUNK:] + _mm(attn_buf[slot, n], v_new)
            s_ref[d * DN_HEADS + hd] = st * eg_buf[slot, n] + lax.dot_general(
                kdec_buf[slot, n], v_new, (((0,), (0,)), ((), ())), preferred_element_type=F32)
            o_refs[d][c * CHUNK:(c + 1) * CHUNK, hd * DN_HEAD:(hd + 1) * DN_HEAD] = o.astype(o_refs[d].dtype)
        yield


def _dn_main_kernel(qf, kf, vf, gbf, grf, qb, kb, vb, gbb, grb, of_ref, ob_ref, s_ref,
                    u_buf, wq_buf, attn_buf, kdec_buf, eg_buf, *, cb):
    j = pl.program_id(1)
    bufs = (u_buf, wq_buf, attn_buf, kdec_buf, eg_buf)

    @pl.when(j == 0)
    def _():
        s_ref[...] = jnp.zeros_like(s_ref)
        for buf in bufs:
            buf[1] = jnp.zeros(buf.shape[1:], buf.dtype)

    cur = (j + 1) % 2
    nxt = j % 2
    groups = [tuple(range(h, h + DN_STREAM_HEADS)) for h in range(0, DN_HEADS, DN_STREAM_HEADS)]
    streams = ([(0, hds, (qf, kf, vf, gbf, grf)) for hds in groups]
               + [(1, hds, (qb, kb, vb, gbb, grb)) for hds in groups])
    units = DN_STREAM_HEADS * cb
    pre = []
    terms = _dn_chunk_terms(streams, cb, pre)
    scan = _dn_scan(streams, cur, bufs, (of_ref, ob_ref), s_ref, cb)
    live = [terms, scan]
    while live:
        for gen, reps in ((terms, 2), (scan, 1)):
            for _ in range(reps):
                if gen in live and next(gen, StopIteration) is StopIteration:
                    live.remove(gen)
    for si, (u, wq, attn, kdec, eg) in enumerate(pre):
        lo, hi = si * units, (si + 1) * units
        u_buf[nxt, lo:hi] = u
        wq_buf[nxt, lo:hi] = wq
        attn_buf[nxt, lo:hi] = attn
        kdec_buf[nxt, lo:hi] = kdec
        eg_buf[nxt, lo:hi] = jnp.broadcast_to(eg, (units, 1, DN_HEAD))


def _dn_main(qn, kn, vv, gb, gr, cb):
    b, t, _ = qn.shape
    cbt = cb * CHUNK
    ng = t // cbt
    fwd_in = lambda bi, j: (bi, jnp.minimum(j, ng - 1), 0)
    bwd_in = lambda bi, j: (bi, ng - 1 - jnp.minimum(j, ng - 1), 0)
    fwd_in4 = lambda bi, j: (bi, jnp.minimum(j, ng - 1), 0, 0)
    bwd_in4 = lambda bi, j: (bi, ng - 1 - jnp.minimum(j, ng - 1), 0, 0)
    fwd_out = lambda bi, j: (bi, jnp.maximum(j - 1, 0), 0)
    bwd_out = lambda bi, j: (bi, ng - 1 - jnp.maximum(j - 1, 0), 0)

    def specs(im3, im4):
        return ([pl.BlockSpec((None, cbt, DN_QK), im3)] * 3
                + [pl.BlockSpec((None, cbt, LANES), im3), pl.BlockSpec((None, cb, 2 * SUBLANES, CHUNK), im4)])

    units = N_DIR * DN_HEADS * cb
    kern = functools.partial(_dn_main_kernel, cb=cb)
    return pl.pallas_call(
        kern,
        grid=(b, ng + 1),
        in_specs=specs(fwd_in, fwd_in4) + specs(bwd_in, bwd_in4),
        out_specs=[pl.BlockSpec((None, cbt, DN_QK), fwd_out), pl.BlockSpec((None, cbt, DN_QK), bwd_out)],
        out_shape=[jax.ShapeDtypeStruct((b, t, DN_QK), BF16)] * 2,
        scratch_shapes=[pltpu.VMEM((N_DIR * DN_HEADS, DN_HEAD, DN_HEAD), F32),
                        pltpu.VMEM((2, units, CHUNK, DN_HEAD), F32),
                        pltpu.VMEM((2, units, 2 * CHUNK, DN_HEAD), BF16),
                        pltpu.VMEM((2, units, CHUNK, CHUNK), BF16),
                        pltpu.VMEM((2, units, CHUNK, DN_HEAD), BF16),
                        pltpu.VMEM((2, units, 1, DN_HEAD), F32)],
        compiler_params=_params(("parallel", "arbitrary")),
        name="dn_main",
    )(qn, kn, vv, gb, gr, qn, kn, vv, gb, gr)


def _rope_norm(x, w, seg, cos, sin):
    sq = x * x
    sq_hi = sq.astype(BF16)
    sq_lo = (sq - sq_hi.astype(F32)).astype(BF16)
    ss = jnp.dot(sq_hi, seg, preferred_element_type=F32) + jnp.dot(sq_lo, seg, preferred_element_type=F32)
    xn = x * lax.rsqrt(ss * (1.0 / ATT_HEAD_DIM) + EPS) * w
    quarter = ATT_HEAD_DIM // 4
    outs = []
    for t in range(x.shape[-1] // LANES):
        xt = xn[:, t * LANES:(t + 1) * LANES]
        lane = lax.broadcasted_iota(jnp.int32, xt.shape, 1)
        rot = jnp.where(lane % (2 * quarter) < quarter,
                        pltpu.roll(xt, LANES - quarter, axis=1), pltpu.roll(xt, quarter, axis=1))
        outs.append(xt * cos + rot * sin)
    return outs[0] if len(outs) == 1 else jnp.concatenate(outs, axis=-1)


def _attn_prep_kernel(q_ref, k_ref, v_ref, qw_ref, kw_ref, seg_ref, cos_ref, sin_ref, qo_ref, ko_ref, vo_ref):
    cos = cos_ref[...]
    sin = sin_ref[...]
    q = _rope_norm(q_ref[...].astype(F32), qw_ref[...], seg_ref[...], cos, sin)
    qt = (q * (LOG2E * ATT_HEAD_DIM ** -0.5)).T.astype(BF16)
    for kvh in range(ATT_KV_HEADS):
        rows = [qt[(kvh * ATT_GROUP + g) * ATT_HEAD_DIM:(kvh * ATT_GROUP + g + 1) * ATT_HEAD_DIM] for g in range(ATT_GROUP)]
        qo_ref[kvh] = jnp.concatenate(rows, axis=1)
    k = _rope_norm(k_ref[...].astype(F32), kw_ref[...], seg_ref[0:LANES, 0:LANES], cos, sin).astype(BF16)
    vt = v_ref[...].astype(F32).T.astype(BF16)
    for kvh in range(ATT_KV_HEADS):
        ko_ref[kvh] = k[:, kvh * ATT_HEAD_DIM:(kvh + 1) * ATT_HEAD_DIM]
        vo_ref[kvh] = vt[kvh * ATT_HEAD_DIM:(kvh + 1) * ATT_HEAD_DIM]


def _attn_prep(atq, atk, atv, qw_row, kw_row, seg, cos_t, sin_t, tq):
    b, t, _ = atq.shape
    nq = t // tq
    row = lambda bi, i: (bi, i, 0)
    const = lambda bi, i: (0, 0)
    return pl.pallas_call(
        _attn_prep_kernel,
        grid=(b, nq),
        in_specs=[pl.BlockSpec((None, tq, ATT_Q), row), pl.BlockSpec((None, tq, ATT_KV), row),
                  pl.BlockSpec((None, tq, ATT_KV), row),
                  pl.BlockSpec((1, ATT_Q), const), pl.BlockSpec((1, ATT_KV), const),
                  pl.BlockSpec((ATT_Q, ATT_Q), const),
                  pl.BlockSpec((tq, LANES), lambda bi, i: (i, 0)), pl.BlockSpec((tq, LANES), lambda bi, i: (i, 0))],
        out_specs=[pl.BlockSpec((None, ATT_KV_HEADS, None, ATT_HEAD_DIM, ATT_GROUP * tq), lambda bi, i: (bi, 0, i, 0, 0)),
                   pl.BlockSpec((None, ATT_KV_HEADS, tq, ATT_HEAD_DIM), lambda bi, i: (bi, 0, i, 0)),
                   pl.BlockSpec((None, ATT_KV_HEADS, ATT_HEAD_DIM, tq), lambda bi, i: (bi, 0, 0, i))],
        out_shape=[jax.ShapeDtypeStruct((b, ATT_KV_HEADS, nq, ATT_HEAD_DIM, ATT_GROUP * tq), BF16),
                   jax.ShapeDtypeStruct((b, ATT_KV_HEADS, t, ATT_HEAD_DIM), BF16),
                   jax.ShapeDtypeStruct((b, ATT_KV_HEADS, ATT_HEAD_DIM, t), BF16)],
        compiler_params=_params(("parallel", "parallel")),
        name="attn_prep",
    )(atq, atk, atv, qw_row, kw_row, seg, cos_t, sin_t)


def _attn_kernel(qt_ref, k_ref, vt_ref, o_ref, s_buf, p_buf, *, tk):
    m_cols = qt_ref.shape[1]
    qt = qt_ref[...]
    nk = k_ref.shape[0] // tk
    ones = jnp.ones((2 * SUBLANES, tk), BF16)

    def scores(j, slot):
        start = pl.multiple_of(j * tk, tk)
        s = jnp.dot(k_ref[pl.ds(start, tk), :], qt, preferred_element_type=F32)
        s_buf[slot] = s
        return jnp.max(jnp.max(s.reshape(tk // SUBLANES, SUBLANES, m_cols), axis=0), axis=0, keepdims=True)

    def softmax(slot, m, chunk_max):
        m_new = jnp.maximum(m, chunk_max)
        p_buf[slot] = jnp.exp2(s_buf[slot] - m_new).astype(BF16)
        return m_new, jnp.exp2(m - m_new)

    def values(j, slot, alpha, acc):
        start = pl.multiple_of(j * tk, tk)
        vt = jnp.concatenate([vt_ref[:, pl.ds(start, tk)], ones], axis=0)
        return alpha * acc + jnp.dot(vt, p_buf[slot], preferred_element_type=F32)

    cmax0 = scores(0, 0)
    p_buf[1] = jnp.zeros((tk, m_cols), BF16)

    def body(i, carry):
        m, cmax, alpha_prev, acc = carry
        j0 = 2 * i
        cmax1 = scores(j0 + 1, 1)
        m, alpha0 = softmax(0, m, cmax)
        acc = values(jnp.maximum(j0 - 1, 0), 1, alpha_prev, acc)
        cmax2 = scores(jnp.minimum(j0 + 2, nk - 1), 0)
        m, alpha1 = softmax(1, m, cmax1)
        acc = values(j0, 0, alpha0, acc)
        return m, cmax2, alpha1, acc

    init = (jnp.full((1, m_cols), jnp.finfo(F32).min, F32), cmax0, jnp.ones((1, m_cols), F32),
            jnp.zeros((ATT_HEAD_DIM + 2 * SUBLANES, m_cols), F32))
    _, _, alpha, acc = lax.fori_loop(0, nk // 2, body, init)
    acc = values(nk - 1, 1, alpha, acc)
    o = acc[:ATT_HEAD_DIM] / acc[ATT_HEAD_DIM:ATT_HEAD_DIM + 1]
    tq = m_cols // ATT_GROUP
    o = jnp.concatenate([o[:, g * tq:(g + 1) * tq] for g in range(ATT_GROUP)], axis=0)
    o_ref[...] = o.T.astype(o_ref.dtype)


def _attention(qt5, k4, vt4, tk):
    b, _, nq, _, m_cols = qt5.shape
    t = k4.shape[2]
    tq = m_cols // ATT_GROUP
    assert t % (2 * tk) == 0
    kern = functools.partial(_attn_kernel, tk=tk)
    return pl.pallas_call(
        kern,
        grid=(b, ATT_KV_HEADS, nq),
        in_specs=[pl.BlockSpec((None, None, None, ATT_HEAD_DIM, m_cols), lambda bi, h, i: (bi, h, i, 0, 0)),
                  pl.BlockSpec((None, None, t, ATT_HEAD_DIM), lambda bi, h, i: (bi, h, 0, 0)),
                  pl.BlockSpec((None, None, ATT_HEAD_DIM, t), lambda bi, h, i: (bi, h, 0, 0))],
        out_specs=pl.BlockSpec((None, tq, ATT_GROUP * ATT_HEAD_DIM), lambda bi, h, i: (bi, i, h)),
        out_shape=jax.ShapeDtypeStruct((b, t, ATT_Q), BF16),
        scratch_shapes=[pltpu.VMEM((2, tk, m_cols), F32), pltpu.VMEM((2, tk, m_cols), BF16)],
        compiler_params=_params(("parallel", "parallel", "parallel")),
        name="attention",
    )(qt5, k4, vt4)


def _out_proj_kernel(of_ref, ob_ref, dnz_ref, dnw_ref, oat_ref, atz_ref, x_ref, w_ref, fw_ref, y_ref, *, final):
    o = of_ref[...].astype(F32) + ob_ref[...].astype(F32)
    parts = []
    for hd in range(DN_HEADS):
        oh = o[:, hd * DN_HEAD:(hd + 1) * DN_HEAD]
        parts.append(oh * lax.rsqrt(jnp.mean(oh * oh, axis=-1, keepdims=True) + EPS) * dnw_ref[...])
    y_dn = jnp.concatenate(parts, axis=-1) * _silu(dnz_ref[...].astype(F32))
    y_at = oat_ref[...].astype(F32) * _silu(atz_ref[...].astype(F32))
    y = jnp.concatenate([y_dn, y_at], axis=-1).astype(BF16)
    out = x_ref[...] + jnp.dot(y, w_ref[...], preferred_element_type=F32)
    if final:
        out = out * lax.rsqrt(jnp.mean(out * out, axis=-1, keepdims=True) + EPS) * fw_ref[...]
    y_ref[...] = out


def _out_proj(o_f, o_b, dnz, dnw_row, o_at, atz, x2, w_out, fw_row, tm, final):
    bt = x2.shape[0]
    row = lambda i: (i, 0)
    const = lambda i: (0, 0)
    kern = functools.partial(_out_proj_kernel, final=final)
    return pl.pallas_call(
        kern,
        grid=(bt // tm,),
        in_specs=[pl.BlockSpec((tm, DN_QK), row), pl.BlockSpec((tm, DN_QK), row), pl.BlockSpec((tm, DN_QK), row),
                  pl.BlockSpec((1, DN_HEAD), const),
                  pl.BlockSpec((tm, ATT_Q), row), pl.BlockSpec((tm, ATT_Q), row),
                  pl.BlockSpec((tm, D_MODEL), row),
                  pl.BlockSpec((D_MODEL, D_MODEL), const), pl.BlockSpec((1, D_MODEL), const)],
        out_specs=pl.BlockSpec((tm, D_MODEL), row),
        out_shape=jax.ShapeDtypeStruct((bt, D_MODEL), F32),
        compiler_params=_params(("parallel",)),
        name="out_proj_final" if final else "out_proj",
    )(o_f, o_b, dnz, dnw_row, o_at, atz, x2, w_out, fw_row)


def _pack_w_in(w):
    splits = np.cumsum(IN_SIZES)[:-1].tolist()
    qkv, dnz, bb, aa, atq, atk, atv, atz = jnp.split(w, splits, axis=-1)
    pad = jnp.zeros((w.shape[0], LANES - 2 * N_DIR * DN_HEADS), w.dtype)
    return jnp.concatenate([qkv, dnz, atq, atk, atv, atz, bb, aa, pad], axis=-1).astype(BF16)


def _rope_tables(t):
    half = ATT_HEAD_DIM // 2
    pos = np.arange(t)
    row = (pos // GRID_W).astype(np.float32)
    col = (pos % GRID_W).astype(np.float32)
    inv = jnp.asarray(ROPE_THETA, F32) ** (-jnp.arange(0, half, 2, dtype=F32) / half)
    ang_r = jnp.asarray(row)[:, None] * inv[None, :]
    ang_c = jnp.asarray(col)[:, None] * inv[None, :]
    cos = jnp.concatenate([jnp.cos(ang_r)] * 2 + [jnp.cos(ang_c)] * 2, axis=-1)
    sin = jnp.concatenate([-jnp.sin(ang_r), jnp.sin(ang_r), -jnp.sin(ang_c), jnp.sin(ang_c)], axis=-1)
    reps = LANES // ATT_HEAD_DIM
    return jnp.tile(cos, (1, reps)), jnp.tile(sin, (1, reps))


def _gate_row(p):
    return jnp.zeros((1, LANES), F32).at[0, DECAY_LANE:DECAY_LANE + N_DIR * DN_HEADS].set(p.reshape(-1))


def kernel(x, norm_w, w_in, conv_w, a_log, dt_bias, dn_norm_w, q_norm_w, k_norm_w, w_out, final_norm_w):
    b, t, d = x.shape
    depth = w_in.shape[0]
    tm = 256
    cb = min(4, t // CHUNK)
    tq = min(1024, t)
    tk = min(512, t // 2)
    cos_t, sin_t = _rope_tables(t)
    lane = np.arange(ATT_Q)
    seg = jnp.asarray((lane[:, None] // ATT_HEAD_DIM) == (lane[None, :] // ATT_HEAD_DIM), BF16)
    x2 = x.reshape(b * t, d)
    for l in range(depth):
        qkv, dnz, atq, atk, atv, atz, ba = _in_proj(x2, norm_w[l][None, :], _pack_w_in(w_in[l]), tm)
        cw = jnp.zeros((SUBLANES, DN_CONV_DIM), F32).at[:CONV_K].set(conv_w[l])
        qn, kn, vv, gb = _dn_prep(qkv.reshape(b, t, -1), ba.reshape(b, t, -1), cw,
                                  _gate_row(a_log[l]), _gate_row(dt_bias[l]), tm)
        gr = gb[..., :2 * SUBLANES].reshape(b, t // CHUNK, CHUNK, 2 * SUBLANES).transpose(0, 1, 3, 2)
        o_f, o_b = _dn_main(qn, kn, vv, gb, gr, cb)
        qt5, k4, vt4 = _attn_prep(atq.reshape(b, t, -1), atk.reshape(b, t, -1), atv.reshape(b, t, -1),
                                  jnp.tile(q_norm_w[l], ATT_HEADS)[None, :], jnp.tile(k_norm_w[l], ATT_KV_HEADS)[None, :],
                                  seg, cos_t, sin_t, tq)
        o_at = _attention(qt5, k4, vt4, tk).reshape(b * t, ATT_Q)
        x2 = _out_proj(o_f.reshape(b * t, -1), o_b.reshape(b * t, -1), dnz, dn_norm_w[l][None, :], o_at, atz,
                       x2, w_out[l].astype(BF16), final_norm_w[None, :], tm, final=(l == depth - 1))
    return x2.reshape(b, t, d)
```

```python
import functools

import jax
import jax.numpy as jnp
import numpy as np
from jax import lax
from jax.experimental import pallas as pl
from jax.experimental.pallas import tpu as pltpu

D_MODEL = 1024
DN_HEADS = 4
DN_HEAD = 128
DN_QK = DN_HEADS * DN_HEAD
DN_CONV_DIM = 3 * DN_QK
CONV_K = 5
CHUNK = 128
N_DIR = 2
ATT_HEADS = 8
ATT_KV_HEADS = 2
ATT_GROUP = ATT_HEADS // ATT_KV_HEADS
ATT_HEAD_DIM = 64
ATT_Q = ATT_HEADS * ATT_HEAD_DIM
ATT_KV = ATT_KV_HEADS * ATT_HEAD_DIM
ROPE_THETA = 10000.0
GRID_W = 64
EPS = 1e-6
IN_SIZES = (DN_CONV_DIM, DN_QK, N_DIR * DN_HEADS, N_DIR * DN_HEADS, ATT_Q, ATT_KV, ATT_KV, ATT_Q)
LANES = 128
SUBLANES = 8
VMEM_LIMIT = 48 * 1024 * 1024
PACK_COLS = DN_CONV_DIM + DN_QK + ATT_Q + ATT_KV + ATT_KV + ATT_Q + LANES
BETA_LANE = 0
DECAY_LANE = N_DIR * DN_HEADS
DN_STREAM_HEADS = 4
MXU_WIDTH = 256
DN_PACK = MXU_WIDTH // CHUNK

F32 = jnp.float32
BF16 = jnp.bfloat16
HI = lax.Precision.HIGHEST
LOG2E = 1.4426950408889634


def _params(sem):
    return pltpu.CompilerParams(dimension_semantics=sem, vmem_limit_bytes=VMEM_LIMIT)


def _silu(x):
    return x / (1.0 + jnp.exp(-x))


def _in_proj_kernel(x_ref, nw_ref, w_ref, qkv_ref, dnz_ref, atq_ref, atk_ref, atv_ref, atz_ref, ba_ref):
    x = x_ref[...]
    h = x * lax.rsqrt(jnp.mean(x * x, axis=-1, keepdims=True) + EPS) * nw_ref[...]
    p = jnp.dot(h.astype(BF16), w_ref[...], preferred_element_type=F32)
    o = 0
    for ref in (qkv_ref, dnz_ref, atq_ref, atk_ref, atv_ref, atz_ref, ba_ref):
        n = ref.shape[-1]
        ref[...] = p[:, o:o + n].astype(ref.dtype)
        o += n


def _in_proj(x2, nw, w_packed, tm):
    bt = x2.shape[0]
    widths = (DN_CONV_DIM, DN_QK, ATT_Q, ATT_KV, ATT_KV, ATT_Q, LANES)
    dtypes = (F32, BF16, BF16, BF16, BF16, BF16, F32)
    return pl.pallas_call(
        _in_proj_kernel,
        grid=(bt // tm,),
        in_specs=[pl.BlockSpec((tm, D_MODEL), lambda i: (i, 0)),
                  pl.BlockSpec((1, D_MODEL), lambda i: (0, 0)),
                  pl.BlockSpec((D_MODEL, PACK_COLS), lambda i: (0, 0))],
        out_specs=[pl.BlockSpec((tm, n), lambda i: (i, 0)) for n in widths],
        out_shape=[jax.ShapeDtypeStruct((bt, n), dt) for n, dt in zip(widths, dtypes)],
        compiler_params=_params(("parallel",)),
        name="in_proj",
    )(x2, nw, w_packed)


def _dn_prep_kernel(x_ref, prev_ref, next_ref, ba_ref, cw_ref, alog_ref, dtb_ref,
                    q_ref, k_ref, v_ref, gb_ref, xe_ref, *, tm):
    i = pl.program_id(1)
    n = pl.num_programs(1)
    halo = SUBLANES
    xe_ref[0:halo, :] = jnp.where(i > 0, prev_ref[...], 0.0)
    xe_ref[halo:halo + tm, :] = x_ref[...]
    xe_ref[halo + tm:2 * halo + tm, :] = jnp.where(i < n - 1, next_ref[...], 0.0)
    pad = CONV_K // 2
    y = jnp.zeros((tm, DN_CONV_DIM), F32)
    for j in range(CONV_K):
        y = y + xe_ref[halo + j - pad:halo + j - pad + tm, :] * cw_ref[j:j + 1, :]
    y = _silu(y)
    for hd in range(DN_HEADS):
        lo = hd * DN_HEAD
        q = y[:, lo:lo + DN_HEAD]
        k = y[:, DN_QK + lo:DN_QK + lo + DN_HEAD]
        q_ref[:, lo:lo + DN_HEAD] = q * lax.rsqrt(jnp.sum(q * q, axis=-1, keepdims=True) + EPS) * (DN_HEAD ** -0.5)
        k_ref[:, lo:lo + DN_HEAD] = k * lax.rsqrt(jnp.sum(k * k, axis=-1, keepdims=True) + EPS)
    v_ref[...] = y[:, 2 * DN_QK:]

    ba = ba_ref[...]
    beta = 1.0 / (1.0 + jnp.exp(-ba))
    z = ba + dtb_ref[...]
    softplus = jnp.maximum(z, 0.0) + jnp.log(1.0 + jnp.exp(-jnp.abs(z)))
    g = -jnp.exp(alog_ref[...]) * softplus
    r = lax.broadcasted_iota(jnp.int32, (tm, tm), 0)
    c = lax.broadcasted_iota(jnp.int32, (tm, tm), 1)
    same = (r // CHUNK) == (c // CHUNK)
    tri_f = jnp.where(same & (c <= r), 1.0, 0.0).astype(F32)
    tri_b = jnp.where(same & (c >= r), 1.0, 0.0).astype(F32)
    gc_f = jnp.dot(tri_f, g, precision=HI, preferred_element_type=F32)
    gc_b = jnp.dot(tri_b, g, precision=HI, preferred_element_type=F32)
    lane = lax.broadcasted_iota(jnp.int32, (tm, LANES), 1)
    gc = jnp.where(lane < DECAY_LANE + DN_HEADS, gc_f, gc_b)
    gb_ref[...] = jnp.where(lane < DECAY_LANE, beta, gc)


def _dn_prep(qkv, ba, cw, alog_row, dtb_row, tm):
    b, t, _ = qkv.shape
    nb = tm // SUBLANES
    last = t // SUBLANES - 1
    kern = functools.partial(_dn_prep_kernel, tm=tm)
    return pl.pallas_call(
        kern,
        grid=(b, t // tm),
        in_specs=[pl.BlockSpec((None, tm, DN_CONV_DIM), lambda bi, i: (bi, i, 0)),
                  pl.BlockSpec((None, SUBLANES, DN_CONV_DIM), lambda bi, i: (bi, jnp.maximum(i * nb - 1, 0), 0)),
                  pl.BlockSpec((None, SUBLANES, DN_CONV_DIM), lambda bi, i: (bi, jnp.minimum((i + 1) * nb, last), 0)),
                  pl.BlockSpec((None, tm, LANES), lambda bi, i: (bi, i, 0)),
                  pl.BlockSpec((SUBLANES, DN_CONV_DIM), lambda bi, i: (0, 0)),
                  pl.BlockSpec((1, LANES), lambda bi, i: (0, 0)),
                  pl.BlockSpec((1, LANES), lambda bi, i: (0, 0))],
        out_specs=[pl.BlockSpec((None, tm, DN_QK), lambda bi, i: (bi, i, 0))] * 3
        + [pl.BlockSpec((None, tm, LANES), lambda bi, i: (bi, i, 0))],
        out_shape=[jax.ShapeDtypeStruct((b, t, DN_QK), F32)] * 3 + [jax.ShapeDtypeStruct((b, t, LANES), F32)],
        scratch_shapes=[pltpu.VMEM((tm + 2 * SUBLANES, DN_CONV_DIM), F32)],
        compiler_params=_params(("parallel", "parallel")),
        name="dn_prep",
    )(qkv, qkv, qkv, ba, cw, alog_row, dtb_row)


def _mm(a, b):
    return jnp.dot(a.astype(BF16), b.astype(BF16), preferred_element_type=F32)


def _bmm(a, b):
    return jnp.einsum('cij,cjk->cik', a.astype(BF16), b.astype(BF16), preferred_element_type=F32)


def _bmm_nt(a, b):
    return jnp.einsum('cid,cjd->cij', a.astype(BF16), b.astype(BF16), preferred_element_type=F32)


def _dn_masks(d, pack=1):
    ii = lax.broadcasted_iota(jnp.int32, (CHUNK, pack * CHUNK), 0)
    jj = lax.broadcasted_iota(jnp.int32, (CHUNK, pack * CHUNK), 1) % CHUNK
    if d == 1:
        ii, jj = jj, ii
    eye = jnp.where(ii == jj, 1.0, 0.0).astype(F32)
    merge = []
    s = 1
    while s < CHUNK:
        merge.append((ii // (2 * s) == jj // (2 * s)) & ((ii // s) % 2 == 1) & ((jj // s) % 2 == 0))
        s *= 2
    return ii >= jj, ii > jj, eye, merge


def _dn_chunk_terms(streams, cb, out):
    masks = {d: _dn_masks(d) for d in sorted({s[0] for s in streams})}
    data = []
    for d, hds, (q_ref, k_ref, v_ref, gb_ref, gr_ref) in streams:
        gb = gb_ref[...]
        gr = gr_ref[...]

        def heads(ref):
            return jnp.concatenate([ref[:, hd * DN_HEAD:(hd + 1) * DN_HEAD].reshape(cb, CHUNK, DN_HEAD)
                                    for hd in hds], axis=0)

        def gate_cols(lane0):
            return jnp.concatenate([gb[:, lane0 + hd:lane0 + hd + 1].reshape(cb, CHUNK, 1) for hd in hds], axis=0)

        q, k, v = heads(q_ref), heads(k_ref), heads(v_ref)
        beta = gate_cols(BETA_LANE + d * DN_HEADS)
        gcol = gate_cols(DECAY_LANE + d * DN_HEADS)
        gl = DECAY_LANE + d * DN_HEADS
        grow = jnp.concatenate([gr[:, gl + hd:gl + hd + 1, :] for hd in hds], axis=0)
        glast = grow[:, :, CHUNK - 1:CHUNK] if d == 0 else grow[:, :, 0:1]
        decay = jnp.where(masks[d][0], jnp.exp(jnp.minimum(gcol - grow, 0.0)), 0.0)
        data.append(dict(d=d, q=q, k=k, v=v, beta=beta, gcol=gcol, glast=glast, decay=decay, kb=k * beta))
    n = range(len(data))
    lower = [jnp.where(masks[x['d']][1], _bmm_nt(x['kb'], x['k']) * x['decay'], 0.0) for x in data]
    wmask = {d: _dn_masks(d, DN_PACK) for d in masks}
    unit = lax.broadcasted_iota(jnp.int32, (CHUNK, DN_PACK * CHUNK), 1) // CHUNK

    def wide(x):
        x4 = x.reshape(x.shape[0] // DN_PACK, DN_PACK, CHUNK, CHUNK)
        return jnp.concatenate([x4[:, j] for j in range(DN_PACK)], axis=-1)

    def block_diag(xw):
        return jnp.concatenate([jnp.where(unit == j, xw, 0.0) for j in range(DN_PACK)], axis=1)

    yield
    lower_w = [wide(lower[i]) for i in n]
    tinv = [wmask[data[i]['d']][2] - jnp.where(wmask[data[i]['d']][3][0], lower_w[i], 0.0) for i in n]
    for lvl in range(1, len(wmask[data[0]['d']][3])):
        te = [_bmm(tinv[i], block_diag(jnp.where(wmask[data[i]['d']][3][lvl], lower_w[i], 0.0))) for i in n]
        yield
        tinv = [tinv[i] - _bmm(te[i], block_diag(tinv[i])) for i in n]
        yield
    sol = []
    for i, x in enumerate(data):
        rhs = jnp.concatenate([x['v'] * x['beta'], x['kb'] * jnp.exp(x['gcol'])], axis=-1)
        units = rhs.shape[0]
        s_tall = _bmm(block_diag(tinv[i]), rhs.reshape(units // DN_PACK, DN_PACK * CHUNK, 2 * DN_HEAD))
        sol.append(s_tall.reshape(units, CHUNK, 2 * DN_HEAD))
    yield
    for i, x in enumerate(data):
        u, w = sol[i][..., :DN_HEAD], sol[i][..., DN_HEAD:]
        attn = (_bmm_nt(x['q'], x['k']) * x['decay']).astype(BF16)
        wq = jnp.concatenate([w, x['q'] * jnp.exp(x['gcol'])], axis=1).astype(BF16)
        kdec = (x['k'] * jnp.exp(x['glast'] - x['gcol'])).astype(BF16)
        out.append((u, wq, attn, kdec, jnp.exp(x['glast'])))


def _dn_scan(streams, slot, bufs, o_refs, s_ref, cb):
    u_buf, wq_buf, attn_buf, kdec_buf, eg_buf = bufs
    units = DN_STREAM_HEADS * cb
    for step in range(cb):
        chains = []
        for si, (d, hds, _) in enumerate(streams):
            c = step if d == 0 else cb - 1 - step
            chains += [(d, hd, c, si * units + i * cb + c) for i, hd in enumerate(hds)]
        states = [s_ref[d * DN_HEADS + hd] for d, hd, _, _ in chains]
        ws = [_mm(wq_buf[slot, n], st) for (_, _, _, n), st in zip(chains, states)]
        yield
        for (d, hd, c, n), st, w in zip(chains, states, ws):
            v_new = (u_buf[slot, n] - w[:CHUNK]).astype(BF16)
            o = w[CHUNK:] + _mm(attn_buf[slot, n], v_new)
            s_ref[d * DN_HEADS + hd] = st * eg_buf[slot, n] + lax.dot_general(
                kdec_buf[slot, n], v_new, (((0,), (0,)), ((), ())), preferred_element_type=F32)
            o_refs[d][c * CHUNK:(c + 1) * CHUNK, hd * DN_HEAD:(hd + 1) * DN_HEAD] = o.astype(o_refs[d].dtype)
        yield


def _dn_main_kernel(qf, kf, vf, gbf, grf, qb, kb, vb, gbb, grb, of_ref, ob_ref, s_ref,
                    u_buf, wq_buf, attn_buf, kdec_buf, eg_buf, *, cb):
    j = pl.program_id(1)
    bufs = (u_buf, wq_buf, attn_buf, kdec_buf, eg_buf)

    @pl.when(j == 0)
    def _():
        s_ref[...] = jnp.zeros_like(s_ref)
        for buf in bufs:
            buf[1] = jnp.zeros(buf.shape[1:], buf.dtype)

    cur = (j + 1) % 2
    nxt = j % 2
    groups = [tuple(range(h, h + DN_STREAM_HEADS)) for h in range(0, DN_HEADS, DN_STREAM_HEADS)]
    streams = ([(0, hds, (qf, kf, vf, gbf, grf)) for hds in groups]
               + [(1, hds, (qb, kb, vb, gbb, grb)) for hds in groups])
    units = DN_STREAM_HEADS * cb
    pre = []
    terms = _dn_chunk_terms(streams, cb, pre)
    scan = _dn_scan(streams, cur, bufs, (of_ref, ob_ref), s_ref, cb)
    live = [terms, scan]
    while live:
        for gen, reps in ((terms, 2), (scan, 1)):
            for _ in range(reps):
                if gen in live and next(gen, StopIteration) is StopIteration:
                    live.remove(gen)
    for si, (u, wq, attn, kdec, eg) in enumerate(pre):
        lo, hi = si * units, (si + 1) * units
        u_buf[nxt, lo:hi] = u
        wq_buf[nxt, lo:hi] = wq
        attn_buf[nxt, lo:hi] = attn
        kdec_buf[nxt, lo:hi] = kdec
        eg_buf[nxt, lo:hi] = jnp.broadcast_to(eg, (units, 1, DN_HEAD))


def _dn_main(qn, kn, vv, gb, gr, cb):
    b, t, _ = qn.shape
    cbt = cb * CHUNK
    ng = t // cbt
    fwd_in = lambda bi, j: (bi, jnp.minimum(j, ng - 1), 0)
    bwd_in = lambda bi, j: (bi, ng - 1 - jnp.minimum(j, ng - 1), 0)
    fwd_in4 = lambda bi, j: (bi, jnp.minimum(j, ng - 1), 0, 0)
    bwd_in4 = lambda bi, j: (bi, ng - 1 - jnp.minimum(j, ng - 1), 0, 0)
    fwd_out = lambda bi, j: (bi, jnp.maximum(j - 1, 0), 0)
    bwd_out = lambda bi, j: (bi, ng - 1 - jnp.maximum(j - 1, 0), 0)

    def specs(im3, im4):
        return ([pl.BlockSpec((None, cbt, DN_QK), im3)] * 3
                + [pl.BlockSpec((None, cbt, LANES), im3), pl.BlockSpec((None, cb, 2 * SUBLANES, CHUNK), im4)])

    units = N_DIR * DN_HEADS * cb
    kern = functools.partial(_dn_main_kernel, cb=cb)
    return pl.pallas_call(
        kern,
        grid=(b, ng + 1),
        in_specs=specs(fwd_in, fwd_in4) + specs(bwd_in, bwd_in4),
        out_specs=[pl.BlockSpec((None, cbt, DN_QK), fwd_out), pl.BlockSpec((None, cbt, DN_QK), bwd_out)],
        out_shape=[jax.ShapeDtypeStruct((b, t, DN_QK), BF16)] * 2,
        scratch_shapes=[pltpu.VMEM((N_DIR * DN_HEADS, DN_HEAD, DN_HEAD), F32),
                        pltpu.VMEM((2, units, CHUNK, DN_HEAD), F32),
                        pltpu.VMEM((2, units, 2 * CHUNK, DN_HEAD), BF16),
                        pltpu.VMEM((2, units, CHUNK, CHUNK), BF16),
                        pltpu.VMEM((2, units, CHUNK, DN_HEAD), BF16),
                        pltpu.VMEM((2, units, 1, DN_HEAD), F32)],
        compiler_params=_params(("parallel", "arbitrary")),
        name="dn_main",
    )(qn, kn, vv, gb, gr, qn, kn, vv, gb, gr)


def _rope_norm(x, w, seg, cos, sin):
    sq = x * x
    sq_hi = sq.astype(BF16)
    sq_lo = (sq - sq_hi.astype(F32)).astype(BF16)
    ss = jnp.dot(sq_hi, seg, preferred_element_type=F32) + jnp.dot(sq_lo, seg, preferred_element_type=F32)
    xn = x * lax.rsqrt(ss * (1.0 / ATT_HEAD_DIM) + EPS) * w
    quarter = ATT_HEAD_DIM // 4
    outs = []
    for t in range(x.shape[-1] // LANES):
        xt = xn[:, t * LANES:(t + 1) * LANES]
        lane = lax.broadcasted_iota(jnp.int32, xt.shape, 1)
        rot = jnp.where(lane % (2 * quarter) < quarter,
                        pltpu.roll(xt, LANES - quarter, axis=1), pltpu.roll(xt, quarter, axis=1))
        outs.append(xt * cos + rot * sin)
    return outs[0] if len(outs) == 1 else jnp.concatenate(outs, axis=-1)


def _attn_prep_kernel(q_ref, k_ref, v_ref, qw_ref, kw_ref, seg_ref, cos_ref, sin_ref, qo_ref, ko_ref, vo_ref):
    cos = cos_ref[...]
    sin = sin_ref[...]
    q = _rope_norm(q_ref[...].astype(F32), qw_ref[...], seg_ref[...], cos, sin)
    qt = (q * (LOG2E * ATT_HEAD_DIM ** -0.5)).T.astype(BF16)
    for kvh in range(ATT_KV_HEADS):
        rows = [qt[(kvh * ATT_GROUP + g) * ATT_HEAD_DIM:(kvh * ATT_GROUP + g + 1) * ATT_HEAD_DIM] for g in range(ATT_GROUP)]
        qo_ref[kvh] = jnp.concatenate(rows, axis=1)
    k = _rope_norm(k_ref[...].astype(F32), kw_ref[...], seg_ref[0:LANES, 0:LANES], cos, sin).astype(BF16)
    vt = v_ref[...].astype(F32).T.astype(BF16)
    for kvh in range(ATT_KV_HEADS):
        ko_ref[kvh] = k[:, kvh * ATT_HEAD_DIM:(kvh + 1) * ATT_HEAD_DIM]
        vo_ref[kvh] = vt[kvh * ATT_HEAD_DIM:(kvh + 1) * ATT_HEAD_DIM]


def _attn_prep(atq, atk, atv, qw_row, kw_row, seg, cos_t, sin_t, tq):
    b, t, _ = atq.shape
    nq = t // tq
    row = lambda bi, i: (bi, i, 0)
    const = lambda bi, i: (0, 0)
    return pl.pallas_call(
        _attn_prep_kernel,
        grid=(b, nq),
        in_specs=[pl.BlockSpec((None, tq, ATT_Q), row), pl.BlockSpec((None, tq, ATT_KV), row),
                  pl.BlockSpec((None, tq, ATT_KV), row),
                  pl.BlockSpec((1, ATT_Q), const), pl.BlockSpec((1, ATT_KV), const),
                  pl.BlockSpec((ATT_Q, ATT_Q), const),
                  pl.BlockSpec((tq, LANES), lambda bi, i: (i, 0)), pl.BlockSpec((tq, LANES), lambda bi, i: (i, 0))],
        out_specs=[pl.BlockSpec((None, ATT_KV_HEADS, None, ATT_HEAD_DIM, ATT_GROUP * tq), lambda bi, i: (bi, 0, i, 0, 0)),
                   pl.BlockSpec((None, ATT_KV_HEADS, tq, ATT_HEAD_DIM), lambda bi, i: (bi, 0, i, 0)),
                   pl.BlockSpec((None, ATT_KV_HEADS, ATT_HEAD_DIM, tq), lambda bi, i: (bi, 0, 0, i))],
        out_shape=[jax.ShapeDtypeStruct((b, ATT_KV_HEADS, nq, ATT_HEAD_DIM, ATT_GROUP * tq), BF16),
                   jax.ShapeDtypeStruct((b, ATT_KV_HEADS, t, ATT_HEAD_DIM), BF16),
                   jax.ShapeDtypeStruct((b, ATT_KV_HEADS, ATT_HEAD_DIM, t), BF16)],
        compiler_params=_params(("parallel", "parallel")),
        name="attn_prep",
    )(atq, atk, atv, qw_row, kw_row, seg, cos_t, sin_t)


def _attn_kernel(qt_ref, k_ref, vt_ref, o_ref, s_buf, p_buf, *, tk):
    m_cols = qt_ref.shape[1]
    qt = qt_ref[...]
    nk = k_ref.shape[0] // tk
    ones = jnp.ones((2 * SUBLANES, tk), BF16)

    def scores(j, slot):
        start = pl.multiple_of(j * tk, tk)
        s = jnp.dot(k_ref[pl.ds(start, tk), :], qt, preferred_element_type=F32)
        s_buf[slot] = s
        return jnp.max(jnp.max(s.reshape(tk // SUBLANES, SUBLANES, m_cols), axis=0), axis=0, keepdims=True)

    def softmax(slot, m, chunk_max):
        m_new = jnp.maximum(m, chunk_max)
        p_buf[slot] = jnp.exp2(s_buf[slot] - m_new).astype(BF16)
        return m_new, jnp.exp2(m - m_new)

    def values(j, slot, alpha, acc):
        start = pl.multiple_of(j * tk, tk)
        vt = jnp.concatenate([vt_ref[:, pl.ds(start, tk)], ones], axis=0)
        return alpha * acc + jnp.dot(vt, p_buf[slot], preferred_element_type=F32)

    m = jnp.full((1, m_cols), jnp.finfo(F32).min, F32)
    acc = jnp.zeros((ATT_HEAD_DIM + 2 * SUBLANES, m_cols), F32)
    cmax0 = scores(0, 0)
    cmax1 = scores(1, 1)
    m, alpha0 = softmax(0, m, cmax0)
    cmax2 = scores(2, 0)
    m, alpha1 = softmax(1, m, cmax1)
    acc = values(0, 0, alpha0, acc)

    def body(i, carry):
        m, cmax, alpha_prev, acc = carry
        j0 = 2 * i
        cmax1 = scores(j0 + 1, 1)
        m, alpha0 = softmax(0, m, cmax)
        acc = values(j0 - 1, 1, alpha_prev, acc)
        cmax2 = scores(j0 + 2, 0)
        m, alpha1 = softmax(1, m, cmax1)
        acc = values(j0, 0, alpha0, acc)
        return m, cmax2, alpha1, acc

    m, cmax, alpha_prev, acc = lax.fori_loop(1, nk // 2 - 1, body, (m, cmax2, alpha1, acc))
    cmax1 = scores(nk - 1, 1)
    m, alpha0 = softmax(0, m, cmax)
    acc = values(nk - 3, 1, alpha_prev, acc)
    m, alpha = softmax(1, m, cmax1)
    acc = values(nk - 2, 0, alpha0, acc)
    acc = values(nk - 1, 1, alpha, acc)
    o = acc[:ATT_HEAD_DIM] / acc[ATT_HEAD_DIM:ATT_HEAD_DIM + 1]
    tq = m_cols // ATT_GROUP
    o = jnp.concatenate([o[:, g * tq:(g + 1) * tq] for g in range(ATT_GROUP)], axis=0)
    o_ref[...] = o.T.astype(o_ref.dtype)


def _attention(qt5, k4, vt4, tk):
    b, _, nq, _, m_cols = qt5.shape
    t = k4.shape[2]
    tq = m_cols // ATT_GROUP
    assert t % (2 * tk) == 0 and t // tk >= 4
    kern = functools.partial(_attn_kernel, tk=tk)
    return pl.pallas_call(
        kern,
        grid=(b, ATT_KV_HEADS, nq),
        in_specs=[pl.BlockSpec((None, None, None, ATT_HEAD_DIM, m_cols), lambda bi, h, i: (bi, h, i, 0, 0)),
                  pl.BlockSpec((None, None, t, ATT_HEAD_DIM), lambda bi, h, i: (bi, h, 0, 0)),
                  pl.BlockSpec((None, None, ATT_HEAD_DIM, t), lambda bi, h, i: (bi, h, 0, 0))],
        out_specs=pl.BlockSpec((None, tq, ATT_GROUP * ATT_HEAD_DIM), lambda bi, h, i: (bi, i, h)),
        out_shape=jax.ShapeDtypeStruct((b, t, ATT_Q), BF16),
        scratch_shapes=[pltpu.VMEM((2, tk, m_cols), F32), pltpu.VMEM((2, tk, m_cols), BF16)],
        compiler_params=_params(("parallel", "parallel", "parallel")),
        name="attention",
    )(qt5, k4, vt4)


def _out_proj_kernel(of_ref, ob_ref, dnz_ref, dnw_ref, oat_ref, atz_ref, x_ref, w_ref, fw_ref, y_ref, *, final):
    o = of_ref[...].astype(F32) + ob_ref[...].astype(F32)
    parts = []
    for hd in range(DN_HEADS):
        oh = o[:, hd * DN_HEAD:(hd + 1) * DN_HEAD]
        parts.append(oh * lax.rsqrt(jnp.mean(oh * oh, axis=-1, keepdims=True) + EPS) * dnw_ref[...])
    y_dn = jnp.concatenate(parts, axis=-1) * _silu(dnz_ref[...].astype(F32))
    y_at = oat_ref[...].astype(F32) * _silu(atz_ref[...].astype(F32))
    y = jnp.concatenate([y_dn, y_at], axis=-1).astype(BF16)
    out = x_ref[...] + jnp.dot(y, w_ref[...], preferred_element_type=F32)
    if final:
        out = out * lax.rsqrt(jnp.mean(out * out, axis=-1, keepdims=True) + EPS) * fw_ref[...]
    y_ref[...] = out


def _out_proj(o_f, o_b, dnz, dnw_row, o_at, atz, x2, w_out, fw_row, tm, final):
    bt = x2.shape[0]
    row = lambda i: (i, 0)
    const = lambda i: (0, 0)
    kern = functools.partial(_out_proj_kernel, final=final)
    return pl.pallas_call(
        kern,
        grid=(bt // tm,),
        in_specs=[pl.BlockSpec((tm, DN_QK), row), pl.BlockSpec((tm, DN_QK), row), pl.BlockSpec((tm, DN_QK), row),
                  pl.BlockSpec((1, DN_HEAD), const),
                  pl.BlockSpec((tm, ATT_Q), row), pl.BlockSpec((tm, ATT_Q), row),
                  pl.BlockSpec((tm, D_MODEL), row),
                  pl.BlockSpec((D_MODEL, D_MODEL), const), pl.BlockSpec((1, D_MODEL), const)],
        out_specs=pl.BlockSpec((tm, D_MODEL), row),
        out_shape=jax.ShapeDtypeStruct((bt, D_MODEL), F32),
        compiler_params=_params(("parallel",)),
        name="out_proj_final" if final else "out_proj",
    )(o_f, o_b, dnz, dnw_row, o_at, atz, x2, w_out, fw_row)


def _pack_w_in(w):
    splits = np.cumsum(IN_SIZES)[:-1].tolist()
    qkv, dnz, bb, aa, atq, atk, atv, atz = jnp.split(w, splits, axis=-1)
    pad = jnp.zeros((w.shape[0], LANES - 2 * N_DIR * DN_HEADS), w.dtype)
    return jnp.concatenate([qkv, dnz, atq, atk, atv, atz, bb, aa, pad], axis=-1).astype(BF16)


def _rope_tables(t):
    half = ATT_HEAD_DIM // 2
    pos = np.arange(t)
    row = (pos // GRID_W).astype(np.float32)
    col = (pos % GRID_W).astype(np.float32)
    inv = jnp.asarray(ROPE_THETA, F32) ** (-jnp.arange(0, half, 2, dtype=F32) / half)
    ang_r = jnp.asarray(row)[:, None] * inv[None, :]
    ang_c = jnp.asarray(col)[:, None] * inv[None, :]
    cos = jnp.concatenate([jnp.cos(ang_r)] * 2 + [jnp.cos(ang_c)] * 2, axis=-1)
    sin = jnp.concatenate([-jnp.sin(ang_r), jnp.sin(ang_r), -jnp.sin(ang_c), jnp.sin(ang_c)], axis=-1)
    reps = LANES // ATT_HEAD_DIM
    return jnp.tile(cos, (1, reps)), jnp.tile(sin, (1, reps))


def _gate_row(p):
    return jnp.zeros((1, LANES), F32).at[0, DECAY_LANE:DECAY_LANE + N_DIR * DN_HEADS].set(p.reshape(-1))


def kernel(x, norm_w, w_in, conv_w, a_log, dt_bias, dn_norm_w, q_norm_w, k_norm_w, w_out, final_norm_w):
    b, t, d = x.shape
    depth = w_in.shape[0]
    tm = 256
    tm_proj = min(512, b * t)
    cb = min(4, t // CHUNK)
    tq = min(1024, t)
    tk = min(512, t // 4)
    cos_t, sin_t = _rope_tables(t)
    lane = np.arange(ATT_Q)
    seg = jnp.asarray((lane[:, None] // ATT_HEAD_DIM) == (lane[None, :] // ATT_HEAD_DIM), BF16)
    x2 = x.reshape(b * t, d)
    for l in range(depth):
        qkv, dnz, atq, atk, atv, atz, ba = _in_proj(x2, norm_w[l][None, :], _pack_w_in(w_in[l]), tm_proj)
        cw = jnp.zeros((SUBLANES, DN_CONV_DIM), F32).at[:CONV_K].set(conv_w[l])
        qn, kn, vv, gb = _dn_prep(qkv.reshape(b, t, -1), ba.reshape(b, t, -1), cw,
                                  _gate_row(a_log[l]), _gate_row(dt_bias[l]), tm)
        gr = gb[..., :2 * SUBLANES].reshape(b, t // CHUNK, CHUNK, 2 * SUBLANES).transpose(0, 1, 3, 2)
        o_f, o_b = _dn_main(qn, kn, vv, gb, gr, cb)
        qt5, k4, vt4 = _attn_prep(atq.reshape(b, t, -1), atk.reshape(b, t, -1), atv.reshape(b, t, -1),
                                  jnp.tile(q_norm_w[l], ATT_HEADS)[None, :], jnp.tile(k_norm_w[l], ATT_KV_HEADS)[None, :],
                                  seg, cos_t, sin_t, tq)
        o_at = _attention(qt5, k4, vt4, tk).reshape(b * t, ATT_Q)
        x2 = _out_proj(o_f.reshape(b * t, -1), o_b.reshape(b * t, -1), dnz, dn_norm_w[l][None, :], o_at, atz,
                       x2, w_out[l].astype(BF16), final_norm_w[None, :], tm_proj, final=(l == depth - 1))
    return x2.reshape(b, t, d)
```

```python
import functools

import jax
import jax.numpy as jnp
import numpy as np
from jax import lax
from jax.experimental import pallas as pl
from jax.experimental.pallas import tpu as pltpu

D_MODEL = 1024
DN_HEADS = 4
DN_HEAD = 128
DN_QK = DN_HEADS * DN_HEAD
DN_CONV_DIM = 3 * DN_QK
CONV_K = 5
CHUNK = 128
N_DIR = 2
ATT_HEADS = 8
ATT_KV_HEADS = 2
ATT_GROUP = ATT_HEADS // ATT_KV_HEADS
ATT_HEAD_DIM = 64
ATT_Q = ATT_HEADS * ATT_HEAD_DIM
ATT_KV = ATT_KV_HEADS * ATT_HEAD_DIM
ROPE_THETA = 10000.0
GRID_W = 64
EPS = 1e-6
IN_SIZES = (DN_CONV_DIM, DN_QK, N_DIR * DN_HEADS, N_DIR * DN_HEADS, ATT_Q, ATT_KV, ATT_KV, ATT_Q)
LANES = 128
SUBLANES = 8
VMEM_LIMIT = 48 * 1024 * 1024
PACK_WIDTHS = (ATT_Q, ATT_KV, ATT_KV, DN_CONV_DIM, DN_QK, ATT_Q, LANES)
PACK_ATT_COLS = ATT_Q + 2 * ATT_KV
PACK_COLS = sum(PACK_WIDTHS)
BETA_LANE = 0
DECAY_LANE = N_DIR * DN_HEADS
DN_STREAM_HEADS = 4
MXU_WIDTH = 256
DN_PACK = MXU_WIDTH // CHUNK

F32 = jnp.float32
BF16 = jnp.bfloat16
HI = lax.Precision.HIGHEST
LOG2E = 1.4426950408889634


def _params(sem):
    return pltpu.CompilerParams(dimension_semantics=sem, vmem_limit_bytes=VMEM_LIMIT)


def _silu(x):
    return x / (1.0 + jnp.exp(-x))


def _in_proj_kernel(x_ref, nw_ref, w_ref, qw_ref, kw_ref, seg_ref, cos_ref, sin_ref,
                    qkv_ref, dnz_ref, atz_ref, ba_ref, qo_ref, ko_ref, vo_ref):
    x = x_ref[...]
    h = x * lax.rsqrt(jnp.mean(x * x, axis=-1, keepdims=True) + EPS) * nw_ref[...]
    hb = h.astype(BF16)
    p_att = jnp.dot(hb, w_ref[:, :PACK_ATT_COLS], preferred_element_type=F32)
    atq, atk, atv = p_att[:, :ATT_Q], p_att[:, ATT_Q:ATT_Q + ATT_KV], p_att[:, ATT_Q + ATT_KV:]
    cos = cos_ref[...]
    sin = sin_ref[...]
    q = _rope_norm(atq, qw_ref[...], seg_ref[...], cos, sin)
    k = _rope_norm(atk, kw_ref[...], seg_ref[0:LANES, 0:LANES], cos, sin).astype(BF16)
    p_rest = jnp.dot(hb, w_ref[:, PACK_ATT_COLS:], preferred_element_type=F32)
    qt = (q * (LOG2E * ATT_HEAD_DIM ** -0.5)).T.astype(BF16)
    for hd in range(ATT_HEADS):
        qo_ref[hd // ATT_GROUP, :, hd % ATT_GROUP, :] = qt[hd * ATT_HEAD_DIM:(hd + 1) * ATT_HEAD_DIM]
    vt = atv.T.astype(BF16)
    for kvh in range(ATT_KV_HEADS):
        ko_ref[kvh] = k[:, kvh * ATT_HEAD_DIM:(kvh + 1) * ATT_HEAD_DIM]
        vo_ref[kvh] = vt[kvh * ATT_HEAD_DIM:(kvh + 1) * ATT_HEAD_DIM]
    o = 0
    for ref in (qkv_ref, dnz_ref, atz_ref, ba_ref):
        n = ref.shape[-1]
        ref[...] = p_rest[:, o:o + n].astype(ref.dtype)
        o += n


def _in_proj(x2, nw, w_packed, qw_row, kw_row, seg, cos_t, sin_t, b, t, tm, tq):
    bt = x2.shape[0]
    nr = t // tm
    per_q = tq // tm
    row = lambda i: (i, 0)
    const = lambda i: (0, 0)
    tab = lambda i: (i % nr, 0)
    return pl.pallas_call(
        _in_proj_kernel,
        grid=(bt // tm,),
        in_specs=[pl.BlockSpec((tm, D_MODEL), row), pl.BlockSpec((1, D_MODEL), const),
                  pl.BlockSpec((D_MODEL, PACK_COLS), const),
                  pl.BlockSpec((1, ATT_Q), const), pl.BlockSpec((1, ATT_KV), const), pl.BlockSpec((ATT_Q, ATT_Q), const),
                  pl.BlockSpec((tm, LANES), tab), pl.BlockSpec((tm, LANES), tab)],
        out_specs=[pl.BlockSpec((tm, DN_CONV_DIM), row), pl.BlockSpec((tm, DN_QK), row), pl.BlockSpec((tm, ATT_Q), row),
                   pl.BlockSpec((tm, LANES), row),
                   pl.BlockSpec((None, ATT_KV_HEADS, None, ATT_HEAD_DIM, ATT_GROUP, tm),
                                lambda i: (i // nr, 0, (i % nr) // per_q, 0, 0, (i % nr) % per_q)),
                   pl.BlockSpec((None, ATT_KV_HEADS, tm, ATT_HEAD_DIM), lambda i: (i // nr, 0, i % nr, 0)),
                   pl.BlockSpec((None, ATT_KV_HEADS, ATT_HEAD_DIM, tm), lambda i: (i // nr, 0, 0, i % nr))],
        out_shape=[jax.ShapeDtypeStruct((bt, DN_CONV_DIM), F32), jax.ShapeDtypeStruct((bt, DN_QK), BF16),
                   jax.ShapeDtypeStruct((bt, ATT_Q), BF16), jax.ShapeDtypeStruct((bt, LANES), F32),
                   jax.ShapeDtypeStruct((b, ATT_KV_HEADS, t // tq, ATT_HEAD_DIM, ATT_GROUP, tq), BF16),
                   jax.ShapeDtypeStruct((b, ATT_KV_HEADS, t, ATT_HEAD_DIM), BF16),
                   jax.ShapeDtypeStruct((b, ATT_KV_HEADS, ATT_HEAD_DIM, t), BF16)],
        compiler_params=_params(("parallel",)),
        name="in_proj",
    )(x2, nw, w_packed, qw_row, kw_row, seg, cos_t, sin_t)


def _dn_prep_kernel(x_ref, prev_ref, next_ref, ba_ref, cw_ref, alog_ref, dtb_ref,
                    q_ref, k_ref, v_ref, gb_ref, xe_ref, *, tm):
    i = pl.program_id(1)
    n = pl.num_programs(1)
    halo = SUBLANES
    xe_ref[0:halo, :] = jnp.where(i > 0, prev_ref[...], 0.0)
    xe_ref[halo:halo + tm, :] = x_ref[...]
    xe_ref[halo + tm:2 * halo + tm, :] = jnp.where(i < n - 1, next_ref[...], 0.0)
    pad = CONV_K // 2
    y = jnp.zeros((tm, DN_CONV_DIM), F32)
    for j in range(CONV_K):
        y = y + xe_ref[halo + j - pad:halo + j - pad + tm, :] * cw_ref[j:j + 1, :]
    y = _silu(y)
    for hd in range(DN_HEADS):
        lo = hd * DN_HEAD
        q = y[:, lo:lo + DN_HEAD]
        k = y[:, DN_QK + lo:DN_QK + lo + DN_HEAD]
        q_ref[:, lo:lo + DN_HEAD] = q * lax.rsqrt(jnp.sum(q * q, axis=-1, keepdims=True) + EPS) * (DN_HEAD ** -0.5)
        k_ref[:, lo:lo + DN_HEAD] = k * lax.rsqrt(jnp.sum(k * k, axis=-1, keepdims=True) + EPS)
    v_ref[...] = y[:, 2 * DN_QK:]

    ba = ba_ref[...]
    beta = 1.0 / (1.0 + jnp.exp(-ba))
    z = ba + dtb_ref[...]
    softplus = jnp.maximum(z, 0.0) + jnp.log(1.0 + jnp.exp(-jnp.abs(z)))
    g = -jnp.exp(alog_ref[...]) * softplus
    r = lax.broadcasted_iota(jnp.int32, (tm, tm), 0)
    c = lax.broadcasted_iota(jnp.int32, (tm, tm), 1)
    same = (r // CHUNK) == (c // CHUNK)
    tri_f = jnp.where(same & (c <= r), 1.0, 0.0).astype(F32)
    tri_b = jnp.where(same & (c >= r), 1.0, 0.0).astype(F32)
    gc_f = jnp.dot(tri_f, g, precision=HI, preferred_element_type=F32)
    gc_b = jnp.dot(tri_b, g, precision=HI, preferred_element_type=F32)
    lane = lax.broadcasted_iota(jnp.int32, (tm, LANES), 1)
    gc = jnp.where(lane < DECAY_LANE + DN_HEADS, gc_f, gc_b)
    gb_ref[...] = jnp.where(lane < DECAY_LANE, beta, gc)


def _dn_prep(qkv, ba, cw, alog_row, dtb_row, tm):
    b, t, _ = qkv.shape
    nb = tm // SUBLANES
    last = t // SUBLANES - 1
    kern = functools.partial(_dn_prep_kernel, tm=tm)
    return pl.pallas_call(
        kern,
        grid=(b, t // tm),
        in_specs=[pl.BlockSpec((None, tm, DN_CONV_DIM), lambda bi, i: (bi, i, 0)),
                  pl.BlockSpec((None, SUBLANES, DN_CONV_DIM), lambda bi, i: (bi, jnp.maximum(i * nb - 1, 0), 0)),
                  pl.BlockSpec((None, SUBLANES, DN_CONV_DIM), lambda bi, i: (bi, jnp.minimum((i + 1) * nb, last), 0)),
                  pl.BlockSpec((None, tm, LANES), lambda bi, i: (bi, i, 0)),
                  pl.BlockSpec((SUBLANES, DN_CONV_DIM), lambda bi, i: (0, 0)),
                  pl.BlockSpec((1, LANES), lambda bi, i: (0, 0)),
                  pl.BlockSpec((1, LANES), lambda bi, i: (0, 0))],
        out_specs=[pl.BlockSpec((None, tm, DN_QK), lambda bi, i: (bi, i, 0))] * 3
        + [pl.BlockSpec((None, tm, LANES), lambda bi, i: (bi, i, 0))],
        out_shape=[jax.ShapeDtypeStruct((b, t, DN_QK), F32)] * 3 + [jax.ShapeDtypeStruct((b, t, LANES), F32)],
        scratch_shapes=[pltpu.VMEM((tm + 2 * SUBLANES, DN_CONV_DIM), F32)],
        compiler_params=_params(("parallel", "parallel")),
        name="dn_prep",
    )(qkv, qkv, qkv, ba, cw, alog_row, dtb_row)


def _mm(a, b):
    return jnp.dot(a.astype(BF16), b.astype(BF16), preferred_element_type=F32)


def _bmm(a, b):
    return jnp.einsum('cij,cjk->cik', a.astype(BF16), b.astype(BF16), preferred_element_type=F32)


def _bmm_nt(a, b):
    return jnp.einsum('cid,cjd->cij', a.astype(BF16), b.astype(BF16), preferred_element_type=F32)


def _dn_masks(d, pack=1):
    ii = lax.broadcasted_iota(jnp.int32, (CHUNK, pack * CHUNK), 0)
    jj = lax.broadcasted_iota(jnp.int32, (CHUNK, pack * CHUNK), 1) % CHUNK
    if d == 1:
        ii, jj = jj, ii
    eye = jnp.where(ii == jj, 1.0, 0.0).astype(F32)
    merge = []
    s = 1
    while s < CHUNK:
        merge.append((ii // (2 * s) == jj // (2 * s)) & ((ii // s) % 2 == 1) & ((jj // s) % 2 == 0))
        s *= 2
    return ii >= jj, ii > jj, eye, merge


def _dn_chunk_terms(streams, cb, out):
    masks = {d: _dn_masks(d) for d in sorted({s[0] for s in streams})}
    data = []
    for d, hds, (q_ref, k_ref, v_ref, gb_ref, gr_ref) in streams:
        gb = gb_ref[...]
        gr = gr_ref[...]

        def heads(ref):
            return jnp.concatenate([ref[:, hd * DN_HEAD:(hd + 1) * DN_HEAD].reshape(cb, CHUNK, DN_HEAD)
                                    for hd in hds], axis=0)

        def gate_cols(lane0):
            return jnp.concatenate([gb[:, lane0 + hd:lane0 + hd + 1].reshape(cb, CHUNK, 1) for hd in hds], axis=0)

        q, k, v = heads(q_ref), heads(k_ref), heads(v_ref)
        beta = gate_cols(BETA_LANE + d * DN_HEADS)
        gcol = gate_cols(DECAY_LANE + d * DN_HEADS)
        gl = DECAY_LANE + d * DN_HEADS
        grow = jnp.concatenate([gr[:, gl + hd:gl + hd + 1, :] for hd in hds], axis=0)
        glast = grow[:, :, CHUNK - 1:CHUNK] if d == 0 else grow[:, :, 0:1]
        decay = jnp.where(masks[d][0], jnp.exp(jnp.minimum(gcol - grow, 0.0)), 0.0)
        data.append(dict(d=d, q=q, k=k, v=v, beta=beta, gcol=gcol, glast=glast, decay=decay, kb=k * beta))
    n = range(len(data))
    lower = [jnp.where(masks[x['d']][1], _bmm_nt(x['kb'], x['k']) * x['decay'], 0.0) for x in data]
    wmask = {d: _dn_masks(d, DN_PACK) for d in masks}
    unit = lax.broadcasted_iota(jnp.int32, (CHUNK, DN_PACK * CHUNK), 1) // CHUNK

    def wide(x):
        x4 = x.reshape(x.shape[0] // DN_PACK, DN_PACK, CHUNK, CHUNK)
        return jnp.concatenate([x4[:, j] for j in range(DN_PACK)], axis=-1)

    def block_diag(xw):
        return jnp.concatenate([jnp.where(unit == j, xw, 0.0) for j in range(DN_PACK)], axis=1)

    yield
    lower_w = [wide(lower[i]) for i in n]
    tinv = [wmask[data[i]['d']][2] - jnp.where(wmask[data[i]['d']][3][0], lower_w[i], 0.0) for i in n]
    for lvl in range(1, len(wmask[data[0]['d']][3])):
        te = [_bmm(tinv[i], block_diag(jnp.where(wmask[data[i]['d']][3][lvl], lower_w[i], 0.0))) for i in n]
        yield
        tinv = [tinv[i] - _bmm(te[i], block_diag(tinv[i])) for i in n]
        yield
    sol = []
    for i, x in enumerate(data):
        rhs = jnp.concatenate([x['v'] * x['beta'], x['kb'] * jnp.exp(x['gcol'])], axis=-1)
        units = rhs.shape[0]
        s_tall = _bmm(block_diag(tinv[i]), rhs.reshape(units // DN_PACK, DN_PACK * CHUNK, 2 * DN_HEAD))
        sol.append(s_tall.reshape(units, CHUNK, 2 * DN_HEAD))
    yield
    for i, x in enumerate(data):
        u, w = sol[i][..., :DN_HEAD], sol[i][..., DN_HEAD:]
        attn = (_bmm_nt(x['q'], x['k']) * x['decay']).astype(BF16)
        wq = jnp.concatenate([w, x['q'] * jnp.exp(x['gcol'])], axis=1).astype(BF16)
        kdec = (x['k'] * jnp.exp(x['glast'] - x['gcol'])).astype(BF16)
        out.append((u, wq, attn, kdec, jnp.exp(x['glast'])))


def _dn_scan(streams, slot, bufs, o_refs, s_ref, cb):
    u_buf, wq_buf, attn_buf, kdec_buf, eg_buf = bufs
    units = DN_STREAM_HEADS * cb
    for step in range(cb):
        chains = []
        for si, (d, hds, _) in enumerate(streams):
            c = step if d == 0 else cb - 1 - step
            chains += [(d, hd, c, si * units + i * cb + c) for i, hd in enumerate(hds)]
        states = [s_ref[d * DN_HEADS + hd] for d, hd, _, _ in chains]
        ws = [_mm(wq_buf[slot, n], st) for (_, _, _, n), st in zip(chains, states)]
        yield
        for (d, hd, c, n), st, w in zip(chains, states, ws):
            v_new = (u_buf[slot, n] - w[:CHUNK]).astype(BF16)
            o = w[CHUNK:] + _mm(attn_buf[slot, n], v_new)
            s_ref[d * DN_HEADS + hd] = st * eg_buf[slot, n] + lax.dot_general(
                kdec_buf[slot, n], v_new, (((0,), (0,)), ((), ())), preferred_element_type=F32)
            o_refs[d][c * CHUNK:(c + 1) * CHUNK, hd * DN_HEAD:(hd + 1) * DN_HEAD] = o.astype(o_refs[d].dtype)
        yield


def _dn_main_kernel(qf, kf, vf, gbf, grf, qb, kb, vb, gbb, grb, of_ref, ob_ref, s_ref,
                    u_buf, wq_buf, attn_buf, kdec_buf, eg_buf, *, cb):
    j = pl.program_id(1)
    bufs = (u_buf, wq_buf, attn_buf, kdec_buf, eg_buf)

    @pl.when(j == 0)
    def _():
        s_ref[...] = jnp.zeros_like(s_ref)
        for buf in bufs:
            buf[1] = jnp.zeros(buf.shape[1:], buf.dtype)

    cur = (j + 1) % 2
    nxt = j % 2
    groups = [tuple(range(h, h + DN_STREAM_HEADS)) for h in range(0, DN_HEADS, DN_STREAM_HEADS)]
    streams = ([(0, hds, (qf, kf, vf, gbf, grf)) for hds in groups]
               + [(1, hds, (qb, kb, vb, gbb, grb)) for hds in groups])
    units = DN_STREAM_HEADS * cb
    pre = []
    terms = _dn_chunk_terms(streams, cb, pre)
    scan = _dn_scan(streams, cur, bufs, (of_ref, ob_ref), s_ref, cb)
    live = [terms, scan]
    while live:
        for gen, reps in ((terms, 2), (scan, 1)):
            for _ in range(reps):
                if gen in live and next(gen, StopIteration) is StopIteration:
                    live.remove(gen)
    for si, (u, wq, attn, kdec, eg) in enumerate(pre):
        lo, hi = si * units, (si + 1) * units
        u_buf[nxt, lo:hi] = u
        wq_buf[nxt, lo:hi] = wq
        attn_buf[nxt, lo:hi] = attn
        kdec_buf[nxt, lo:hi] = kdec
        eg_buf[nxt, lo:hi] = jnp.broadcast_to(eg, (units, 1, DN_HEAD))


def _dn_main(qn, kn, vv, gb, gr, cb):
    b, t, _ = qn.shape
    cbt = cb * CHUNK
    ng = t // cbt
    fwd_in = lambda bi, j: (bi, jnp.minimum(j, ng - 1), 0)
    bwd_in = lambda bi, j: (bi, ng - 1 - jnp.minimum(j, ng - 1), 0)
    fwd_in4 = lambda bi, j: (bi, jnp.minimum(j, ng - 1), 0, 0)
    bwd_in4 = lambda bi, j: (bi, ng - 1 - jnp.minimum(j, ng - 1), 0, 0)
    fwd_out = lambda bi, j: (bi, jnp.maximum(j - 1, 0), 0)
    bwd_out = lambda bi, j: (bi, ng - 1 - jnp.maximum(j - 1, 0), 0)

    def specs(im3, im4):
        return ([pl.BlockSpec((None, cbt, DN_QK), im3)] * 3
                + [pl.BlockSpec((None, cbt, LANES), im3), pl.BlockSpec((None, cb, 2 * SUBLANES, CHUNK), im4)])

    units = N_DIR * DN_HEADS * cb
    kern = functools.partial(_dn_main_kernel, cb=cb)
    return pl.pallas_call(
        kern,
        grid=(b, ng + 1),
        in_specs=specs(fwd_in, fwd_in4) + specs(bwd_in, bwd_in4),
        out_specs=[pl.BlockSpec((None, cbt, DN_QK), fwd_out), pl.BlockSpec((None, cbt, DN_QK), bwd_out)],
        out_shape=[jax.ShapeDtypeStruct((b, t, DN_QK), BF16)] * 2,
        scratch_shapes=[pltpu.VMEM((N_DIR * DN_HEADS, DN_HEAD, DN_HEAD), F32),
                        pltpu.VMEM((2, units, CHUNK, DN_HEAD), F32),
                        pltpu.VMEM((2, units, 2 * CHUNK, DN_HEAD), BF16),
                        pltpu.VMEM((2, units, CHUNK, CHUNK), BF16),
                        pltpu.VMEM((2, units, CHUNK, DN_HEAD), BF16),
                        pltpu.VMEM((2, units, 1, DN_HEAD), F32)],
        compiler_params=_params(("parallel", "arbitrary")),
        name="dn_main",
    )(qn, kn, vv, gb, gr, qn, kn, vv, gb, gr)


def _rope_norm(x, w, seg, cos, sin):
    sq = x * x
    sq_hi = sq.astype(BF16)
    sq_lo = (sq - sq_hi.astype(F32)).astype(BF16)
    ss = jnp.dot(sq_hi, seg, preferred_element_type=F32) + jnp.dot(sq_lo, seg, preferred_element_type=F32)
    xn = x * lax.rsqrt(ss * (1.0 / ATT_HEAD_DIM) + EPS) * w
    quarter = ATT_HEAD_DIM // 4
    outs = []
    for t in range(x.shape[-1] // LANES):
        xt = xn[:, t * LANES:(t + 1) * LANES]
        lane = lax.broadcasted_iota(jnp.int32, xt.shape, 1)
        rot = jnp.where(lane % (2 * quarter) < quarter,
                        pltpu.roll(xt, LANES - quarter, axis=1), pltpu.roll(xt, quarter, axis=1))
        outs.append(xt * cos + rot * sin)
    return outs[0] if len(outs) == 1 else jnp.concatenate(outs, axis=-1)


def _attn_kernel(qt_ref, k_ref, vt_ref, o_ref, s_buf, p_buf, *, tk):
    m_cols = qt_ref.shape[1]
    qt = qt_ref[...]
    nk = k_ref.shape[0] // tk
    ones = jnp.ones((2 * SUBLANES, tk), BF16)

    def scores(j, slot):
        start = pl.multiple_of(j * tk, tk)
        s = jnp.dot(k_ref[pl.ds(start, tk), :], qt, preferred_element_type=F32)
        s_buf[slot] = s
        return jnp.max(jnp.max(s.reshape(tk // SUBLANES, SUBLANES, m_cols), axis=0), axis=0, keepdims=True)

    def softmax(slot, m, chunk_max):
        m_new = jnp.maximum(m, chunk_max)
        p_buf[slot] = jnp.exp2(s_buf[slot] - m_new).astype(BF16)
        return m_new, jnp.exp2(m - m_new)

    def values(j, slot, alpha, acc):
        start = pl.multiple_of(j * tk, tk)
        vt = jnp.concatenate([vt_ref[:, pl.ds(start, tk)], ones], axis=0)
        return alpha * acc + jnp.dot(vt, p_buf[slot], preferred_element_type=F32)

    m = jnp.full((1, m_cols), jnp.finfo(F32).min, F32)
    acc = jnp.zeros((ATT_HEAD_DIM + 2 * SUBLANES, m_cols), F32)
    cmax0 = scores(0, 0)
    cmax1 = scores(1, 1)
    m, alpha0 = softmax(0, m, cmax0)
    cmax2 = scores(2, 0)
    m, alpha1 = softmax(1, m, cmax1)
    acc = values(0, 0, alpha0, acc)

    def body(i, carry):
        m, cmax, alpha_prev, acc = carry
        j0 = 2 * i
        cmax1 = scores(j0 + 1, 1)
        m, alpha0 = softmax(0, m, cmax)
        acc = values(j0 - 1, 1, alpha_prev, acc)
        cmax2 = scores(j0 + 2, 0)
        m, alpha1 = softmax(1, m, cmax1)
        acc = values(j0, 0, alpha0, acc)
        return m, cmax2, alpha1, acc

    m, cmax, alpha_prev, acc = lax.fori_loop(1, nk // 2 - 1, body, (m, cmax2, alpha1, acc))
    cmax1 = scores(nk - 1, 1)
    m, alpha0 = softmax(0, m, cmax)
    acc = values(nk - 3, 1, alpha_prev, acc)
    m, alpha = softmax(1, m, cmax1)
    acc = values(nk - 2, 0, alpha0, acc)
    acc = values(nk - 1, 1, alpha, acc)
    o = acc[:ATT_HEAD_DIM] / acc[ATT_HEAD_DIM:ATT_HEAD_DIM + 1]
    tq = m_cols // ATT_GROUP
    o = jnp.concatenate([o[:, g * tq:(g + 1) * tq] for g in range(ATT_GROUP)], axis=0)
    o_ref[...] = o.T.astype(o_ref.dtype)


def _attention(qt5, k4, vt4, tk):
    b, _, nq, _, m_cols = qt5.shape
    t = k4.shape[2]
    tq = m_cols // ATT_GROUP
    assert t % (2 * tk) == 0 and t // tk >= 4
    kern = functools.partial(_attn_kernel, tk=tk)
    return pl.pallas_call(
        kern,
        grid=(b, ATT_KV_HEADS, nq),
        in_specs=[pl.BlockSpec((None, None, None, ATT_HEAD_DIM, m_cols), lambda bi, h, i: (bi, h, i, 0, 0)),
                  pl.BlockSpec((None, None, t, ATT_HEAD_DIM), lambda bi, h, i: (bi, h, 0, 0)),
                  pl.BlockSpec((None, None, ATT_HEAD_DIM, t), lambda bi, h, i: (bi, h, 0, 0))],
        out_specs=pl.BlockSpec((None, tq, ATT_GROUP * ATT_HEAD_DIM), lambda bi, h, i: (bi, i, h)),
        out_shape=jax.ShapeDtypeStruct((b, t, ATT_Q), BF16),
        scratch_shapes=[pltpu.VMEM((2, tk, m_cols), F32), pltpu.VMEM((2, tk, m_cols), BF16)],
        compiler_params=_params(("parallel", "parallel", "parallel")),
        name="attention",
    )(qt5, k4, vt4)


def _out_proj_kernel(of_ref, ob_ref, dnz_ref, dnw_ref, oat_ref, atz_ref, x_ref, w_ref, fw_ref, y_ref, *, final):
    o = of_ref[...].astype(F32) + ob_ref[...].astype(F32)
    parts = []
    for hd in range(DN_HEADS):
        oh = o[:, hd * DN_HEAD:(hd + 1) * DN_HEAD]
        parts.append(oh * lax.rsqrt(jnp.mean(oh * oh, axis=-1, keepdims=True) + EPS) * dnw_ref[...])
    y_dn = jnp.concatenate(parts, axis=-1) * _silu(dnz_ref[...].astype(F32))
    y_at = oat_ref[...].astype(F32) * _silu(atz_ref[...].astype(F32))
    y = jnp.concatenate([y_dn, y_at], axis=-1).astype(BF16)
    out = x_ref[...] + jnp.dot(y, w_ref[...], preferred_element_type=F32)
    if final:
        out = out * lax.rsqrt(jnp.mean(out * out, axis=-1, keepdims=True) + EPS) * fw_ref[...]
    y_ref[...] = out


def _out_proj(o_f, o_b, dnz, dnw_row, o_at, atz, x2, w_out, fw_row, tm, final):
    bt = x2.shape[0]
    row = lambda i: (i, 0)
    const = lambda i: (0, 0)
    kern = functools.partial(_out_proj_kernel, final=final)
    return pl.pallas_call(
        kern,
        grid=(bt // tm,),
        in_specs=[pl.BlockSpec((tm, DN_QK), row), pl.BlockSpec((tm, DN_QK), row), pl.BlockSpec((tm, DN_QK), row),
                  pl.BlockSpec((1, DN_HEAD), const),
                  pl.BlockSpec((tm, ATT_Q), row), pl.BlockSpec((tm, ATT_Q), row),
                  pl.BlockSpec((tm, D_MODEL), row),
                  pl.BlockSpec((D_MODEL, D_MODEL), const), pl.BlockSpec((1, D_MODEL), const)],
        out_specs=pl.BlockSpec((tm, D_MODEL), row),
        out_shape=jax.ShapeDtypeStruct((bt, D_MODEL), F32),
        compiler_params=_params(("parallel",)),
        name="out_proj_final" if final else "out_proj",
    )(o_f, o_b, dnz, dnw_row, o_at, atz, x2, w_out, fw_row)


def _pack_w_in(w):
    splits = np.cumsum(IN_SIZES)[:-1].tolist()
    qkv, dnz, bb, aa, atq, atk, atv, atz = jnp.split(w, splits, axis=-1)
    pad = jnp.zeros((w.shape[0], LANES - 2 * N_DIR * DN_HEADS), w.dtype)
    return jnp.concatenate([atq, atk, atv, qkv, dnz, atz, bb, aa, pad], axis=-1).astype(BF16)


def _rope_tables(t):
    half = ATT_HEAD_DIM // 2
    pos = np.arange(t)
    row = (pos // GRID_W).astype(np.float32)
    col = (pos % GRID_W).astype(np.float32)
    inv = jnp.asarray(ROPE_THETA, F32) ** (-jnp.arange(0, half, 2, dtype=F32) / half)
    ang_r = jnp.asarray(row)[:, None] * inv[None, :]
    ang_c = jnp.asarray(col)[:, None] * inv[None, :]
    cos = jnp.concatenate([jnp.cos(ang_r)] * 2 + [jnp.cos(ang_c)] * 2, axis=-1)
    sin = jnp.concatenate([-jnp.sin(ang_r), jnp.sin(ang_r), -jnp.sin(ang_c), jnp.sin(ang_c)], axis=-1)
    reps = LANES // ATT_HEAD_DIM
    return jnp.tile(cos, (1, reps)), jnp.tile(sin, (1, reps))


def _gate_row(p):
    return jnp.zeros((1, LANES), F32).at[0, DECAY_LANE:DECAY_LANE + N_DIR * DN_HEADS].set(p.reshape(-1))


def kernel(x, norm_w, w_in, conv_w, a_log, dt_bias, dn_norm_w, q_norm_w, k_norm_w, w_out, final_norm_w):
    b, t, d = x.shape
    depth = w_in.shape[0]
    tm = 256
    tm_proj = min(512, b * t)
    cb = min(4, t // CHUNK)
    tq = min(1024, t)
    tk = min(512, t // 4)
    cos_t, sin_t = _rope_tables(t)
    lane = np.arange(ATT_Q)
    seg = jnp.asarray((lane[:, None] // ATT_HEAD_DIM) == (lane[None, :] // ATT_HEAD_DIM), BF16)
    x2 = x.reshape(b * t, d)
    for l in range(depth):
        qkv, dnz, atz, ba, qt6, k4, vt4 = _in_proj(
            x2, norm_w[l][None, :], _pack_w_in(w_in[l]),
            jnp.tile(q_norm_w[l], ATT_HEADS)[None, :], jnp.tile(k_norm_w[l], ATT_KV_HEADS)[None, :],
            seg, cos_t, sin_t, b, t, tm_proj, tq)
        cw = jnp.zeros((SUBLANES, DN_CONV_DIM), F32).at[:CONV_K].set(conv_w[l])
        qn, kn, vv, gb = _dn_prep(qkv.reshape(b, t, -1), ba.reshape(b, t, -1), cw,
                                  _gate_row(a_log[l]), _gate_row(dt_bias[l]), tm)
        gr = gb[..., :2 * SUBLANES].reshape(b, t // CHUNK, CHUNK, 2 * SUBLANES).transpose(0, 1, 3, 2)
        o_f, o_b = _dn_main(qn, kn, vv, gb, gr, cb)
        qt5 = qt6.reshape(b, ATT_KV_HEADS, t // tq, ATT_HEAD_DIM, ATT_GROUP * tq)
        o_at = _attention(qt5, k4, vt4, tk).reshape(b * t, ATT_Q)
        x2 = _out_proj(o_f.reshape(b * t, -1), o_b.reshape(b * t, -1), dnz, dn_norm_w[l][None, :], o_at, atz,
                       x2, w_out[l].astype(BF16), final_norm_w[None, :], tm_proj, final=(l == depth - 1))
    return x2.reshape(b, t, d)
```

```python
import functools

import jax
import jax.numpy as jnp
import numpy as np
from jax import lax
from jax.experimental import pallas as pl
from jax.experimental.pallas import tpu as pltpu

D_MODEL = 1024
DN_HEADS = 4
DN_HEAD = 128
DN_QK = DN_HEADS * DN_HEAD
DN_CONV_DIM = 3 * DN_QK
CONV_K = 5
CHUNK = 128
N_DIR = 2
ATT_HEADS = 8
ATT_KV_HEADS = 2
ATT_GROUP = ATT_HEADS // ATT_KV_HEADS
ATT_HEAD_DIM = 64
ATT_Q = ATT_HEADS * ATT_HEAD_DIM
ATT_KV = ATT_KV_HEADS * ATT_HEAD_DIM
ROPE_THETA = 10000.0
GRID_W = 64
EPS = 1e-6
IN_SIZES = (DN_CONV_DIM, DN_QK, N_DIR * DN_HEADS, N_DIR * DN_HEADS, ATT_Q, ATT_KV, ATT_KV, ATT_Q)
LANES = 128
SUBLANES = 8
VMEM_LIMIT = 48 * 1024 * 1024
PACK_WIDTHS = (ATT_Q, ATT_KV, ATT_KV, DN_CONV_DIM, DN_QK, ATT_Q, LANES)
PACK_ATT_COLS = ATT_Q + 2 * ATT_KV
PACK_COLS = sum(PACK_WIDTHS)
BETA_LANE = 0
DECAY_LANE = N_DIR * DN_HEADS
DN_STREAM_HEADS = 4
MXU_WIDTH = 256
DN_PACK = MXU_WIDTH // CHUNK

F32 = jnp.float32
BF16 = jnp.bfloat16
HI = lax.Precision.HIGHEST
LOG2E = 1.4426950408889634


def _params(sem):
    return pltpu.CompilerParams(dimension_semantics=sem, vmem_limit_bytes=VMEM_LIMIT)


def _silu(x):
    return x / (1.0 + jnp.exp(-x))


def _in_proj_kernel(x_ref, nw_ref, w_ref, qw_ref, kw_ref, seg_ref, cos_ref, sin_ref,
                    qkv_ref, dnz_ref, atz_ref, ba_ref, qo_ref, ko_ref, vo_ref):
    x = x_ref[...]
    h = x * lax.rsqrt(jnp.mean(x * x, axis=-1, keepdims=True) + EPS) * nw_ref[...]
    hb = h.astype(BF16)
    p_att = jnp.dot(hb, w_ref[:, :PACK_ATT_COLS], preferred_element_type=F32)
    atq, atk, atv = p_att[:, :ATT_Q], p_att[:, ATT_Q:ATT_Q + ATT_KV], p_att[:, ATT_Q + ATT_KV:]
    cos = cos_ref[...]
    sin = sin_ref[...]
    q = _rope_norm(atq, qw_ref[...], seg_ref[...], cos, sin)
    k = _rope_norm(atk, kw_ref[...], seg_ref[0:LANES, 0:LANES], cos, sin).astype(BF16)
    p_rest = jnp.dot(hb, w_ref[:, PACK_ATT_COLS:], preferred_element_type=F32)
    qt = (q * (LOG2E * ATT_HEAD_DIM ** -0.5)).T.astype(BF16)
    for hd in range(ATT_HEADS):
        qo_ref[hd // ATT_GROUP, hd % ATT_GROUP] = qt[hd * ATT_HEAD_DIM:(hd + 1) * ATT_HEAD_DIM]
    vt = atv.T.astype(BF16)
    for kvh in range(ATT_KV_HEADS):
        ko_ref[kvh] = k[:, kvh * ATT_HEAD_DIM:(kvh + 1) * ATT_HEAD_DIM]
        vo_ref[kvh] = vt[kvh * ATT_HEAD_DIM:(kvh + 1) * ATT_HEAD_DIM]
    o = 0
    for ref in (qkv_ref, dnz_ref, atz_ref, ba_ref):
        n = ref.shape[-1]
        ref[...] = p_rest[:, o:o + n].astype(ref.dtype)
        o += n


def _in_proj(x2, nw, w_packed, qw_row, kw_row, seg, cos_t, sin_t, b, t, tm, tq):
    bt = x2.shape[0]
    nr = t // tm
    per_q = tq // tm
    row = lambda i: (i, 0)
    const = lambda i: (0, 0)
    tab = lambda i: (i % nr, 0)
    return pl.pallas_call(
        _in_proj_kernel,
        grid=(bt // tm,),
        in_specs=[pl.BlockSpec((tm, D_MODEL), row), pl.BlockSpec((1, D_MODEL), const),
                  pl.BlockSpec((D_MODEL, PACK_COLS), const),
                  pl.BlockSpec((1, ATT_Q), const), pl.BlockSpec((1, ATT_KV), const), pl.BlockSpec((ATT_Q, ATT_Q), const),
                  pl.BlockSpec((tm, LANES), tab), pl.BlockSpec((tm, LANES), tab)],
        out_specs=[pl.BlockSpec((tm, DN_CONV_DIM), row), pl.BlockSpec((tm, DN_QK), row), pl.BlockSpec((tm, ATT_Q), row),
                   pl.BlockSpec((tm, LANES), row),
                   pl.BlockSpec((None, ATT_KV_HEADS, None, ATT_GROUP, ATT_HEAD_DIM, tm),
                                lambda i: (i // nr, 0, (i % nr) // per_q, 0, 0, (i % nr) % per_q)),
                   pl.BlockSpec((None, ATT_KV_HEADS, tm, ATT_HEAD_DIM), lambda i: (i // nr, 0, i % nr, 0)),
                   pl.BlockSpec((None, ATT_KV_HEADS, ATT_HEAD_DIM, tm), lambda i: (i // nr, 0, 0, i % nr))],
        out_shape=[jax.ShapeDtypeStruct((bt, DN_CONV_DIM), F32), jax.ShapeDtypeStruct((bt, DN_QK), BF16),
                   jax.ShapeDtypeStruct((bt, ATT_Q), BF16), jax.ShapeDtypeStruct((bt, LANES), F32),
                   jax.ShapeDtypeStruct((b, ATT_KV_HEADS, t // tq, ATT_GROUP, ATT_HEAD_DIM, tq), BF16),
                   jax.ShapeDtypeStruct((b, ATT_KV_HEADS, t, ATT_HEAD_DIM), BF16),
                   jax.ShapeDtypeStruct((b, ATT_KV_HEADS, ATT_HEAD_DIM, t), BF16)],
        compiler_params=_params(("parallel",)),
        name="in_proj",
    )(x2, nw, w_packed, qw_row, kw_row, seg, cos_t, sin_t)


def _dn_prep_kernel(x_ref, prev_ref, next_ref, ba_ref, cw_ref, alog_ref, dtb_ref,
                    q_ref, k_ref, v_ref, gb_ref, xe_ref, *, tm):
    i = pl.program_id(1)
    n = pl.num_programs(1)
    halo = SUBLANES
    xe_ref[0:halo, :] = jnp.where(i > 0, prev_ref[...], 0.0)
    xe_ref[halo:halo + tm, :] = x_ref[...]
    xe_ref[halo + tm:2 * halo + tm, :] = jnp.where(i < n - 1, next_ref[...], 0.0)
    pad = CONV_K // 2
    y = jnp.zeros((tm, DN_CONV_DIM), F32)
    for j in range(CONV_K):
        y = y + xe_ref[halo + j - pad:halo + j - pad + tm, :] * cw_ref[j:j + 1, :]
    y = _silu(y)
    for hd in range(DN_HEADS):
        lo = hd * DN_HEAD
        q = y[:, lo:lo + DN_HEAD]
        k = y[:, DN_QK + lo:DN_QK + lo + DN_HEAD]
        q_ref[:, lo:lo + DN_HEAD] = q * lax.rsqrt(jnp.sum(q * q, axis=-1, keepdims=True) + EPS) * (DN_HEAD ** -0.5)
        k_ref[:, lo:lo + DN_HEAD] = k * lax.rsqrt(jnp.sum(k * k, axis=-1, keepdims=True) + EPS)
    v_ref[...] = y[:, 2 * DN_QK:]

    ba = ba_ref[...]
    beta = 1.0 / (1.0 + jnp.exp(-ba))
    z = ba + dtb_ref[...]
    softplus = jnp.maximum(z, 0.0) + jnp.log(1.0 + jnp.exp(-jnp.abs(z)))
    g = -jnp.exp(alog_ref[...]) * softplus
    r = lax.broadcasted_iota(jnp.int32, (tm, tm), 0)
    c = lax.broadcasted_iota(jnp.int32, (tm, tm), 1)
    same = (r // CHUNK) == (c // CHUNK)
    tri_f = jnp.where(same & (c <= r), 1.0, 0.0).astype(F32)
    tri_b = jnp.where(same & (c >= r), 1.0, 0.0).astype(F32)
    gc_f = jnp.dot(tri_f, g, precision=HI, preferred_element_type=F32)
    gc_b = jnp.dot(tri_b, g, precision=HI, preferred_element_type=F32)
    lane = lax.broadcasted_iota(jnp.int32, (tm, LANES), 1)
    gc = jnp.where(lane < DECAY_LANE + DN_HEADS, gc_f, gc_b)
    gb_ref[...] = jnp.where(lane < DECAY_LANE, beta, gc)


def _dn_prep(qkv, ba, cw, alog_row, dtb_row, tm):
    b, t, _ = qkv.shape
    nb = tm // SUBLANES
    last = t // SUBLANES - 1
    kern = functools.partial(_dn_prep_kernel, tm=tm)
    return pl.pallas_call(
        kern,
        grid=(b, t // tm),
        in_specs=[pl.BlockSpec((None, tm, DN_CONV_DIM), lambda bi, i: (bi, i, 0)),
                  pl.BlockSpec((None, SUBLANES, DN_CONV_DIM), lambda bi, i: (bi, jnp.maximum(i * nb - 1, 0), 0)),
                  pl.BlockSpec((None, SUBLANES, DN_CONV_DIM), lambda bi, i: (bi, jnp.minimum((i + 1) * nb, last), 0)),
                  pl.BlockSpec((None, tm, LANES), lambda bi, i: (bi, i, 0)),
                  pl.BlockSpec((SUBLANES, DN_CONV_DIM), lambda bi, i: (0, 0)),
                  pl.BlockSpec((1, LANES), lambda bi, i: (0, 0)),
                  pl.BlockSpec((1, LANES), lambda bi, i: (0, 0))],
        out_specs=[pl.BlockSpec((None, tm, DN_QK), lambda bi, i: (bi, i, 0))] * 3
        + [pl.BlockSpec((None, tm, LANES), lambda bi, i: (bi, i, 0))],
        out_shape=[jax.ShapeDtypeStruct((b, t, DN_QK), F32)] * 3 + [jax.ShapeDtypeStruct((b, t, LANES), F32)],
        scratch_shapes=[pltpu.VMEM((tm + 2 * SUBLANES, DN_CONV_DIM), F32)],
        compiler_params=_params(("parallel", "parallel")),
        name="dn_prep",
    )(qkv, qkv, qkv, ba, cw, alog_row, dtb_row)


def _mm(a, b):
    return jnp.dot(a.astype(BF16), b.astype(BF16), preferred_element_type=F32)


def _bmm(a, b):
    return jnp.einsum('cij,cjk->cik', a.astype(BF16), b.astype(BF16), preferred_element_type=F32)


def _bmm_nt(a, b):
    return jnp.einsum('cid,cjd->cij', a.astype(BF16), b.astype(BF16), preferred_element_type=F32)


def _dn_masks(d, pack=1):
    ii = lax.broadcasted_iota(jnp.int32, (CHUNK, pack * CHUNK), 0)
    jj = lax.broadcasted_iota(jnp.int32, (CHUNK, pack * CHUNK), 1) % CHUNK
    if d == 1:
        ii, jj = jj, ii
    eye = jnp.where(ii == jj, 1.0, 0.0).astype(F32)
    merge = []
    s = 1
    while s < CHUNK:
        merge.append((ii // (2 * s) == jj // (2 * s)) & ((ii // s) % 2 == 1) & ((jj // s) % 2 == 0))
        s *= 2
    return ii >= jj, ii > jj, eye, merge


def _dn_chunk_terms(streams, cb, out):
    masks = {d: _dn_masks(d) for d in sorted({s[0] for s in streams})}
    data = []
    for d, hds, (q_ref, k_ref, v_ref, gb_ref, gr_ref) in streams:
        gb = gb_ref[...]
        gr = gr_ref[...]

        def heads(ref):
            return jnp.concatenate([ref[:, hd * DN_HEAD:(hd + 1) * DN_HEAD].reshape(cb, CHUNK, DN_HEAD)
                                    for hd in hds], axis=0)

        def gate_cols(lane0):
            return jnp.concatenate([gb[:, lane0 + hd:lane0 + hd + 1].reshape(cb, CHUNK, 1) for hd in hds], axis=0)

        q, k, v = heads(q_ref), heads(k_ref), heads(v_ref)
        beta = gate_cols(BETA_LANE + d * DN_HEADS)
        gcol = gate_cols(DECAY_LANE + d * DN_HEADS)
        gl = DECAY_LANE + d * DN_HEADS
        grow = jnp.concatenate([gr[:, gl + hd:gl + hd + 1, :] for hd in hds], axis=0)
        glast = grow[:, :, CHUNK - 1:CHUNK] if d == 0 else grow[:, :, 0:1]
        decay = jnp.where(masks[d][0], jnp.exp(jnp.minimum(gcol - grow, 0.0)), 0.0)
        data.append(dict(d=d, q=q, k=k, v=v, beta=beta, gcol=gcol, glast=glast, decay=decay, kb=k * beta))
    n = range(len(data))
    lower = [jnp.where(masks[x['d']][1], _bmm_nt(x['kb'], x['k']) * x['decay'], 0.0) for x in data]
    wmask = {d: _dn_masks(d, DN_PACK) for d in masks}
    unit = lax.broadcasted_iota(jnp.int32, (CHUNK, DN_PACK * CHUNK), 1) // CHUNK

    def wide(x):
        x4 = x.reshape(x.shape[0] // DN_PACK, DN_PACK, CHUNK, CHUNK)
        return jnp.concatenate([x4[:, j] for j in range(DN_PACK)], axis=-1)

    def block_diag(xw):
        return jnp.concatenate([jnp.where(unit == j, xw, 0.0) for j in range(DN_PACK)], axis=1)

    yield
    lower_w = [wide(lower[i]) for i in n]
    tinv = [wmask[data[i]['d']][2] - jnp.where(wmask[data[i]['d']][3][0], lower_w[i], 0.0) for i in n]
    for lvl in range(1, len(wmask[data[0]['d']][3])):
        te = [_bmm(tinv[i], block_diag(jnp.where(wmask[data[i]['d']][3][lvl], lower_w[i], 0.0))) for i in n]
        yield
        tinv = [tinv[i] - _bmm(te[i], block_diag(tinv[i])) for i in n]
        yield
    sol = []
    for i, x in enumerate(data):
        rhs = jnp.concatenate([x['v'] * x['beta'], x['kb'] * jnp.exp(x['gcol'])], axis=-1)
        units = rhs.shape[0]
        s_tall = _bmm(block_diag(tinv[i]), rhs.reshape(units // DN_PACK, DN_PACK * CHUNK, 2 * DN_HEAD))
        sol.append(s_tall.reshape(units, CHUNK, 2 * DN_HEAD))
    yield
    for i, x in enumerate(data):
        u, w = sol[i][..., :DN_HEAD], sol[i][..., DN_HEAD:]
        attn = (_bmm_nt(x['q'], x['k']) * x['decay']).astype(BF16)
        wq = jnp.concatenate([w, x['q'] * jnp.exp(x['gcol'])], axis=1).astype(BF16)
        kdec = (x['k'] * jnp.exp(x['glast'] - x['gcol'])).astype(BF16)
        out.append((u, wq, attn, kdec, jnp.exp(x['glast'])))


def _dn_scan(streams, slot, bufs, o_refs, s_ref, cb):
    u_buf, wq_buf, attn_buf, kdec_buf, eg_buf = bufs
    units = DN_STREAM_HEADS * cb
    for step in range(cb):
        chains = []
        for si, (d, hds, _) in enumerate(streams):
            c = step if d == 0 else cb - 1 - step
            chains += [(d, hd, c, si * units + i * cb + c) for i, hd in enumerate(hds)]
        states = [s_ref[d * DN_HEADS + hd] for d, hd, _, _ in chains]
        ws = [_mm(wq_buf[slot, n], st) for (_, _, _, n), st in zip(chains, states)]
        yield
        for (d, hd, c, n), st, w in zip(chains, states, ws):
            v_new = (u_buf[slot, n] - w[:CHUNK]).astype(BF16)
            o = w[CHUNK:] + _mm(attn_buf[slot, n], v_new)
            s_ref[d * DN_HEADS + hd] = st * eg_buf[slot, n] + lax.dot_general(
                kdec_buf[slot, n], v_new, (((0,), (0,)), ((), ())), preferred_element_type=F32)
            o_refs[d][c * CHUNK:(c + 1) * CHUNK, hd * DN_HEAD:(hd + 1) * DN_HEAD] = o.astype(o_refs[d].dtype)
        yield


def _dn_main_kernel(qf, kf, vf, gbf, grf, qb, kb, vb, gbb, grb, of_ref, ob_ref, s_ref,
                    u_buf, wq_buf, attn_buf, kdec_buf, eg_buf, *, cb):
    j = pl.program_id(1)
    bufs = (u_buf, wq_buf, attn_buf, kdec_buf, eg_buf)

    @pl.when(j == 0)
    def _():
        s_ref[...] = jnp.zeros_like(s_ref)
        for buf in bufs:
            buf[1] = jnp.zeros(buf.shape[1:], buf.dtype)

    cur = (j + 1) % 2
    nxt = j % 2
    groups = [tuple(range(h, h + DN_STREAM_HEADS)) for h in range(0, DN_HEADS, DN_STREAM_HEADS)]
    streams = ([(0, hds, (qf, kf, vf, gbf, grf)) for hds in groups]
               + [(1, hds, (qb, kb, vb, gbb, grb)) for hds in groups])
    units = DN_STREAM_HEADS * cb
    pre = []
    terms = _dn_chunk_terms(streams, cb, pre)
    scan = _dn_scan(streams, cur, bufs, (of_ref, ob_ref), s_ref, cb)
    live = [terms, scan]
    while live:
        for gen, reps in ((terms, 2), (scan, 1)):
            for _ in range(reps):
                if gen in live and next(gen, StopIteration) is StopIteration:
                    live.remove(gen)
    for si, (u, wq, attn, kdec, eg) in enumerate(pre):
        lo, hi = si * units, (si + 1) * units
        u_buf[nxt, lo:hi] = u
        wq_buf[nxt, lo:hi] = wq
        attn_buf[nxt, lo:hi] = attn
        kdec_buf[nxt, lo:hi] = kdec
        eg_buf[nxt, lo:hi] = jnp.broadcast_to(eg, (units, 1, DN_HEAD))


def _dn_main(qn, kn, vv, gb, gr, cb):
    b, t, _ = qn.shape
    cbt = cb * CHUNK
    ng = t // cbt
    fwd_in = lambda bi, j: (bi, jnp.minimum(j, ng - 1), 0)
    bwd_in = lambda bi, j: (bi, ng - 1 - jnp.minimum(j, ng - 1), 0)
    fwd_in4 = lambda bi, j: (bi, jnp.minimum(j, ng - 1), 0, 0)
    bwd_in4 = lambda bi, j: (bi, ng - 1 - jnp.minimum(j, ng - 1), 0, 0)
    fwd_out = lambda bi, j: (bi, jnp.maximum(j - 1, 0), 0)
    bwd_out = lambda bi, j: (bi, ng - 1 - jnp.maximum(j - 1, 0), 0)

    def specs(im3, im4):
        return ([pl.BlockSpec((None, cbt, DN_QK), im3)] * 3
                + [pl.BlockSpec((None, cbt, LANES), im3), pl.BlockSpec((None, cb, 2 * SUBLANES, CHUNK), im4)])

    units = N_DIR * DN_HEADS * cb
    kern = functools.partial(_dn_main_kernel, cb=cb)
    return pl.pallas_call(
        kern,
        grid=(b, ng + 1),
        in_specs=specs(fwd_in, fwd_in4) + specs(bwd_in, bwd_in4),
        out_specs=[pl.BlockSpec((None, cbt, DN_QK), fwd_out), pl.BlockSpec((None, cbt, DN_QK), bwd_out)],
        out_shape=[jax.ShapeDtypeStruct((b, t, DN_QK), BF16)] * 2,
        scratch_shapes=[pltpu.VMEM((N_DIR * DN_HEADS, DN_HEAD, DN_HEAD), F32),
                        pltpu.VMEM((2, units, CHUNK, DN_HEAD), F32),
                        pltpu.VMEM((2, units, 2 * CHUNK, DN_HEAD), BF16),
                        pltpu.VMEM((2, units, CHUNK, CHUNK), BF16),
                        pltpu.VMEM((2, units, CHUNK, DN_HEAD), BF16),
                        pltpu.VMEM((2, units, 1, DN_HEAD), F32)],
        compiler_params=_params(("parallel", "arbitrary")),
        name="dn_main",
    )(qn, kn, vv, gb, gr, qn, kn, vv, gb, gr)


def _rope_norm(x, w, seg, cos, sin):
    sq = x * x
    sq_hi = sq.astype(BF16)
    sq_lo = (sq - sq_hi.astype(F32)).astype(BF16)
    ss = jnp.dot(sq_hi, seg, preferred_element_type=F32) + jnp.dot(sq_lo, seg, preferred_element_type=F32)
    xn = x * lax.rsqrt(ss * (1.0 / ATT_HEAD_DIM) + EPS) * w
    quarter = ATT_HEAD_DIM // 4
    outs = []
    for t in range(x.shape[-1] // LANES):
        xt = xn[:, t * LANES:(t + 1) * LANES]
        lane = lax.broadcasted_iota(jnp.int32, xt.shape, 1)
        rot = jnp.where(lane % (2 * quarter) < quarter,
                        pltpu.roll(xt, LANES - quarter, axis=1), pltpu.roll(xt, quarter, axis=1))
        outs.append(xt * cos + rot * sin)
    return outs[0] if len(outs) == 1 else jnp.concatenate(outs, axis=-1)


def _attn_kernel(qt_ref, k_ref, vt_ref, o_ref, s_buf, p_buf, *, tk):
    qt = jnp.concatenate([qt_ref[g] for g in range(ATT_GROUP)], axis=1)
    m_cols = qt.shape[1]
    nk = k_ref.shape[0] // tk
    ones = jnp.ones((2 * SUBLANES, tk), BF16)

    def scores(j, slot):
        start = pl.multiple_of(j * tk, tk)
        s = jnp.dot(k_ref[pl.ds(start, tk), :], qt, preferred_element_type=F32)
        s_buf[slot] = s
        return jnp.max(jnp.max(s.reshape(tk // SUBLANES, SUBLANES, m_cols), axis=0), axis=0, keepdims=True)

    def softmax(slot, m, chunk_max):
        m_new = jnp.maximum(m, chunk_max)
        p_buf[slot] = jnp.exp2(s_buf[slot] - m_new).astype(BF16)
        return m_new, jnp.exp2(m - m_new)

    def values(j, slot, alpha, acc):
        start = pl.multiple_of(j * tk, tk)
        vt = jnp.concatenate([vt_ref[:, pl.ds(start, tk)], ones], axis=0)
        return alpha * acc + jnp.dot(vt, p_buf[slot], preferred_element_type=F32)

    m = jnp.full((1, m_cols), jnp.finfo(F32).min, F32)
    acc = jnp.zeros((ATT_HEAD_DIM + 2 * SUBLANES, m_cols), F32)
    cmax0 = scores(0, 0)
    cmax1 = scores(1, 1)
    m, alpha0 = softmax(0, m, cmax0)
    cmax2 = scores(2, 0)
    m, alpha1 = softmax(1, m, cmax1)
    acc = values(0, 0, alpha0, acc)

    def body(i, carry):
        m, cmax, alpha_prev, acc = carry
        j0 = 2 * i
        cmax1 = scores(j0 + 1, 1)
        m, alpha0 = softmax(0, m, cmax)
        acc = values(j0 - 1, 1, alpha_prev, acc)
        cmax2 = scores(j0 + 2, 0)
        m, alpha1 = softmax(1, m, cmax1)
        acc = values(j0, 0, alpha0, acc)
        return m, cmax2, alpha1, acc

    m, cmax, alpha_prev, acc = lax.fori_loop(1, nk // 2 - 1, body, (m, cmax2, alpha1, acc))
    cmax1 = scores(nk - 1, 1)
    m, alpha0 = softmax(0, m, cmax)
    acc = values(nk - 3, 1, alpha_prev, acc)
    m, alpha = softmax(1, m, cmax1)
    acc = values(nk - 2, 0, alpha0, acc)
    acc = values(nk - 1, 1, alpha, acc)
    o = acc[:ATT_HEAD_DIM] / acc[ATT_HEAD_DIM:ATT_HEAD_DIM + 1]
    tq = m_cols // ATT_GROUP
    o = jnp.concatenate([o[:, g * tq:(g + 1) * tq] for g in range(ATT_GROUP)], axis=0)
    o_ref[...] = o.T.astype(o_ref.dtype)


def _attention(qt6, k4, vt4, tk):
    b, _, nq, _, _, tq = qt6.shape
    t = k4.shape[2]
    m_cols = ATT_GROUP * tq
    assert t % (2 * tk) == 0 and t // tk >= 4
    kern = functools.partial(_attn_kernel, tk=tk)
    return pl.pallas_call(
        kern,
        grid=(b, ATT_KV_HEADS, nq),
        in_specs=[pl.BlockSpec((None, None, None, ATT_GROUP, ATT_HEAD_DIM, tq), lambda bi, h, i: (bi, h, i, 0, 0, 0)),
                  pl.BlockSpec((None, None, t, ATT_HEAD_DIM), lambda bi, h, i: (bi, h, 0, 0)),
                  pl.BlockSpec((None, None, ATT_HEAD_DIM, t), lambda bi, h, i: (bi, h, 0, 0))],
        out_specs=pl.BlockSpec((None, tq, ATT_GROUP * ATT_HEAD_DIM), lambda bi, h, i: (bi, i, h)),
        out_shape=jax.ShapeDtypeStruct((b, t, ATT_Q), BF16),
        scratch_shapes=[pltpu.VMEM((2, tk, m_cols), F32), pltpu.VMEM((2, tk, m_cols), BF16)],
        compiler_params=_params(("parallel", "parallel", "parallel")),
        name="attention",
    )(qt6, k4, vt4)


def _out_proj_kernel(of_ref, ob_ref, dnz_ref, dnw_ref, oat_ref, atz_ref, x_ref, w_ref, fw_ref, y_ref, *, final):
    o = of_ref[...].astype(F32) + ob_ref[...].astype(F32)
    parts = []
    for hd in range(DN_HEADS):
        oh = o[:, hd * DN_HEAD:(hd + 1) * DN_HEAD]
        parts.append(oh * lax.rsqrt(jnp.mean(oh * oh, axis=-1, keepdims=True) + EPS) * dnw_ref[...])
    y_dn = jnp.concatenate(parts, axis=-1) * _silu(dnz_ref[...].astype(F32))
    y_at = oat_ref[...].astype(F32) * _silu(atz_ref[...].astype(F32))
    y = jnp.concatenate([y_dn, y_at], axis=-1).astype(BF16)
    out = x_ref[...] + jnp.dot(y, w_ref[...], preferred_element_type=F32)
    if final:
        out = out * lax.rsqrt(jnp.mean(out * out, axis=-1, keepdims=True) + EPS) * fw_ref[...]
    y_ref[...] = out


def _out_proj(o_f, o_b, dnz, dnw_row, o_at, atz, x2, w_out, fw_row, tm, final):
    bt = x2.shape[0]
    row = lambda i: (i, 0)
    const = lambda i: (0, 0)
    kern = functools.partial(_out_proj_kernel, final=final)
    return pl.pallas_call(
        kern,
        grid=(bt // tm,),
        in_specs=[pl.BlockSpec((tm, DN_QK), row), pl.BlockSpec((tm, DN_QK), row), pl.BlockSpec((tm, DN_QK), row),
                  pl.BlockSpec((1, DN_HEAD), const),
                  pl.BlockSpec((tm, ATT_Q), row), pl.BlockSpec((tm, ATT_Q), row),
                  pl.BlockSpec((tm, D_MODEL), row),
                  pl.BlockSpec((D_MODEL, D_MODEL), const), pl.BlockSpec((1, D_MODEL), const)],
        out_specs=pl.BlockSpec((tm, D_MODEL), row),
        out_shape=jax.ShapeDtypeStruct((bt, D_MODEL), F32),
        compiler_params=_params(("parallel",)),
        name="out_proj_final" if final else "out_proj",
    )(o_f, o_b, dnz, dnw_row, o_at, atz, x2, w_out, fw_row)


def _pack_w_in(w):
    splits = np.cumsum(IN_SIZES)[:-1].tolist()
    qkv, dnz, bb, aa, atq, atk, atv, atz = jnp.split(w, splits, axis=-1)
    pad = jnp.zeros((w.shape[0], LANES - 2 * N_DIR * DN_HEADS), w.dtype)
    return jnp.concatenate([atq, atk, atv, qkv, dnz, atz, bb, aa, pad], axis=-1).astype(BF16)


def _rope_tables(t):
    half = ATT_HEAD_DIM // 2
    pos = np.arange(t)
    row = (pos // GRID_W).astype(np.float32)
    col = (pos % GRID_W).astype(np.float32)
    inv = jnp.asarray(ROPE_THETA, F32) ** (-jnp.arange(0, half, 2, dtype=F32) / half)
    ang_r = jnp.asarray(row)[:, None] * inv[None, :]
    ang_c = jnp.asarray(col)[:, None] * inv[None, :]
    cos = jnp.concatenate([jnp.cos(ang_r)] * 2 + [jnp.cos(ang_c)] * 2, axis=-1)
    sin = jnp.concatenate([-jnp.sin(ang_r), jnp.sin(ang_r), -jnp.sin(ang_c), jnp.sin(ang_c)], axis=-1)
    reps = LANES // ATT_HEAD_DIM
    return jnp.tile(cos, (1, reps)), jnp.tile(sin, (1, reps))


def _gate_row(p):
    return jnp.zeros((1, LANES), F32).at[0, DECAY_LANE:DECAY_LANE + N_DIR * DN_HEADS].set(p.reshape(-1))


def kernel(x, norm_w, w_in, conv_w, a_log, dt_bias, dn_norm_w, q_norm_w, k_norm_w, w_out, final_norm_w):
    b, t, d = x.shape
    depth = w_in.shape[0]
    tm = 256
    tm_proj = min(512, b * t)
    cb = min(4, t // CHUNK)
    tq = min(1024, t)
    tk = min(512, t // 4)
    cos_t, sin_t = _rope_tables(t)
    lane = np.arange(ATT_Q)
    seg = jnp.asarray((lane[:, None] // ATT_HEAD_DIM) == (lane[None, :] // ATT_HEAD_DIM), BF16)
    x2 = x.reshape(b * t, d)
    for l in range(depth):
        qkv, dnz, atz, ba, qt6, k4, vt4 = _in_proj(
            x2, norm_w[l][None, :], _pack_w_in(w_in[l]),
            jnp.tile(q_norm_w[l], ATT_HEADS)[None, :], jnp.tile(k_norm_w[l], ATT_KV_HEADS)[None, :],
            seg, cos_t, sin_t, b, t, tm_proj, tq)
        cw = jnp.zeros((SUBLANES, DN_CONV_DIM), F32).at[:CONV_K].set(conv_w[l])
        qn, kn, vv, gb = _dn_prep(qkv.reshape(b, t, -1), ba.reshape(b, t, -1), cw,
                                  _gate_row(a_log[l]), _gate_row(dt_bias[l]), tm)
        gr = gb[..., :2 * SUBLANES].reshape(b, t // CHUNK, CHUNK, 2 * SUBLANES).transpose(0, 1, 3, 2)
        o_f, o_b = _dn_main(qn, kn, vv, gb, gr, cb)
        o_at = _attention(qt6, k4, vt4, tk).reshape(b * t, ATT_Q)
        x2 = _out_proj(o_f.reshape(b * t, -1), o_b.reshape(b * t, -1), dnz, dn_norm_w[l][None, :], o_at, atz,
                       x2, w_out[l].astype(BF16), final_norm_w[None, :], tm_proj, final=(l == depth - 1))
    return x2.reshape(b, t, d)
```

```python
import functools

import jax
import jax.numpy as jnp
import numpy as np
from jax import lax
from jax.experimental import pallas as pl
from jax.experimental.pallas import tpu as pltpu

D_MODEL = 1024
DN_HEADS = 4
DN_HEAD = 128
DN_QK = DN_HEADS * DN_HEAD
DN_CONV_DIM = 3 * DN_QK
CONV_K = 5
CHUNK = 128
N_DIR = 2
ATT_HEADS = 8
ATT_KV_HEADS = 2
ATT_GROUP = ATT_HEADS // ATT_KV_HEADS
ATT_HEAD_DIM = 64
ATT_Q = ATT_HEADS * ATT_HEAD_DIM
ATT_KV = ATT_KV_HEADS * ATT_HEAD_DIM
ROPE_THETA = 10000.0
GRID_W = 64
EPS = 1e-6
IN_SIZES = (DN_CONV_DIM, DN_QK, N_DIR * DN_HEADS, N_DIR * DN_HEADS, ATT_Q, ATT_KV, ATT_KV, ATT_Q)
LANES = 128
SUBLANES = 8
VMEM_LIMIT = 48 * 1024 * 1024
PACK_WIDTHS = (ATT_Q, ATT_KV, ATT_KV, DN_CONV_DIM, DN_QK, ATT_Q, LANES)
PACK_ATT_COLS = ATT_Q + 2 * ATT_KV
PACK_COLS = sum(PACK_WIDTHS)
BETA_LANE = 0
DECAY_LANE = N_DIR * DN_HEADS
DN_STREAM_HEADS = 4
MXU_WIDTH = 256
DN_PACK = MXU_WIDTH // CHUNK

F32 = jnp.float32
BF16 = jnp.bfloat16
HI = lax.Precision.HIGHEST
LOG2E = 1.4426950408889634


def _params(sem):
    return pltpu.CompilerParams(dimension_semantics=sem, vmem_limit_bytes=VMEM_LIMIT)


def _silu(x):
    return x / (1.0 + jnp.exp(-x))


def _in_proj_kernel(x_ref, nw_ref, w_ref, qw_ref, kw_ref, seg_ref, cos_ref, sin_ref,
                    qkv_ref, dnz_ref, atz_ref, ba_ref, qo_ref, ko_ref, vo_ref):
    x = x_ref[...]
    h = x * lax.rsqrt(jnp.mean(x * x, axis=-1, keepdims=True) + EPS) * nw_ref[...]
    hb = h.astype(BF16)
    p_att = jnp.dot(hb, w_ref[:, :PACK_ATT_COLS], preferred_element_type=F32)
    atq, atk, atv = p_att[:, :ATT_Q], p_att[:, ATT_Q:ATT_Q + ATT_KV], p_att[:, ATT_Q + ATT_KV:]
    cos = cos_ref[...]
    sin = sin_ref[...]
    q = _rope_norm(atq, qw_ref[...], seg_ref[...], cos, sin)
    k = _rope_norm(atk, kw_ref[...], seg_ref[0:LANES, 0:LANES], cos, sin).astype(BF16)
    p_rest = jnp.dot(hb, w_ref[:, PACK_ATT_COLS:], preferred_element_type=F32)
    qt = (q * (LOG2E * ATT_HEAD_DIM ** -0.5)).T.astype(BF16)
    for hd in range(ATT_HEADS):
        qo_ref[hd // ATT_GROUP, hd % ATT_GROUP] = qt[hd * ATT_HEAD_DIM:(hd + 1) * ATT_HEAD_DIM]
    vt = atv.T.astype(BF16)
    for kvh in range(ATT_KV_HEADS):
        ko_ref[kvh] = k[:, kvh * ATT_HEAD_DIM:(kvh + 1) * ATT_HEAD_DIM]
        vo_ref[kvh] = vt[kvh * ATT_HEAD_DIM:(kvh + 1) * ATT_HEAD_DIM]
    o = 0
    for ref in (qkv_ref, dnz_ref, atz_ref, ba_ref):
        n = ref.shape[-1]
        ref[...] = p_rest[:, o:o + n].astype(ref.dtype)
        o += n


def _in_proj(x2, nw, w_packed, qw_row, kw_row, seg, cos_t, sin_t, b, t, tm, tq):
    bt = x2.shape[0]
    nr = t // tm
    per_q = tq // tm
    row = lambda i: (i, 0)
    const = lambda i: (0, 0)
    tab = lambda i: (i % nr, 0)
    return pl.pallas_call(
        _in_proj_kernel,
        grid=(bt // tm,),
        in_specs=[pl.BlockSpec((tm, D_MODEL), row), pl.BlockSpec((1, D_MODEL), const),
                  pl.BlockSpec((D_MODEL, PACK_COLS), const),
                  pl.BlockSpec((1, ATT_Q), const), pl.BlockSpec((1, ATT_KV), const), pl.BlockSpec((ATT_Q, ATT_Q), const),
                  pl.BlockSpec((tm, LANES), tab), pl.BlockSpec((tm, LANES), tab)],
        out_specs=[pl.BlockSpec((tm, DN_CONV_DIM), row), pl.BlockSpec((tm, DN_QK), row), pl.BlockSpec((tm, ATT_Q), row),
                   pl.BlockSpec((tm, LANES), row),
                   pl.BlockSpec((None, ATT_KV_HEADS, None, ATT_GROUP, ATT_HEAD_DIM, tm),
                                lambda i: (i // nr, 0, (i % nr) // per_q, 0, 0, (i % nr) % per_q)),
                   pl.BlockSpec((None, ATT_KV_HEADS, tm, ATT_HEAD_DIM), lambda i: (i // nr, 0, i % nr, 0)),
                   pl.BlockSpec((None, ATT_KV_HEADS, ATT_HEAD_DIM, tm), lambda i: (i // nr, 0, 0, i % nr))],
        out_shape=[jax.ShapeDtypeStruct((bt, DN_CONV_DIM), F32), jax.ShapeDtypeStruct((bt, DN_QK), BF16),
                   jax.ShapeDtypeStruct((bt, ATT_Q), BF16), jax.ShapeDtypeStruct((bt, LANES), F32),
                   jax.ShapeDtypeStruct((b, ATT_KV_HEADS, t // tq, ATT_GROUP, ATT_HEAD_DIM, tq), BF16),
                   jax.ShapeDtypeStruct((b, ATT_KV_HEADS, t, ATT_HEAD_DIM), BF16),
                   jax.ShapeDtypeStruct((b, ATT_KV_HEADS, ATT_HEAD_DIM, t), BF16)],
        compiler_params=_params(("parallel",)),
        name="in_proj",
    )(x2, nw, w_packed, qw_row, kw_row, seg, cos_t, sin_t)


def _dn_prep_kernel(x_ref, prev_ref, next_ref, ba_ref, cw_ref, alog_ref, dtb_ref,
                    q_ref, k_ref, v_ref, gb_ref, xe_ref, *, tm):
    i = pl.program_id(1)
    n = pl.num_programs(1)
    halo = SUBLANES
    xe_ref[0:halo, :] = jnp.where(i > 0, prev_ref[...], 0.0)
    xe_ref[halo:halo + tm, :] = x_ref[...]
    xe_ref[halo + tm:2 * halo + tm, :] = jnp.where(i < n - 1, next_ref[...], 0.0)
    pad = CONV_K // 2
    y = jnp.zeros((tm, DN_CONV_DIM), F32)
    for j in range(CONV_K):
        y = y + xe_ref[halo + j - pad:halo + j - pad + tm, :] * cw_ref[j:j + 1, :]
    y = _silu(y)
    for hd in range(DN_HEADS):
        lo = hd * DN_HEAD
        q = y[:, lo:lo + DN_HEAD]
        k = y[:, DN_QK + lo:DN_QK + lo + DN_HEAD]
        q_ref[:, lo:lo + DN_HEAD] = q * lax.rsqrt(jnp.sum(q * q, axis=-1, keepdims=True) + EPS) * (DN_HEAD ** -0.5)
        k_ref[:, lo:lo + DN_HEAD] = k * lax.rsqrt(jnp.sum(k * k, axis=-1, keepdims=True) + EPS)
    v_ref[...] = y[:, 2 * DN_QK:]

    ba = ba_ref[...]
    beta = 1.0 / (1.0 + jnp.exp(-ba))
    z = ba + dtb_ref[...]
    softplus = jnp.maximum(z, 0.0) + jnp.log(1.0 + jnp.exp(-jnp.abs(z)))
    g = -jnp.exp(alog_ref[...]) * softplus
    r = lax.broadcasted_iota(jnp.int32, (tm, tm), 0)
    c = lax.broadcasted_iota(jnp.int32, (tm, tm), 1)
    same = (r // CHUNK) == (c // CHUNK)
    tri_f = jnp.where(same & (c <= r), 1.0, 0.0).astype(F32)
    tri_b = jnp.where(same & (c >= r), 1.0, 0.0).astype(F32)
    gc_f = jnp.dot(tri_f, g, precision=HI, preferred_element_type=F32)
    gc_b = jnp.dot(tri_b, g, precision=HI, preferred_element_type=F32)
    lane = lax.broadcasted_iota(jnp.int32, (tm, LANES), 1)
    gc = jnp.where(lane < DECAY_LANE + DN_HEADS, gc_f, gc_b)
    gb_ref[...] = jnp.where(lane < DECAY_LANE, beta, gc)


def _dn_prep(qkv, ba, cw, alog_row, dtb_row, tm):
    b, t, _ = qkv.shape
    nb = tm // SUBLANES
    last = t // SUBLANES - 1
    kern = functools.partial(_dn_prep_kernel, tm=tm)
    return pl.pallas_call(
        kern,
        grid=(b, t // tm),
        in_specs=[pl.BlockSpec((None, tm, DN_CONV_DIM), lambda bi, i: (bi, i, 0)),
                  pl.BlockSpec((None, SUBLANES, DN_CONV_DIM), lambda bi, i: (bi, jnp.maximum(i * nb - 1, 0), 0)),
                  pl.BlockSpec((None, SUBLANES, DN_CONV_DIM), lambda bi, i: (bi, jnp.minimum((i + 1) * nb, last), 0)),
                  pl.BlockSpec((None, tm, LANES), lambda bi, i: (bi, i, 0)),
                  pl.BlockSpec((SUBLANES, DN_CONV_DIM), lambda bi, i: (0, 0)),
                  pl.BlockSpec((1, LANES), lambda bi, i: (0, 0)),
                  pl.BlockSpec((1, LANES), lambda bi, i: (0, 0))],
        out_specs=[pl.BlockSpec((None, tm, DN_QK), lambda bi, i: (bi, i, 0))] * 3
        + [pl.BlockSpec((None, tm, LANES), lambda bi, i: (bi, i, 0))],
        out_shape=[jax.ShapeDtypeStruct((b, t, DN_QK), F32)] * 3 + [jax.ShapeDtypeStruct((b, t, LANES), F32)],
        scratch_shapes=[pltpu.VMEM((tm + 2 * SUBLANES, DN_CONV_DIM), F32)],
        compiler_params=_params(("parallel", "parallel")),
        name="dn_prep",
    )(qkv, qkv, qkv, ba, cw, alog_row, dtb_row)


def _mm(a, b):
    return jnp.dot(a.astype(BF16), b.astype(BF16), preferred_element_type=F32)


def _bmm(a, b):
    return jnp.einsum('cij,cjk->cik', a.astype(BF16), b.astype(BF16), preferred_element_type=F32)


def _bmm_nt(a, b):
    return jnp.einsum('cid,cjd->cij', a.astype(BF16), b.astype(BF16), preferred_element_type=F32)


def _dn_masks(d, pack=1):
    ii = lax.broadcasted_iota(jnp.int32, (CHUNK, pack * CHUNK), 0)
    jj = lax.broadcasted_iota(jnp.int32, (CHUNK, pack * CHUNK), 1) % CHUNK
    if d == 1:
        ii, jj = jj, ii
    eye = jnp.where(ii == jj, 1.0, 0.0).astype(F32)
    merge = []
    s = 1
    while s < CHUNK:
        merge.append((ii // (2 * s) == jj // (2 * s)) & ((ii // s) % 2 == 1) & ((jj // s) % 2 == 0))
        s *= 2
    return ii >= jj, ii > jj, eye, merge


def _dn_chunk_terms(streams, cb, out):
    masks = {d: _dn_masks(d) for d in sorted({s[0] for s in streams})}
    data = []
    for d, hds, (q_ref, k_ref, v_ref, gb_ref, gr_ref) in streams:
        gb = gb_ref[...]
        gr = gr_ref[...]

        def heads(ref):
            return jnp.concatenate([ref[:, hd * DN_HEAD:(hd + 1) * DN_HEAD].reshape(cb, CHUNK, DN_HEAD)
                                    for hd in hds], axis=0)

        def gate_cols(lane0):
            return jnp.concatenate([gb[:, lane0 + hd:lane0 + hd + 1].reshape(cb, CHUNK, 1) for hd in hds], axis=0)

        q, k, v = heads(q_ref), heads(k_ref), heads(v_ref)
        beta = gate_cols(BETA_LANE + d * DN_HEADS)
        gcol = gate_cols(DECAY_LANE + d * DN_HEADS)
        gl = DECAY_LANE + d * DN_HEADS
        grow = jnp.concatenate([gr[:, gl + hd:gl + hd + 1, :] for hd in hds], axis=0)
        glast = grow[:, :, CHUNK - 1:CHUNK] if d == 0 else grow[:, :, 0:1]
        decay = jnp.where(masks[d][0], jnp.exp(jnp.minimum(gcol - grow, 0.0)), 0.0)
        data.append(dict(d=d, q=q, k=k, v=v, beta=beta, gcol=gcol, glast=glast, decay=decay, kb=k * beta))
    n = range(len(data))
    lower = [jnp.where(masks[x['d']][1], _bmm_nt(x['kb'], x['k']) * x['decay'], 0.0) for x in data]
    wmask = {d: _dn_masks(d, DN_PACK) for d in masks}
    unit = lax.broadcasted_iota(jnp.int32, (CHUNK, DN_PACK * CHUNK), 1) // CHUNK

    def wide(x):
        x4 = x.reshape(x.shape[0] // DN_PACK, DN_PACK, CHUNK, CHUNK)
        return jnp.concatenate([x4[:, j] for j in range(DN_PACK)], axis=-1)

    def block_diag(xw):
        return jnp.concatenate([jnp.where(unit == j, xw, 0.0) for j in range(DN_PACK)], axis=1)

    yield
    lower_w = [wide(lower[i]) for i in n]
    tinv = [wmask[data[i]['d']][2] - jnp.where(wmask[data[i]['d']][3][0], lower_w[i], 0.0) for i in n]
    for lvl in range(1, len(wmask[data[0]['d']][3])):
        te = [_bmm(tinv[i], block_diag(jnp.where(wmask[data[i]['d']][3][lvl], lower_w[i], 0.0))) for i in n]
        yield
        tinv = [tinv[i] - _bmm(te[i], block_diag(tinv[i])) for i in n]
        yield
    sol = []
    for i, x in enumerate(data):
        rhs = jnp.concatenate([x['v'] * x['beta'], x['kb'] * jnp.exp(x['gcol'])], axis=-1)
        units = rhs.shape[0]
        s_tall = _bmm(block_diag(tinv[i]), rhs.reshape(units // DN_PACK, DN_PACK * CHUNK, 2 * DN_HEAD))
        sol.append(s_tall.reshape(units, CHUNK, 2 * DN_HEAD))
    yield
    for i, x in enumerate(data):
        u, w = sol[i][..., :DN_HEAD], sol[i][..., DN_HEAD:]
        attn = (_bmm_nt(x['q'], x['k']) * x['decay']).astype(BF16)
        wq = jnp.concatenate([w, x['q'] * jnp.exp(x['gcol'])], axis=1).astype(BF16)
        kdec = (x['k'] * jnp.exp(x['glast'] - x['gcol'])).astype(BF16)
        out.append((u, wq, attn, kdec, jnp.exp(x['glast'])))


def _dn_scan(streams, slot, bufs, o_refs, s_ref, cb):
    u_buf, wq_buf, attn_buf, kdec_buf, eg_buf = bufs
    units = DN_STREAM_HEADS * cb
    for step in range(cb):
        chains = []
        for si, (d, hds, _) in enumerate(streams):
            c = step if d == 0 else cb - 1 - step
            chains += [(d, hd, c, si * units + i * cb + c) for i, hd in enumerate(hds)]
        states = [s_ref[d * DN_HEADS + hd] for d, hd, _, _ in chains]
        ws = [_mm(wq_buf[slot, n], st) for (_, _, _, n), st in zip(chains, states)]
        yield
        for (d, hd, c, n), st, w in zip(chains, states, ws):
            v_new = (u_buf[slot, n] - w[:CHUNK]).astype(BF16)
            o = w[CHUNK:] + _mm(attn_buf[slot, n], v_new)
            s_ref[d * DN_HEADS + hd] = st * eg_buf[slot, n] + lax.dot_general(
                kdec_buf[slot, n], v_new, (((0,), (0,)), ((), ())), preferred_element_type=F32)
            o_refs[d][c * CHUNK:(c + 1) * CHUNK, hd * DN_HEAD:(hd + 1) * DN_HEAD] = o.astype(o_refs[d].dtype)
        yield


def _dn_main_kernel(qf, kf, vf, gbf, grf, qb, kb, vb, gbb, grb, of_ref, ob_ref, s_ref,
                    u_buf, wq_buf, attn_buf, kdec_buf, eg_buf, *, cb):
    j = pl.program_id(1)
    bufs = (u_buf, wq_buf, attn_buf, kdec_buf, eg_buf)

    @pl.when(j == 0)
    def _():
        s_ref[...] = jnp.zeros_like(s_ref)
        for buf in bufs:
            buf[1] = jnp.zeros(buf.shape[1:], buf.dtype)

    cur = (j + 1) % 2
    nxt = j % 2
    groups = [tuple(range(h, h + DN_STREAM_HEADS)) for h in range(0, DN_HEADS, DN_STREAM_HEADS)]
    streams = ([(0, hds, (qf, kf, vf, gbf, grf)) for hds in groups]
               + [(1, hds, (qb, kb, vb, gbb, grb)) for hds in groups])
    units = DN_STREAM_HEADS * cb
    pre = []
    terms = _dn_chunk_terms(streams, cb, pre)
    scan = _dn_scan(streams, cur, bufs, (of_ref, ob_ref), s_ref, cb)
    live = [terms, scan]
    while live:
        for gen in (terms, scan):
            if gen in live and next(gen, StopIteration) is StopIteration:
                live.remove(gen)
    for si, (u, wq, attn, kdec, eg) in enumerate(pre):
        lo, hi = si * units, (si + 1) * units
        u_buf[nxt, lo:hi] = u
        wq_buf[nxt, lo:hi] = wq
        attn_buf[nxt, lo:hi] = attn
        kdec_buf[nxt, lo:hi] = kdec
        eg_buf[nxt, lo:hi] = jnp.broadcast_to(eg, (units, 1, DN_HEAD))


def _dn_main(qn, kn, vv, gb, gr, cb):
    b, t, _ = qn.shape
    cbt = cb * CHUNK
    ng = t // cbt
    fwd_in = lambda bi, j: (bi, jnp.minimum(j, ng - 1), 0)
    bwd_in = lambda bi, j: (bi, ng - 1 - jnp.minimum(j, ng - 1), 0)
    fwd_in4 = lambda bi, j: (bi, jnp.minimum(j, ng - 1), 0, 0)
    bwd_in4 = lambda bi, j: (bi, ng - 1 - jnp.minimum(j, ng - 1), 0, 0)
    fwd_out = lambda bi, j: (bi, jnp.maximum(j - 1, 0), 0)
    bwd_out = lambda bi, j: (bi, ng - 1 - jnp.maximum(j - 1, 0), 0)

    def specs(im3, im4):
        return ([pl.BlockSpec((None, cbt, DN_QK), im3)] * 3
                + [pl.BlockSpec((None, cbt, LANES), im3), pl.BlockSpec((None, cb, 2 * SUBLANES, CHUNK), im4)])

    units = N_DIR * DN_HEADS * cb
    kern = functools.partial(_dn_main_kernel, cb=cb)
    return pl.pallas_call(
        kern,
        grid=(b, ng + 1),
        in_specs=specs(fwd_in, fwd_in4) + specs(bwd_in, bwd_in4),
        out_specs=[pl.BlockSpec((None, cbt, DN_QK), fwd_out), pl.BlockSpec((None, cbt, DN_QK), bwd_out)],
        out_shape=[jax.ShapeDtypeStruct((b, t, DN_QK), BF16)] * 2,
        scratch_shapes=[pltpu.VMEM((N_DIR * DN_HEADS, DN_HEAD, DN_HEAD), F32),
                        pltpu.VMEM((2, units, CHUNK, DN_HEAD), F32),
                        pltpu.VMEM((2, units, 2 * CHUNK, DN_HEAD), BF16),
                        pltpu.VMEM((2, units, CHUNK, CHUNK), BF16),
                        pltpu.VMEM((2, units, CHUNK, DN_HEAD), BF16),
                        pltpu.VMEM((2, units, 1, DN_HEAD), F32)],
        compiler_params=_params(("parallel", "arbitrary")),
        name="dn_main",
    )(qn, kn, vv, gb, gr, qn, kn, vv, gb, gr)


def _rope_norm(x, w, seg, cos, sin):
    sq = x * x
    sq_hi = sq.astype(BF16)
    sq_lo = (sq - sq_hi.astype(F32)).astype(BF16)
    ss = jnp.dot(sq_hi, seg, preferred_element_type=F32) + jnp.dot(sq_lo, seg, preferred_element_type=F32)
    xn = x * lax.rsqrt(ss * (1.0 / ATT_HEAD_DIM) + EPS) * w
    quarter = ATT_HEAD_DIM // 4
    outs = []
    for t in range(x.shape[-1] // LANES):
        xt = xn[:, t * LANES:(t + 1) * LANES]
        lane = lax.broadcasted_iota(jnp.int32, xt.shape, 1)
        rot = jnp.where(lane % (2 * quarter) < quarter,
                        pltpu.roll(xt, LANES - quarter, axis=1), pltpu.roll(xt, quarter, axis=1))
        outs.append(xt * cos + rot * sin)
    return outs[0] if len(outs) == 1 else jnp.concatenate(outs, axis=-1)


def _attn_kernel(qt_ref, k_ref, vt_ref, o_ref, s_buf, p_buf, *, tk):
    qt = jnp.concatenate([qt_ref[g] for g in range(ATT_GROUP)], axis=1)
    m_cols = qt.shape[1]
    nk = k_ref.shape[0] // tk
    ones = jnp.ones((2 * SUBLANES, tk), BF16)

    def scores(j, slot):
        start = pl.multiple_of(j * tk, tk)
        s = jnp.dot(k_ref[pl.ds(start, tk), :], qt, preferred_element_type=F32)
        s_buf[slot] = s
        return jnp.max(jnp.max(s.reshape(tk // SUBLANES, SUBLANES, m_cols), axis=0), axis=0, keepdims=True)

    def softmax(slot, m, chunk_max):
        m_new = jnp.maximum(m, chunk_max)
        p_buf[slot] = jnp.exp2(s_buf[slot] - m_new).astype(BF16)
        return m_new, jnp.exp2(m - m_new)

    def values(j, slot, alpha, acc):
        start = pl.multiple_of(j * tk, tk)
        vt = jnp.concatenate([vt_ref[:, pl.ds(start, tk)], ones], axis=0)
        return alpha * acc + jnp.dot(vt, p_buf[slot], preferred_element_type=F32)

    m = jnp.full((1, m_cols), jnp.finfo(F32).min, F32)
    acc = jnp.zeros((ATT_HEAD_DIM + 2 * SUBLANES, m_cols), F32)
    cmax0 = scores(0, 0)
    cmax1 = scores(1, 1)
    m, alpha0 = softmax(0, m, cmax0)
    cmax2 = scores(2, 0)
    m, alpha1 = softmax(1, m, cmax1)
    acc = values(0, 0, alpha0, acc)

    def body(i, carry):
        m, cmax, alpha_prev, acc = carry
        j0 = 2 * i
        cmax1 = scores(j0 + 1, 1)
        m, alpha0 = softmax(0, m, cmax)
        acc = values(j0 - 1, 1, alpha_prev, acc)
        cmax2 = scores(j0 + 2, 0)
        m, alpha1 = softmax(1, m, cmax1)
        acc = values(j0, 0, alpha0, acc)
        return m, cmax2, alpha1, acc

    m, cmax, alpha_prev, acc = lax.fori_loop(1, nk // 2 - 1, body, (m, cmax2, alpha1, acc))
    cmax1 = scores(nk - 1, 1)
    m, alpha0 = softmax(0, m, cmax)
    acc = values(nk - 3, 1, alpha_prev, acc)
    m, alpha = softmax(1, m, cmax1)
    acc = values(nk - 2, 0, alpha0, acc)
    acc = values(nk - 1, 1, alpha, acc)
    o = acc[:ATT_HEAD_DIM] / acc[ATT_HEAD_DIM:ATT_HEAD_DIM + 1]
    tq = m_cols // ATT_GROUP
    o = jnp.concatenate([o[:, g * tq:(g + 1) * tq] for g in range(ATT_GROUP)], axis=0)
    o_ref[...] = o.T.astype(o_ref.dtype)


def _attention(qt6, k4, vt4, tk):
    b, _, nq, _, _, tq = qt6.shape
    t = k4.shape[2]
    m_cols = ATT_GROUP * tq
    assert t % (2 * tk) == 0 and t // tk >= 4
    kern = functools.partial(_attn_kernel, tk=tk)
    return pl.pallas_call(
        kern,
        grid=(b, ATT_KV_HEADS, nq),
        in_specs=[pl.BlockSpec((None, None, None, ATT_GROUP, ATT_HEAD_DIM, tq), lambda bi, h, i: (bi, h, i, 0, 0, 0)),
                  pl.BlockSpec((None, None, t, ATT_HEAD_DIM), lambda bi, h, i: (bi, h, 0, 0)),
                  pl.BlockSpec((None, None, ATT_HEAD_DIM, t), lambda bi, h, i: (bi, h, 0, 0))],
        out_specs=pl.BlockSpec((None, tq, ATT_GROUP * ATT_HEAD_DIM), lambda bi, h, i: (bi, i, h)),
        out_shape=jax.ShapeDtypeStruct((b, t, ATT_Q), BF16),
        scratch_shapes=[pltpu.VMEM((2, tk, m_cols), F32), pltpu.VMEM((2, tk, m_cols), BF16)],
        compiler_params=_params(("parallel", "parallel", "parallel")),
        name="attention",
    )(qt6, k4, vt4)


def _out_proj_kernel(of_ref, ob_ref, dnz_ref, dnw_ref, oat_ref, atz_ref, x_ref, w_ref, fw_ref, y_ref, *, final):
    o = of_ref[...].astype(F32) + ob_ref[...].astype(F32)
    parts = []
    for hd in range(DN_HEADS):
        oh = o[:, hd * DN_HEAD:(hd + 1) * DN_HEAD]
        parts.append(oh * lax.rsqrt(jnp.mean(oh * oh, axis=-1, keepdims=True) + EPS) * dnw_ref[...])
    y_dn = jnp.concatenate(parts, axis=-1) * _silu(dnz_ref[...].astype(F32))
    y_at = oat_ref[...].astype(F32) * _silu(atz_ref[...].astype(F32))
    y = jnp.concatenate([y_dn, y_at], axis=-1).astype(BF16)
    out = x_ref[...] + jnp.dot(y, w_ref[...], preferred_element_type=F32)
    if final:
        out = out * lax.rsqrt(jnp.mean(out * out, axis=-1, keepdims=True) + EPS) * fw_ref[...]
    y_ref[...] = out


def _out_proj(o_f, o_b, dnz, dnw_row, o_at, atz, x2, w_out, fw_row, tm, final):
    bt = x2.shape[0]
    row = lambda i: (i, 0)
    const = lambda i: (0, 0)
    kern = functools.partial(_out_proj_kernel, final=final)
    return pl.pallas_call(
        kern,
        grid=(bt // tm,),
        in_specs=[pl.BlockSpec((tm, DN_QK), row), pl.BlockSpec((tm, DN_QK), row), pl.BlockSpec((tm, DN_QK), row),
                  pl.BlockSpec((1, DN_HEAD), const),
                  pl.BlockSpec((tm, ATT_Q), row), pl.BlockSpec((tm, ATT_Q), row),
                  pl.BlockSpec((tm, D_MODEL), row),
                  pl.BlockSpec((D_MODEL, D_MODEL), const), pl.BlockSpec((1, D_MODEL), const)],
        out_specs=pl.BlockSpec((tm, D_MODEL), row),
        out_shape=jax.ShapeDtypeStruct((bt, D_MODEL), F32),
        compiler_params=_params(("parallel",)),
        name="out_proj_final" if final else "out_proj",
    )(o_f, o_b, dnz, dnw_row, o_at, atz, x2, w_out, fw_row)


def _pack_w_in(w):
    splits = np.cumsum(IN_SIZES)[:-1].tolist()
    qkv, dnz, bb, aa, atq, atk, atv, atz = jnp.split(w, splits, axis=-1)
    pad = jnp.zeros((w.shape[0], LANES - 2 * N_DIR * DN_HEADS), w.dtype)
    return jnp.concatenate([atq, atk, atv, qkv, dnz, atz, bb, aa, pad], axis=-1).astype(BF16)


def _rope_tables(t):
    half = ATT_HEAD_DIM // 2
    pos = np.arange(t)
    row = (pos // GRID_W).astype(np.float32)
    col = (pos % GRID_W).astype(np.float32)
    inv = jnp.asarray(ROPE_THETA, F32) ** (-jnp.arange(0, half, 2, dtype=F32) / half)
    ang_r = jnp.asarray(row)[:, None] * inv[None, :]
    ang_c = jnp.asarray(col)[:, None] * inv[None, :]
    cos = jnp.concatenate([jnp.cos(ang_r)] * 2 + [jnp.cos(ang_c)] * 2, axis=-1)
    sin = jnp.concatenate([-jnp.sin(ang_r), jnp.sin(ang_r), -jnp.sin(ang_c), jnp.sin(ang_c)], axis=-1)
    reps = LANES // ATT_HEAD_DIM
    return jnp.tile(cos, (1, reps)), jnp.tile(sin, (1, reps))


def _gate_row(p):
    return jnp.zeros((1, LANES), F32).at[0, DECAY_LANE:DECAY_LANE + N_DIR * DN_HEADS].set(p.reshape(-1))


def kernel(x, norm_w, w_in, conv_w, a_log, dt_bias, dn_norm_w, q_norm_w, k_norm_w, w_out, final_norm_w):
    b, t, d = x.shape
    depth = w_in.shape[0]
    tm = 256
    tm_proj = min(512, b * t)
    cb = min(4, t // CHUNK)
    tq = min(1024, t)
    tk = min(512, t // 4)
    cos_t, sin_t = _rope_tables(t)
    lane = np.arange(ATT_Q)
    seg = jnp.asarray((lane[:, None] // ATT_HEAD_DIM) == (lane[None, :] // ATT_HEAD_DIM), BF16)
    x2 = x.reshape(b * t, d)
    for l in range(depth):
        qkv, dnz, atz, ba, qt6, k4, vt4 = _in_proj(
            x2, norm_w[l][None, :], _pack_w_in(w_in[l]),
            jnp.tile(q_norm_w[l], ATT_HEADS)[None, :], jnp.tile(k_norm_w[l], ATT_KV_HEADS)[None, :],
            seg, cos_t, sin_t, b, t, tm_proj, tq)
        cw = jnp.zeros((SUBLANES, DN_CONV_DIM), F32).at[:CONV_K].set(conv_w[l])
        qn, kn, vv, gb = _dn_prep(qkv.reshape(b, t, -1), ba.reshape(b, t, -1), cw,
                                  _gate_row(a_log[l]), _gate_row(dt_bias[l]), tm)
        gr = gb[..., :2 * SUBLANES].reshape(b, t // CHUNK, CHUNK, 2 * SUBLANES).transpose(0, 1, 3, 2)
        o_f, o_b = _dn_main(qn, kn, vv, gb, gr, cb)
        o_at = _attention(qt6, k4, vt4, tk).reshape(b * t, ATT_Q)
        x2 = _out_proj(o_f.reshape(b * t, -1), o_b.reshape(b * t, -1), dnz, dn_norm_w[l][None, :], o_at, atz,
                       x2, w_out[l].astype(BF16), final_norm_w[None, :], tm_proj, final=(l == depth - 1))
    return x2.reshape(b, t, d)
```

```python
import functools

import jax
import jax.numpy as jnp
import numpy as np
from jax import lax
from jax.experimental import pallas as pl
from jax.experimental.pallas import tpu as pltpu

D_MODEL = 1024
DN_HEADS = 4
DN_HEAD = 128
DN_QK = DN_HEADS * DN_HEAD
DN_CONV_DIM = 3 * DN_QK
CONV_K = 5
CHUNK = 128
N_DIR = 2
ATT_HEADS = 8
ATT_KV_HEADS = 2
ATT_GROUP = ATT_HEADS // ATT_KV_HEADS
ATT_HEAD_DIM = 64
ATT_Q = ATT_HEADS * ATT_HEAD_DIM
ATT_KV = ATT_KV_HEADS * ATT_HEAD_DIM
ROPE_THETA = 10000.0
GRID_W = 64
EPS = 1e-6
IN_SIZES = (DN_CONV_DIM, DN_QK, N_DIR * DN_HEADS, N_DIR * DN_HEADS, ATT_Q, ATT_KV, ATT_KV, ATT_Q)
LANES = 128
SUBLANES = 8
VMEM_LIMIT = 48 * 1024 * 1024
PACK_WIDTHS = (ATT_Q, ATT_KV, ATT_KV, DN_CONV_DIM, DN_QK, ATT_Q, LANES)
PACK_ATT_COLS = ATT_Q + 2 * ATT_KV
PACK_COLS = sum(PACK_WIDTHS)
BETA_LANE = 0
DECAY_LANE = N_DIR * DN_HEADS
DN_STREAM_HEADS = 4
MXU_WIDTH = 256
DN_PACK = MXU_WIDTH // CHUNK

F32 = jnp.float32
BF16 = jnp.bfloat16
HI = lax.Precision.HIGHEST
LOG2E = 1.4426950408889634


def _params(sem):
    return pltpu.CompilerParams(dimension_semantics=sem, vmem_limit_bytes=VMEM_LIMIT)


def _silu(x):
    return x / (1.0 + jnp.exp(-x))


def _in_proj_kernel(x_ref, nw_ref, w_ref, qw_ref, kw_ref, seg_ref, cos_ref, sin_ref,
                    qkv_ref, dnz_ref, atz_ref, ba_ref, qo_ref, ko_ref, vo_ref):
    x = x_ref[...]
    h = x * lax.rsqrt(jnp.mean(x * x, axis=-1, keepdims=True) + EPS) * nw_ref[...]
    hb = h.astype(BF16)
    p_att = jnp.dot(hb, w_ref[:, :PACK_ATT_COLS], preferred_element_type=F32)
    atq, atk, atv = p_att[:, :ATT_Q], p_att[:, ATT_Q:ATT_Q + ATT_KV], p_att[:, ATT_Q + ATT_KV:]
    cos = cos_ref[...]
    sin = sin_ref[...]
    q = _rope_norm(atq, qw_ref[...], seg_ref[...], cos, sin)
    k = _rope_norm(atk, kw_ref[...], seg_ref[0:LANES, 0:LANES], cos, sin).astype(BF16)
    p_rest = jnp.dot(hb, w_ref[:, PACK_ATT_COLS:], preferred_element_type=F32)
    qt = (q * (LOG2E * ATT_HEAD_DIM ** -0.5)).T.astype(BF16)
    for hd in range(ATT_HEADS):
        qo_ref[hd // ATT_GROUP, hd % ATT_GROUP] = qt[hd * ATT_HEAD_DIM:(hd + 1) * ATT_HEAD_DIM]
    vt = atv.T.astype(BF16)
    for kvh in range(ATT_KV_HEADS):
        ko_ref[kvh] = k[:, kvh * ATT_HEAD_DIM:(kvh + 1) * ATT_HEAD_DIM]
        vo_ref[kvh] = vt[kvh * ATT_HEAD_DIM:(kvh + 1) * ATT_HEAD_DIM]
    o = 0
    for ref in (qkv_ref, dnz_ref, atz_ref, ba_ref):
        n = ref.shape[-1]
        ref[...] = p_rest[:, o:o + n].astype(ref.dtype)
        o += n


def _in_proj(x2, nw, w_packed, qw_row, kw_row, seg, cos_t, sin_t, b, t, tm, tq):
    bt = x2.shape[0]
    nr = t // tm
    per_q = tq // tm
    row = lambda i: (i, 0)
    const = lambda i: (0, 0)
    tab = lambda i: (i % nr, 0)
    return pl.pallas_call(
        _in_proj_kernel,
        grid=(bt // tm,),
        in_specs=[pl.BlockSpec((tm, D_MODEL), row), pl.BlockSpec((1, D_MODEL), const),
                  pl.BlockSpec((D_MODEL, PACK_COLS), const),
                  pl.BlockSpec((1, ATT_Q), const), pl.BlockSpec((1, ATT_KV), const), pl.BlockSpec((ATT_Q, ATT_Q), const),
                  pl.BlockSpec((tm, LANES), tab), pl.BlockSpec((tm, LANES), tab)],
        out_specs=[pl.BlockSpec((tm, DN_CONV_DIM), row), pl.BlockSpec((tm, DN_QK), row), pl.BlockSpec((tm, ATT_Q), row),
                   pl.BlockSpec((tm, LANES), row),
                   pl.BlockSpec((None, ATT_KV_HEADS, None, ATT_GROUP, ATT_HEAD_DIM, tm),
                                lambda i: (i // nr, 0, (i % nr) // per_q, 0, 0, (i % nr) % per_q)),
                   pl.BlockSpec((None, ATT_KV_HEADS, tm, ATT_HEAD_DIM), lambda i: (i // nr, 0, i % nr, 0)),
                   pl.BlockSpec((None, ATT_KV_HEADS, ATT_HEAD_DIM, tm), lambda i: (i // nr, 0, 0, i % nr))],
        out_shape=[jax.ShapeDtypeStruct((bt, DN_CONV_DIM), F32), jax.ShapeDtypeStruct((bt, DN_QK), BF16),
                   jax.ShapeDtypeStruct((bt, ATT_Q), BF16), jax.ShapeDtypeStruct((bt, LANES), F32),
                   jax.ShapeDtypeStruct((b, ATT_KV_HEADS, t // tq, ATT_GROUP, ATT_HEAD_DIM, tq), BF16),
                   jax.ShapeDtypeStruct((b, ATT_KV_HEADS, t, ATT_HEAD_DIM), BF16),
                   jax.ShapeDtypeStruct((b, ATT_KV_HEADS, ATT_HEAD_DIM, t), BF16)],
        compiler_params=_params(("parallel",)),
        name="in_proj",
    )(x2, nw, w_packed, qw_row, kw_row, seg, cos_t, sin_t)


def _dn_prep_kernel(x_ref, prev_ref, next_ref, ba_ref, cw_ref, alog_ref, dtb_ref,
                    q_ref, k_ref, v_ref, gb_ref, xe_ref, *, tm):
    i = pl.program_id(1)
    n = pl.num_programs(1)
    halo = SUBLANES
    xe_ref[0:halo, :] = jnp.where(i > 0, prev_ref[...], 0.0)
    xe_ref[halo:halo + tm, :] = x_ref[...]
    xe_ref[halo + tm:2 * halo + tm, :] = jnp.where(i < n - 1, next_ref[...], 0.0)
    pad = CONV_K // 2
    y = jnp.zeros((tm, DN_CONV_DIM), F32)
    for j in range(CONV_K):
        y = y + xe_ref[halo + j - pad:halo + j - pad + tm, :] * cw_ref[j:j + 1, :]
    y = _silu(y)
    for hd in range(DN_HEADS):
        lo = hd * DN_HEAD
        q = y[:, lo:lo + DN_HEAD]
        k = y[:, DN_QK + lo:DN_QK + lo + DN_HEAD]
        q_ref[:, lo:lo + DN_HEAD] = q * lax.rsqrt(jnp.sum(q * q, axis=-1, keepdims=True) + EPS) * (DN_HEAD ** -0.5)
        k_ref[:, lo:lo + DN_HEAD] = k * lax.rsqrt(jnp.sum(k * k, axis=-1, keepdims=True) + EPS)
    v_ref[...] = y[:, 2 * DN_QK:]

    ba = ba_ref[...]
    beta = 1.0 / (1.0 + jnp.exp(-ba))
    z = ba + dtb_ref[...]
    softplus = jnp.maximum(z, 0.0) + jnp.log(1.0 + jnp.exp(-jnp.abs(z)))
    g = -jnp.exp(alog_ref[...]) * softplus
    r = lax.broadcasted_iota(jnp.int32, (tm, tm), 0)
    c = lax.broadcasted_iota(jnp.int32, (tm, tm), 1)
    same = (r // CHUNK) == (c // CHUNK)
    tri_f = jnp.where(same & (c <= r), 1.0, 0.0).astype(F32)
    tri_b = jnp.where(same & (c >= r), 1.0, 0.0).astype(F32)
    gc_f = jnp.dot(tri_f, g, precision=HI, preferred_element_type=F32)
    gc_b = jnp.dot(tri_b, g, precision=HI, preferred_element_type=F32)
    lane = lax.broadcasted_iota(jnp.int32, (tm, LANES), 1)
    gc = jnp.where(lane < DECAY_LANE + DN_HEADS, gc_f, gc_b)
    gb_ref[...] = jnp.where(lane < DECAY_LANE, beta, gc)


def _dn_prep(qkv, ba, cw, alog_row, dtb_row, tm):
    b, t, _ = qkv.shape
    nb = tm // SUBLANES
    last = t // SUBLANES - 1
    kern = functools.partial(_dn_prep_kernel, tm=tm)
    return pl.pallas_call(
        kern,
        grid=(b, t // tm),
        in_specs=[pl.BlockSpec((None, tm, DN_CONV_DIM), lambda bi, i: (bi, i, 0)),
                  pl.BlockSpec((None, SUBLANES, DN_CONV_DIM), lambda bi, i: (bi, jnp.maximum(i * nb - 1, 0), 0)),
                  pl.BlockSpec((None, SUBLANES, DN_CONV_DIM), lambda bi, i: (bi, jnp.minimum((i + 1) * nb, last), 0)),
                  pl.BlockSpec((None, tm, LANES), lambda bi, i: (bi, i, 0)),
                  pl.BlockSpec((SUBLANES, DN_CONV_DIM), lambda bi, i: (0, 0)),
                  pl.BlockSpec((1, LANES), lambda bi, i: (0, 0)),
                  pl.BlockSpec((1, LANES), lambda bi, i: (0, 0))],
        out_specs=[pl.BlockSpec((None, tm, DN_QK), lambda bi, i: (bi, i, 0))] * 3
        + [pl.BlockSpec((None, tm, LANES), lambda bi, i: (bi, i, 0))],
        out_shape=[jax.ShapeDtypeStruct((b, t, DN_QK), F32)] * 3 + [jax.ShapeDtypeStruct((b, t, LANES), F32)],
        scratch_shapes=[pltpu.VMEM((tm + 2 * SUBLANES, DN_CONV_DIM), F32)],
        compiler_params=_params(("parallel", "parallel")),
        name="dn_prep",
    )(qkv, qkv, qkv, ba, cw, alog_row, dtb_row)


def _mm(a, b):
    return jnp.dot(a.astype(BF16), b.astype(BF16), preferred_element_type=F32)


def _bmm(a, b):
    return jnp.einsum('cij,cjk->cik', a.astype(BF16), b.astype(BF16), preferred_element_type=F32)


def _bmm_nt(a, b):
    return jnp.einsum('cid,cjd->cij', a.astype(BF16), b.astype(BF16), preferred_element_type=F32)


def _dn_masks(d, pack=1):
    ii = lax.broadcasted_iota(jnp.int32, (CHUNK, pack * CHUNK), 0)
    jj = lax.broadcasted_iota(jnp.int32, (CHUNK, pack * CHUNK), 1) % CHUNK
    if d == 1:
        ii, jj = jj, ii
    eye = jnp.where(ii == jj, 1.0, 0.0).astype(F32)
    merge = []
    s = 1
    while s < CHUNK:
        merge.append((ii // (2 * s) == jj // (2 * s)) & ((ii // s) % 2 == 1) & ((jj // s) % 2 == 0))
        s *= 2
    return ii >= jj, ii > jj, eye, merge


def _dn_chunk_terms(streams, cb, out):
    masks = {d: _dn_masks(d) for d in sorted({s[0] for s in streams})}
    data = []
    for d, hds, (q_ref, k_ref, v_ref, gb_ref, gr_ref) in streams:
        gb = gb_ref[...]
        gr = gr_ref[...]

        def heads(ref):
            return jnp.concatenate([ref[:, hd * DN_HEAD:(hd + 1) * DN_HEAD].reshape(cb, CHUNK, DN_HEAD)
                                    for hd in hds], axis=0)

        def gate_cols(lane0):
            return jnp.concatenate([gb[:, lane0 + hd:lane0 + hd + 1].reshape(cb, CHUNK, 1) for hd in hds], axis=0)

        q, k, v = heads(q_ref), heads(k_ref), heads(v_ref)
        beta = gate_cols(BETA_LANE + d * DN_HEADS)
        gcol = gate_cols(DECAY_LANE + d * DN_HEADS)
        gl = DECAY_LANE + d * DN_HEADS
        grow = jnp.concatenate([gr[:, gl + hd:gl + hd + 1, :] for hd in hds], axis=0)
        glast = grow[:, :, CHUNK - 1:CHUNK] if d == 0 else grow[:, :, 0:1]
        decay = jnp.where(masks[d][0], jnp.exp(jnp.minimum(gcol - grow, 0.0)), 0.0)
        data.append(dict(d=d, q=q, k=k, v=v, beta=beta, gcol=gcol, glast=glast, decay=decay, kb=k * beta))
    n = range(len(data))
    lower = [jnp.where(masks[x['d']][1], _bmm_nt(x['kb'], x['k']) * x['decay'], 0.0) for x in data]
    wmask = {d: _dn_masks(d, DN_PACK) for d in masks}
    unit = lax.broadcasted_iota(jnp.int32, (CHUNK, DN_PACK * CHUNK), 1) // CHUNK

    def wide(x):
        x4 = x.reshape(x.shape[0] // DN_PACK, DN_PACK, CHUNK, CHUNK)
        return jnp.concatenate([x4[:, j] for j in range(DN_PACK)], axis=-1)

    def block_diag(xw):
        return jnp.concatenate([jnp.where(unit == j, xw, 0.0) for j in range(DN_PACK)], axis=1)

    yield
    lower_w = [wide(lower[i]) for i in n]
    tinv = [wmask[data[i]['d']][2] - jnp.where(wmask[data[i]['d']][3][0], lower_w[i], 0.0) for i in n]
    for lvl in range(1, len(wmask[data[0]['d']][3])):
        te = [_bmm(tinv[i], block_diag(jnp.where(wmask[data[i]['d']][3][lvl], lower_w[i], 0.0))) for i in n]
        yield
        tinv = [tinv[i] - _bmm(te[i], block_diag(tinv[i])) for i in n]
        yield
    sol = []
    for i, x in enumerate(data):
        rhs = jnp.concatenate([x['v'] * x['beta'], x['kb'] * jnp.exp(x['gcol'])], axis=-1)
        units = rhs.shape[0]
        s_tall = _bmm(block_diag(tinv[i]), rhs.reshape(units // DN_PACK, DN_PACK * CHUNK, 2 * DN_HEAD))
        sol.append(s_tall.reshape(units, CHUNK, 2 * DN_HEAD))
    yield
    for i, x in enumerate(data):
        u, w = sol[i][..., :DN_HEAD], sol[i][..., DN_HEAD:]
        attn = (_bmm_nt(x['q'], x['k']) * x['decay']).astype(BF16)
        wq = jnp.concatenate([w, x['q'] * jnp.exp(x['gcol'])], axis=1).astype(BF16)
        kdec = (x['k'] * jnp.exp(x['glast'] - x['gcol'])).astype(BF16)
        out.append((u, wq, attn, kdec, jnp.exp(x['glast'])))


def _dn_scan(streams, slot, bufs, o_refs, s_ref, cb):
    u_buf, wq_buf, attn_buf, kdec_buf, eg_buf = bufs
    units = DN_STREAM_HEADS * cb
    for step in range(cb):
        chains = []
        for si, (d, hds, _) in enumerate(streams):
            c = step if d == 0 else cb - 1 - step
            chains += [(d, hd, c, si * units + i * cb + c) for i, hd in enumerate(hds)]
        states = [s_ref[d * DN_HEADS + hd] for d, hd, _, _ in chains]
        ws = [_mm(wq_buf[slot, n], st) for (_, _, _, n), st in zip(chains, states)]
        yield
        for (d, hd, c, n), st, w in zip(chains, states, ws):
            v_new = (u_buf[slot, n] - w[:CHUNK]).astype(BF16)
            o = w[CHUNK:] + _mm(attn_buf[slot, n], v_new)
            s_ref[d * DN_HEADS + hd] = st * eg_buf[slot, n] + lax.dot_general(
                kdec_buf[slot, n], v_new, (((0,), (0,)), ((), ())), preferred_element_type=F32)
            o_refs[d][c * CHUNK:(c + 1) * CHUNK, hd * DN_HEAD:(hd + 1) * DN_HEAD] = o.astype(o_refs[d].dtype)
        yield


def _dn_main_kernel(qf, kf, vf, gbf, grf, qb, kb, vb, gbb, grb, of_ref, ob_ref, s_ref,
                    u_buf, wq_buf, attn_buf, kdec_buf, eg_buf, *, cb):
    j = pl.program_id(1)
    bufs = (u_buf, wq_buf, attn_buf, kdec_buf, eg_buf)

    @pl.when(j == 0)
    def _():
        s_ref[...] = jnp.zeros_like(s_ref)
        for buf in bufs:
            buf[1] = jnp.zeros(buf.shape[1:], buf.dtype)

    cur = (j + 1) % 2
    nxt = j % 2
    groups = [tuple(range(h, h + DN_STREAM_HEADS)) for h in range(0, DN_HEADS, DN_STREAM_HEADS)]
    streams = ([(0, hds, (qf, kf, vf, gbf, grf)) for hds in groups]
               + [(1, hds, (qb, kb, vb, gbb, grb)) for hds in groups])
    units = DN_STREAM_HEADS * cb
    pre = []
    terms = _dn_chunk_terms(streams, cb, pre)
    scan = _dn_scan(streams, cur, bufs, (of_ref, ob_ref), s_ref, cb)
    live = [terms, scan]
    while live:
        for gen in (terms, scan):
            if gen in live and next(gen, StopIteration) is StopIteration:
                live.remove(gen)
    for si, (u, wq, attn, kdec, eg) in enumerate(pre):
        lo, hi = si * units, (si + 1) * units
        u_buf[nxt, lo:hi] = u
        wq_buf[nxt, lo:hi] = wq
        attn_buf[nxt, lo:hi] = attn
        kdec_buf[nxt, lo:hi] = kdec
        eg_buf[nxt, lo:hi] = jnp.broadcast_to(eg, (units, 1, DN_HEAD))


def _dn_main(qn, kn, vv, gb, gr, cb):
    b, t, _ = qn.shape
    cbt = cb * CHUNK
    ng = t // cbt
    fwd_in = lambda bi, j: (bi, jnp.minimum(j, ng - 1), 0)
    bwd_in = lambda bi, j: (bi, ng - 1 - jnp.minimum(j, ng - 1), 0)
    fwd_in4 = lambda bi, j: (bi, jnp.minimum(j, ng - 1), 0, 0)
    bwd_in4 = lambda bi, j: (bi, ng - 1 - jnp.minimum(j, ng - 1), 0, 0)
    fwd_out = lambda bi, j: (bi, jnp.maximum(j - 1, 0), 0)
    bwd_out = lambda bi, j: (bi, ng - 1 - jnp.maximum(j - 1, 0), 0)

    def specs(im3, im4):
        return ([pl.BlockSpec((None, cbt, DN_QK), im3)] * 3
                + [pl.BlockSpec((None, cbt, LANES), im3), pl.BlockSpec((None, cb, 2 * SUBLANES, CHUNK), im4)])

    units = N_DIR * DN_HEADS * cb
    kern = functools.partial(_dn_main_kernel, cb=cb)
    return pl.pallas_call(
        kern,
        grid=(b, ng + 1),
        in_specs=specs(fwd_in, fwd_in4) + specs(bwd_in, bwd_in4),
        out_specs=[pl.BlockSpec((None, cbt, DN_QK), fwd_out), pl.BlockSpec((None, cbt, DN_QK), bwd_out)],
        out_shape=[jax.ShapeDtypeStruct((b, t, DN_QK), BF16)] * 2,
        scratch_shapes=[pltpu.VMEM((N_DIR * DN_HEADS, DN_HEAD, DN_HEAD), F32),
                        pltpu.VMEM((2, units, CHUNK, DN_HEAD), F32),
                        pltpu.VMEM((2, units, 2 * CHUNK, DN_HEAD), BF16),
                        pltpu.VMEM((2, units, CHUNK, CHUNK), BF16),
                        pltpu.VMEM((2, units, CHUNK, DN_HEAD), BF16),
                        pltpu.VMEM((2, units, 1, DN_HEAD), F32)],
        compiler_params=_params(("parallel", "arbitrary")),
        name="dn_main",
    )(qn, kn, vv, gb, gr, qn, kn, vv, gb, gr)


def _rope_norm(x, w, seg, cos, sin):
    sq = x * x
    sq_hi = sq.astype(BF16)
    sq_lo = (sq - sq_hi.astype(F32)).astype(BF16)
    ss = jnp.dot(sq_hi, seg, preferred_element_type=F32) + jnp.dot(sq_lo, seg, preferred_element_type=F32)
    xn = x * lax.rsqrt(ss * (1.0 / ATT_HEAD_DIM) + EPS) * w
    quarter = ATT_HEAD_DIM // 4
    outs = []
    for t in range(x.shape[-1] // LANES):
        xt = xn[:, t * LANES:(t + 1) * LANES]
        lane = lax.broadcasted_iota(jnp.int32, xt.shape, 1)
        rot = jnp.where(lane % (2 * quarter) < quarter,
                        pltpu.roll(xt, LANES - quarter, axis=1), pltpu.roll(xt, quarter, axis=1))
        outs.append(xt * cos + rot * sin)
    return outs[0] if len(outs) == 1 else jnp.concatenate(outs, axis=-1)


def _attn_kernel(qt_ref, k_ref, vt_ref, o_ref, s_buf, p_buf, qt_buf, acc_buf, *, tk):
    tq = qt_ref.shape[2]
    m_cols = ATT_GROUP * tq
    for g in range(ATT_GROUP):
        qt_buf[:, g * tq:(g + 1) * tq] = qt_ref[g]
    nk = k_ref.shape[0] // tk
    ones = jnp.ones((2 * SUBLANES, tk), BF16)

    def scores(j, slot):
        start = pl.multiple_of(j * tk, tk)
        s = jnp.dot(k_ref[pl.ds(start, tk), :], qt_buf[...], preferred_element_type=F32)
        s_buf[slot] = s
        return jnp.max(jnp.max(s.reshape(tk // SUBLANES, SUBLANES, m_cols), axis=0), axis=0, keepdims=True)

    def softmax(slot, m, chunk_max):
        m_new = jnp.maximum(m, chunk_max)
        p_buf[slot] = jnp.exp2(s_buf[slot] - m_new).astype(BF16)
        return m_new, jnp.exp2(m - m_new)

    def values(j, slot, alpha):
        start = pl.multiple_of(j * tk, tk)
        vt = jnp.concatenate([vt_ref[:, pl.ds(start, tk)], ones], axis=0)
        acc_buf[...] = alpha * acc_buf[...] + jnp.dot(vt, p_buf[slot], preferred_element_type=F32)

    m = jnp.full((1, m_cols), jnp.finfo(F32).min, F32)
    acc_buf[...] = jnp.zeros(acc_buf.shape, F32)
    cmax0 = scores(0, 0)
    cmax1 = scores(1, 1)
    m, alpha0 = softmax(0, m, cmax0)
    cmax2 = scores(2, 0)
    m, alpha1 = softmax(1, m, cmax1)
    values(0, 0, alpha0)

    def body(i, carry):
        m, cmax, alpha_prev = carry
        j0 = 2 * i
        cmax1 = scores(j0 + 1, 1)
        m, alpha0 = softmax(0, m, cmax)
        values(j0 - 1, 1, alpha_prev)
        cmax2 = scores(j0 + 2, 0)
        m, alpha1 = softmax(1, m, cmax1)
        values(j0, 0, alpha0)
        return m, cmax2, alpha1

    m, cmax, alpha_prev = lax.fori_loop(1, nk // 2 - 1, body, (m, cmax2, alpha1))
    cmax1 = scores(nk - 1, 1)
    m, alpha0 = softmax(0, m, cmax)
    values(nk - 3, 1, alpha_prev)
    m, alpha = softmax(1, m, cmax1)
    values(nk - 2, 0, alpha0)
    values(nk - 1, 1, alpha)
    acc = acc_buf[...]
    o = acc[:ATT_HEAD_DIM] / acc[ATT_HEAD_DIM:ATT_HEAD_DIM + 1]
    o = jnp.concatenate([o[:, g * tq:(g + 1) * tq] for g in range(ATT_GROUP)], axis=0)
    o_ref[...] = o.T.astype(o_ref.dtype)


def _attention(qt6, k4, vt4, tk):
    b, _, nq, _, _, tq = qt6.shape
    t = k4.shape[2]
    m_cols = ATT_GROUP * tq
    assert t % (2 * tk) == 0 and t // tk >= 4
    kern = functools.partial(_attn_kernel, tk=tk)
    return pl.pallas_call(
        kern,
        grid=(b, ATT_KV_HEADS, nq),
        in_specs=[pl.BlockSpec((None, None, None, ATT_GROUP, ATT_HEAD_DIM, tq), lambda bi, h, i: (bi, h, i, 0, 0, 0)),
                  pl.BlockSpec((None, None, t, ATT_HEAD_DIM), lambda bi, h, i: (bi, h, 0, 0)),
                  pl.BlockSpec((None, None, ATT_HEAD_DIM, t), lambda bi, h, i: (bi, h, 0, 0))],
        out_specs=pl.BlockSpec((None, tq, ATT_GROUP * ATT_HEAD_DIM), lambda bi, h, i: (bi, i, h)),
        out_shape=jax.ShapeDtypeStruct((b, t, ATT_Q), BF16),
        scratch_shapes=[pltpu.VMEM((2, tk, m_cols), F32), pltpu.VMEM((2, tk, m_cols), BF16),
                        pltpu.VMEM((ATT_HEAD_DIM, m_cols), BF16), pltpu.VMEM((ATT_HEAD_DIM + 2 * SUBLANES, m_cols), F32)],
        compiler_params=_params(("parallel", "parallel", "parallel")),
        name="attention",
    )(qt6, k4, vt4)


def _out_proj_kernel(of_ref, ob_ref, dnz_ref, dnw_ref, oat_ref, atz_ref, x_ref, w_ref, fw_ref, y_ref, *, final):
    o = of_ref[...].astype(F32) + ob_ref[...].astype(F32)
    parts = []
    for hd in range(DN_HEADS):
        oh = o[:, hd * DN_HEAD:(hd + 1) * DN_HEAD]
        parts.append(oh * lax.rsqrt(jnp.mean(oh * oh, axis=-1, keepdims=True) + EPS) * dnw_ref[...])
    y_dn = jnp.concatenate(parts, axis=-1) * _silu(dnz_ref[...].astype(F32))
    y_at = oat_ref[...].astype(F32) * _silu(atz_ref[...].astype(F32))
    y = jnp.concatenate([y_dn, y_at], axis=-1).astype(BF16)
    out = x_ref[...] + jnp.dot(y, w_ref[...], preferred_element_type=F32)
    if final:
        out = out * lax.rsqrt(jnp.mean(out * out, axis=-1, keepdims=True) + EPS) * fw_ref[...]
    y_ref[...] = out


def _out_proj(o_f, o_b, dnz, dnw_row, o_at, atz, x2, w_out, fw_row, tm, final):
    bt = x2.shape[0]
    row = lambda i: (i, 0)
    const = lambda i: (0, 0)
    kern = functools.partial(_out_proj_kernel, final=final)
    return pl.pallas_call(
        kern,
        grid=(bt // tm,),
        in_specs=[pl.BlockSpec((tm, DN_QK), row), pl.BlockSpec((tm, DN_QK), row), pl.BlockSpec((tm, DN_QK), row),
                  pl.BlockSpec((1, DN_HEAD), const),
                  pl.BlockSpec((tm, ATT_Q), row), pl.BlockSpec((tm, ATT_Q), row),
                  pl.BlockSpec((tm, D_MODEL), row),
                  pl.BlockSpec((D_MODEL, D_MODEL), const), pl.BlockSpec((1, D_MODEL), const)],
        out_specs=pl.BlockSpec((tm, D_MODEL), row),
        out_shape=jax.ShapeDtypeStruct((bt, D_MODEL), F32),
        compiler_params=_params(("parallel",)),
        name="out_proj_final" if final else "out_proj",
    )(o_f, o_b, dnz, dnw_row, o_at, atz, x2, w_out, fw_row)


def _pack_w_in(w):
    splits = np.cumsum(IN_SIZES)[:-1].tolist()
    qkv, dnz, bb, aa, atq, atk, atv, atz = jnp.split(w, splits, axis=-1)
    pad = jnp.zeros((w.shape[0], LANES - 2 * N_DIR * DN_HEADS), w.dtype)
    return jnp.concatenate([atq, atk, atv, qkv, dnz, atz, bb, aa, pad], axis=-1).astype(BF16)


def _rope_tables(t):
    half = ATT_HEAD_DIM // 2
    pos = np.arange(t)
    row = (pos // GRID_W).astype(np.float32)
    col = (pos % GRID_W).astype(np.float32)
    inv = jnp.asarray(ROPE_THETA, F32) ** (-jnp.arange(0, half, 2, dtype=F32) / half)
    ang_r = jnp.asarray(row)[:, None] * inv[None, :]
    ang_c = jnp.asarray(col)[:, None] * inv[None, :]
    cos = jnp.concatenate([jnp.cos(ang_r)] * 2 + [jnp.cos(ang_c)] * 2, axis=-1)
    sin = jnp.concatenate([-jnp.sin(ang_r), jnp.sin(ang_r), -jnp.sin(ang_c), jnp.sin(ang_c)], axis=-1)
    reps = LANES // ATT_HEAD_DIM
    return jnp.tile(cos, (1, reps)), jnp.tile(sin, (1, reps))


def _gate_row(p):
    return jnp.zeros((1, LANES), F32).at[0, DECAY_LANE:DECAY_LANE + N_DIR * DN_HEADS].set(p.reshape(-1))


def kernel(x, norm_w, w_in, conv_w, a_log, dt_bias, dn_norm_w, q_norm_w, k_norm_w, w_out, final_norm_w):
    b, t, d = x.shape
    depth = w_in.shape[0]
    tm = 256
    tm_proj = min(512, b * t)
    cb = min(4, t // CHUNK)
    tq = min(1024, t)
    tk = min(512, t // 4)
    cos_t, sin_t = _rope_tables(t)
    lane = np.arange(ATT_Q)
    seg = jnp.asarray((lane[:, None] // ATT_HEAD_DIM) == (lane[None, :] // ATT_HEAD_DIM), BF16)
    x2 = x.reshape(b * t, d)
    for l in range(depth):
        qkv, dnz, atz, ba, qt6, k4, vt4 = _in_proj(
            x2, norm_w[l][None, :], _pack_w_in(w_in[l]),
            jnp.tile(q_norm_w[l], ATT_HEADS)[None, :], jnp.tile(k_norm_w[l], ATT_KV_HEADS)[None, :],
            seg, cos_t, sin_t, b, t, tm_proj, tq)
        cw = jnp.zeros((SUBLANES, DN_CONV_DIM), F32).at[:CONV_K].set(conv_w[l])
        qn, kn, vv, gb = _dn_prep(qkv.reshape(b, t, -1), ba.reshape(b, t, -1), cw,
                                  _gate_row(a_log[l]), _gate_row(dt_bias[l]), tm)
        gr = gb[..., :2 * SUBLANES].reshape(b, t // CHUNK, CHUNK, 2 * SUBLANES).transpose(0, 1, 3, 2)
        o_f, o_b = _dn_main(qn, kn, vv, gb, gr, cb)
        o_at = _attention(qt6, k4, vt4, tk).reshape(b * t, ATT_Q)
        x2 = _out_proj(o_f.reshape(b * t, -1), o_b.reshape(b * t, -1), dnz, dn_norm_w[l][None, :], o_at, atz,
                       x2, w_out[l].astype(BF16), final_norm_w[None, :], tm_proj, final=(l == depth - 1))
    return x2.reshape(b, t, d)
```
